```python
import math
import jax, jax.numpy as jnp
from jax import lax
import numpy as np

D_MODEL = 1024
BATCH = 2
SEQ = 8192
DEPTH = 1
DEC_BATCH = 4
DEC_SEQ = 8192
PAST_LEN = 128

ATT_HEADS = 8
HEAD_DIM = 64
ATT_V_DIM = 2 * HEAD_DIM
ATT_QK_WIDTH = ATT_HEADS * 2 * HEAD_DIM
ATT_V_WIDTH = ATT_HEADS * ATT_V_DIM
ROPE_THETA = 10000.0
Q_BLOCK = 128
NORM_EPS = 1e-6

RWKV_HEAD = 64
RWKV_HEADS = D_MODEL // RWKV_HEAD
RWKV_WIDTH = RWKV_HEADS * RWKV_HEAD
DECAY_LORA = 64
AAA_LORA = 64
GATE_LORA = 128
RWKV_LN_EPS = 64e-5

ATT_COLS = 2 * ATT_QK_WIDTH + ATT_V_WIDTH
RWKV_COLS = 3 * RWKV_WIDTH + 2 * DECAY_LORA + 2 * AAA_LORA + GATE_LORA
GATE_COLS = 2 * D_MODEL
IN_COLS = ATT_COLS + RWKV_COLS + GATE_COLS

PEER_HEADS = 8
PEER_NKEYS = 128
PEER_EXPERTS = PEER_NKEYS * PEER_NKEYS
PEER_QDIM = 256
PEER_SUBDIM = PEER_QDIM // 2
PEER_TOPK = 16
PEER_TOKEN_BLOCK = 128

kernel_name = 'hybrid_diffattn_rwkv7_peer_encoder'


def _lambda_init(layer_idx):
    return 0.8 - 0.6 * math.exp(-0.3 * layer_idx)


def _rms_norm(x, w, eps=NORM_EPS):
    xf = x.astype(jnp.float32)
    y = xf * lax.rsqrt(jnp.mean(xf * xf, axis=-1, keepdims=True) + eps)
    return (y * w.astype(jnp.float32)).astype(x.dtype)


def _rope_tables(S, dtype):
    inv = 1.0 / (ROPE_THETA ** (jnp.arange(0, HEAD_DIM, 2, dtype=jnp.float32) / HEAD_DIM))
    ang = jnp.arange(S, dtype=jnp.float32)[:, None] * inv[None, :]
    return jnp.cos(ang).astype(dtype), jnp.sin(ang).astype(dtype)


def _apply_rope(x, cos, sin):
    x1, x2 = jnp.split(x, 2, axis=-1)
    c = cos[None, :, None, :]
    s = sin[None, :, None, :]
    return jnp.concatenate([x1 * c - x2 * s, x2 * c + x1 * s], axis=-1)


def _centred_shift(z):
    zp = jnp.pad(z, ((0, 0), (1, 1), (0, 0)))
    return 0.5 * (zp[:, :-2] + zp[:, 2:])


def _diff_softmax_attention(q, k, v, lam):
    B, S = q.shape[0], q.shape[1]
    nb = S // Q_BLOCK
    qb = q.reshape(B, nb, Q_BLOCK, 2 * ATT_HEADS, HEAD_DIM).transpose(1, 0, 2, 3, 4)

    def block(qblk):
        s = jnp.einsum('bqhd,bkhd->bhqk', qblk, k).astype(jnp.float32)
        p = jax.nn.softmax(s, axis=-1).reshape(B, ATT_HEADS, 2, Q_BLOCK, S)
        pd = (p[:, :, 0] - lam * p[:, :, 1]).astype(v.dtype)
        return jnp.einsum('bhqk,bkhe->bqhe', pd, v)

    o = lax.map(block, qb)
    return o.transpose(1, 0, 2, 3, 4).reshape(B, S, ATT_HEADS, ATT_V_DIM)


def _attention_branch(za, cos, sin, lambda_init, q_norm_w, k_norm_w, lambda_q1, lambda_k1,
                      lambda_q2, lambda_k2, subln_w, w_o_attn):
    B, S, _ = za.shape
    f32 = jnp.float32
    q = za[..., :ATT_QK_WIDTH].reshape(B, S, 2 * ATT_HEADS, HEAD_DIM)
    k = za[..., ATT_QK_WIDTH:2 * ATT_QK_WIDTH].reshape(B, S, 2 * ATT_HEADS, HEAD_DIM)
    v = za[..., 2 * ATT_QK_WIDTH:].reshape(B, S, ATT_HEADS, ATT_V_DIM)
    q = _apply_rope(_rms_norm(q, q_norm_w), cos, sin) * (HEAD_DIM ** -0.5)
    k = _apply_rope(_rms_norm(k, k_norm_w), cos, sin)
    lam = (jnp.exp(jnp.sum(lambda_q1.astype(f32) * lambda_k1.astype(f32)))
           - jnp.exp(jnp.sum(lambda_q2.astype(f32) * lambda_k2.astype(f32))) + lambda_init)
    o = _diff_softmax_attention(q, k, v, lam)
    o = _rms_norm(o, subln_w) * (1.0 - lambda_init)
    return o.reshape(B, S, ATT_V_WIDTH) @ w_o_attn


def _wkv7_scan(r, w, k, v, kk, a, reverse):
    B, S, H, N = r.shape
    xs = tuple(jnp.moveaxis(t, 1, 0) for t in (r, w, k, v, kk, a))

    def step(state, inp):
        r_t, w_t, k_t, v_t, kk_t, a_t = inp
        sa = jnp.einsum('bhij,bhj->bhi', state, -kk_t)
        state = (state * w_t[:, :, None, :] + sa[..., None] * (kk_t * a_t)[:, :, None, :]
                 + v_t[..., None] * k_t[:, :, None, :])
        y = jnp.einsum('bhij,bhj->bhi', state, r_t)
        return state, y

    s0 = jnp.zeros((B, H, N, N), jnp.float32)
    _, ys = lax.scan(step, s0, xs, reverse=reverse)
    return jnp.moveaxis(ys, 0, 1)


def _rwkv_branch(zr, shift_mu, decay_w0, decay_w2, iclr_a0, iclr_a2, gate_g2, k_k, k_a, r_k,
                 lnx_w, lnx_b, w_o_rwkv):
    B, S, _ = zr.shape
    H, N, C = RWKV_HEADS, RWKV_HEAD, RWKV_WIDTH
    f32 = jnp.float32
    zs = zr + shift_mu * (_centred_shift(zr) - zr)
    o1 = C; o2 = 2 * C; o3 = 3 * C; o4 = o3 + 2 * DECAY_LORA; o5 = o4 + 2 * AAA_LORA
    r, k, v = zs[..., :o1], zs[..., o1:o2], zs[..., o2:o3]
    wl = zs[..., o3:o4].reshape(B, S, 2, DECAY_LORA)
    al = zs[..., o4:o5].reshape(B, S, 2, AAA_LORA)
    gl = zs[..., o5:]
    wpre = (jnp.einsum('bsdr,drc->bsdc', jnp.tanh(wl), decay_w2) + decay_w0).astype(f32)
    decay = jnp.exp(-jnp.exp(-jax.nn.softplus(-wpre) - 0.5))
    a = jax.nn.sigmoid((jnp.einsum('bsdr,drc->bsdc', al, iclr_a2) + iclr_a0).astype(f32))
    g = jax.nn.sigmoid(gl) @ gate_g2
    kf = k.astype(f32)
    kk = (kf * k_k).reshape(B, S, H, N)
    kk = kk * lax.rsqrt(jnp.sum(kk * kk, axis=-1, keepdims=True) + 1e-12)
    k_dir = (kf[:, :, None, :] * (1.0 + (a - 1.0) * k_a)).reshape(B, S, 2, H, N)
    rf = r.astype(f32).reshape(B, S, H, N)
    vf = v.astype(f32).reshape(B, S, H, N)
    decay = decay.reshape(B, S, 2, H, N)
    a = a.reshape(B, S, 2, H, N)
    y_f = _wkv7_scan(rf, decay[:, :, 0], k_dir[:, :, 0], vf, kk, a[:, :, 0], reverse=False)
    y_b = _wkv7_scan(rf, decay[:, :, 1], k_dir[:, :, 1], vf, kk, a[:, :, 1], reverse=True)
    y = y_f + y_b
    mu = jnp.mean(y, axis=-1, keepdims=True)
    var = jnp.mean(jnp.square(y - mu), axis=-1, keepdims=True)
    yn = ((y - mu) * lax.rsqrt(var + RWKV_LN_EPS)).reshape(B, S, C) * lnx_w + lnx_b
    bonus = jnp.sum(rf[:, :, None] * k_dir * r_k.astype(f32), axis=(2, 4))
    out = (yn + (bonus[..., None] * vf).reshape(B, S, C)) * g
    return out.astype(zr.dtype) @ w_o_rwkv


def _peer(x, peer_wq, peer_subkeys, peer_u, peer_v):
    B, S, D = x.shape
    TB, PH, K = PEER_TOKEN_BLOCK, PEER_HEADS, PEER_TOPK
    xt = x.reshape((B * S) // TB, TB, D)

    def block(xb):
        q = (xb @ peer_wq).reshape(TB, PH, 2, PEER_SUBDIM)
        s = jnp.einsum('thcd,hckd->thck', q, peer_subkeys).astype(jnp.float32)
        sv, si = lax.top_k(s, K)
        cand = sv[:, :, 0, :, None] + sv[:, :, 1, None, :]
        cidx = si[:, :, 0, :, None] * PEER_NKEYS + si[:, :, 1, None, :]
        best, pos = lax.top_k(cand.reshape(TB, PH, K * K), K)
        eidx = jnp.take_along_axis(cidx.reshape(TB, PH, K * K), pos, axis=-1)
        gate = jax.nn.softmax(best, axis=-1).astype(xb.dtype)
        hdn = jax.nn.gelu(jnp.einsum('td,thkd->thk', xb, peer_u[eidx]), approximate=False)
        return jnp.einsum('thk,thkd->td', gate * hdn, peer_v[eidx])

    return lax.map(block, xt).reshape(B, S, D)


def _encoder_layer(x, cos, sin, lambda_init, norm1_w, w_in, q_norm_w, k_norm_w, lambda_q1,
                   lambda_k1, lambda_q2, lambda_k2, subln_w, w_o_attn, shift_mu, decay_w0,
                   decay_w2, iclr_a0, iclr_a2, gate_g2, k_k, k_a, r_k, lnx_w, lnx_b, w_o_rwkv,
                   w_out, norm2_w, peer_wq, peer_subkeys, peer_u, peer_v):
    h = _rms_norm(x, norm1_w)
    z = h @ w_in
    za = z[..., :ATT_COLS]
    zr = z[..., ATT_COLS:ATT_COLS + RWKV_COLS]
    zg = z[..., ATT_COLS + RWKV_COLS:]
    h_attn = _attention_branch(za, cos, sin, lambda_init, q_norm_w, k_norm_w, lambda_q1,
                               lambda_k1, lambda_q2, lambda_k2, subln_w, w_o_attn)
    h_rwkv = _rwkv_branch(zr, shift_mu, decay_w0, decay_w2, iclr_a0, iclr_a2, gate_g2, k_k,
                          k_a, r_k, lnx_w, lnx_b, w_o_rwkv)
    g_attn, g_rwkv = jnp.split(zg, 2, axis=-1)
    merged = jax.nn.sigmoid(g_attn) * h_attn + jax.nn.sigmoid(g_rwkv) * h_rwkv
    x = x + merged @ w_out
    x = x + _peer(_rms_norm(x, norm2_w), peer_wq, peer_subkeys, peer_u, peer_v)
    return x


def setup_inputs(seed: int = 0) -> dict:
    key = jax.random.key(seed)
    ks = iter(jax.random.split(key, 40))
    f32 = jnp.float32

    def nrm(shape, scale):
        return jax.random.normal(next(ks), shape, f32) * scale

    L, C = DEPTH, RWKV_WIDTH
    inp = {}
    inp['x_prompt'] = nrm((BATCH, SEQ, D_MODEL), 1.0)
    inp['x_sample'] = nrm((DEC_BATCH, DEC_SEQ, D_MODEL), 1.0)
    inp['norm1_w'] = 1.0 + nrm((L, D_MODEL), 0.02)
    inp['w_in'] = nrm((L, D_MODEL, IN_COLS), D_MODEL ** -0.5)
    inp['q_norm_w'] = 1.0 + nrm((L, HEAD_DIM), 0.02)
    inp['k_norm_w'] = 1.0 + nrm((L, HEAD_DIM), 0.02)
    inp['lambda_q1'] = nrm((L, HEAD_DIM), 0.1)
    inp['lambda_k1'] = nrm((L, HEAD_DIM), 0.1)
    inp['lambda_q2'] = nrm((L, HEAD_DIM), 0.1)
    inp['lambda_k2'] = nrm((L, HEAD_DIM), 0.1)
    inp['subln_w'] = 1.0 + nrm((L, ATT_V_DIM), 0.02)
    inp['w_o_attn'] = nrm((L, ATT_V_WIDTH, D_MODEL), ATT_V_WIDTH ** -0.5)
    inp['shift_mu'] = jax.random.uniform(next(ks), (L, RWKV_COLS), f32, 0.0, 1.0)
    inp['decay_w0'] = jax.random.uniform(next(ks), (L, 2, C), f32, -6.0, 0.5)
    inp['decay_w2'] = nrm((L, 2, DECAY_LORA, C), 0.1 * DECAY_LORA ** -0.5)
    inp['iclr_a0'] = nrm((L, 2, C), 0.5)
    inp['iclr_a2'] = nrm((L, 2, AAA_LORA, C), AAA_LORA ** -0.5)
    inp['gate_g2'] = nrm((L, GATE_LORA, C), GATE_LORA ** -0.5)
    inp['k_k'] = 0.85 + nrm((L, C), 0.02)
    inp['k_a'] = 1.0 + nrm((L, C), 0.02)
    inp['r_k'] = nrm((L, RWKV_HEADS, RWKV_HEAD), 0.1)
    inp['lnx_w'] = 1.0 + nrm((L, C), 0.02)
    inp['lnx_b'] = nrm((L, C), 0.02)
    inp['w_o_rwkv'] = nrm((L, C, D_MODEL), C ** -0.5)
    inp['w_out'] = nrm((L, D_MODEL, D_MODEL), D_MODEL ** -0.5)
    inp['norm2_w'] = 1.0 + nrm((L, D_MODEL), 0.02)
    inp['peer_wq'] = nrm((L, D_MODEL, PEER_HEADS * PEER_QDIM), D_MODEL ** -0.5)
    inp['peer_subkeys'] = nrm((L, PEER_HEADS, 2, PEER_NKEYS, PEER_SUBDIM), PEER_SUBDIM ** -0.5)
    inp['peer_u'] = nrm((L, PEER_EXPERTS, D_MODEL), D_MODEL ** -0.5)
    inp['peer_v'] = nrm((L, PEER_EXPERTS, D_MODEL), 0.3)
    return inp


def reference(x_prompt, x_sample, norm1_w, w_in, q_norm_w, k_norm_w, lambda_q1, lambda_k1,
              lambda_q2, lambda_k2, subln_w, w_o_attn, shift_mu, decay_w0, decay_w2, iclr_a0,
              iclr_a2, gate_g2, k_k, k_a, r_k, lnx_w, lnx_b, w_o_rwkv, w_out, norm2_w, peer_wq,
              peer_subkeys, peer_u, peer_v):
    def trunk(x):
        cos, sin = _rope_tables(x.shape[1], x.dtype)
        for l in range(DEPTH):
            x = _encoder_layer(x, cos, sin, _lambda_init(l), norm1_w[l], w_in[l], q_norm_w[l],
                               k_norm_w[l], lambda_q1[l], lambda_k1[l], lambda_q2[l],
                               lambda_k2[l], subln_w[l], w_o_attn[l], shift_mu[l], decay_w0[l],
                               decay_w2[l], iclr_a0[l], iclr_a2[l], gate_g2[l], k_k[l], k_a[l],
                               r_k[l], lnx_w[l], lnx_b[l], w_o_rwkv[l], w_out[l], norm2_w[l],
                               peer_wq[l], peer_subkeys[l], peer_u[l], peer_v[l])
        return x

    y_prompt = trunk(x_prompt)
    y_sample = trunk(x_sample)
    return (y_prompt, y_sample)
```

```python
import functools
import math

import jax
import jax.numpy as jnp
from jax import lax
from jax.experimental import pallas as pl
from jax.experimental.pallas import tpu as pltpu

F32 = jnp.float32
BF16 = jnp.bfloat16

LANES = 128
HEAD_DIM = 64
ATT_HEADS = 8
NORM_EPS = 1e-6
RWKV_LN_EPS = 64e-5
ROPE_THETA = 10000.0
LAMBDA_INIT = 0.8 - 0.6 * math.exp(-0.3 * 0)
SCAN_CHUNK = 64
PEER_NKEYS = 128
PEER_TOPK = 16
PEER_HEADS = 8
VMEM_LIMIT = 56 * 1024 * 1024


def _params(sem):
    return pltpu.CompilerParams(dimension_semantics=sem, vmem_limit_bytes=VMEM_LIMIT)


def _dot(a, b):
    return jnp.dot(a, b, preferred_element_type=F32)


def _dot_nt(a, b):
    return lax.dot_general(a, b, (((1,), (1,)), ((), ())), preferred_element_type=F32)


def _split_dot(x, m):
    hi = x.astype(BF16)
    lo = (x - hi.astype(F32)).astype(BF16)
    return _dot(hi, m) + _dot(lo, m)


def _seg_mats(width, seg):
    lane = lax.broadcasted_iota(jnp.int32, (width, LANES), 0)
    grp = lax.broadcasted_iota(jnp.int32, (width, LANES), 1)
    e = jnp.where(lane // seg == grp, 1.0, 0.0).astype(BF16)
    grp_t = lax.broadcasted_iota(jnp.int32, (LANES, width), 0)
    lane_t = lax.broadcasted_iota(jnp.int32, (LANES, width), 1)
    et = jnp.where(lane_t // seg == grp_t, 1.0, 0.0).astype(BF16)
    return e, et


def _seg_sum(x, e, et):
    return _split_dot(_split_dot(x, e), et)


def _sigmoid(x):
    return 1.0 / (1.0 + jnp.exp(-x))


def _norm_matmul_kernel(x_ref, nw_ref, w_ref, o_ref):
    x = x_ref[...]
    ms = jnp.mean(x * x, axis=-1, keepdims=True)
    h = (x * lax.rsqrt(ms + NORM_EPS) * nw_ref[...]).astype(BF16)
    o_ref[...] = _dot(h, w_ref[...]).astype(o_ref.dtype)


def _norm_matmul(x, nw, w, tm):
    t, d = x.shape
    n = w.shape[1]
    return pl.pallas_call(
        _norm_matmul_kernel,
        grid=(t // tm,),
        in_specs=[pl.BlockSpec((tm, d), lambda i: (i, 0)),
                  pl.BlockSpec((1, d), lambda i: (0, 0)),
                  pl.BlockSpec((d, n), lambda i: (0, 0))],
        out_specs=pl.BlockSpec((tm, n), lambda i: (i, 0)),
        out_shape=jax.ShapeDtypeStruct((t, n), BF16),
        compiler_params=_params(("parallel",)),
        name="norm_matmul",
    )(x, nw, w)


def _attn_prep_kernel(z_ref, cos_ref, sin_ref, w_ref, o_ref, *, scale):
    x = z_ref[...].astype(F32)
    width = x.shape[1]
    e, et = _seg_mats(width, HEAD_DIM)
    ms = _seg_sum(x * x, e, et) * (1.0 / HEAD_DIM)
    y = x * lax.rsqrt(ms + NORM_EPS) * w_ref[...]
    reps = width // LANES
    c = jnp.concatenate([cos_ref[...]] * reps, axis=1)
    s = jnp.concatenate([sin_ref[...]] * reps, axis=1)
    lane = lax.broadcasted_iota(jnp.int32, y.shape, 1)
    half = HEAD_DIM // 2
    partner = jnp.where(lane % HEAD_DIM < half, pltpu.roll(y, width - half, 1), pltpu.roll(y, half, 1))
    o_ref[...] = ((y * c + partner * s) * scale).astype(o_ref.dtype)


def _attn_prep(za, cos_t, sin_t, w_tiled, col_block, scale, tm):
    b, s, _ = za.shape
    width = w_tiled.shape[1]
    return pl.pallas_call(
        functools.partial(_attn_prep_kernel, scale=scale),
        grid=(b, s // tm),
        in_specs=[pl.BlockSpec((None, tm, width), lambda bi, i: (bi, i, col_block)),
                  pl.BlockSpec((tm, LANES), lambda bi, i: (i, 0)),
                  pl.BlockSpec((tm, LANES), lambda bi, i: (i, 0)),
                  pl.BlockSpec((1, width), lambda bi, i: (0, 0))],
        out_specs=pl.BlockSpec((None, tm, width), lambda bi, i: (bi, i, 0)),
        out_shape=jax.ShapeDtypeStruct((b, s, width), BF16),
        compiler_params=_params(("parallel", "parallel")),
        name="attn_prep",
    )(za, cos_t, sin_t, w_tiled)


def _attn_kernel(lam_ref, q_ref, k_ref, v_ref, sw_ref, o_ref, qs_ref, m_ref, l_ref, acc_ref, *, tq):
    ki = pl.program_id(3)

    @pl.when(ki == 0)
    def _():
        q = q_ref[...]
        lane = lax.broadcasted_iota(jnp.int32, q.shape, 1)
        zero = jnp.zeros_like(q)
        qs_ref[0:tq, :] = jnp.where(lane < HEAD_DIM, q, zero)
        qs_ref[tq:2 * tq, :] = jnp.where(lane < HEAD_DIM, zero, q)
        m_ref[...] = jnp.full(m_ref.shape, -jnp.inf, F32)
        l_ref[...] = jnp.zeros(l_ref.shape, F32)
        acc_ref[...] = jnp.zeros(acc_ref.shape, F32)

    s = _dot_nt(qs_ref[...], k_ref[...])
    m_prev = m_ref[...]
    m_new = jnp.maximum(m_prev, jnp.max(s, axis=1, keepdims=True))
    alpha = jnp.exp(m_prev - m_new)
    p = jnp.exp(s - m_new)
    l_ref[...] = alpha * l_ref[...] + jnp.sum(p, axis=1, keepdims=True)
    acc_ref[...] = alpha * acc_ref[...] + _dot(p.astype(BF16), v_ref[...])
    m_ref[...] = m_new

    @pl.when(ki == pl.num_programs(3) - 1)
    def _():
        lv = lam_ref[...]
        lam = (jnp.exp(jnp.sum(lv[0:1] * lv[1:2], axis=1, keepdims=True))
               - jnp.exp(jnp.sum(lv[2:3] * lv[3:4], axis=1, keepdims=True)) + LAMBDA_INIT)
        acc = acc_ref[...]
        l = l_ref[...]
        o = acc[0:tq] / l[0:tq] - lam * (acc[tq:2 * tq] / l[tq:2 * tq])
        ms = jnp.mean(o * o, axis=-1, keepdims=True)
        o = o * lax.rsqrt(ms + NORM_EPS) * sw_ref[...] * (1.0 - LAMBDA_INIT)
        o_ref[...] = o.astype(o_ref.dtype)


def _attention(lam_rows, q, k, za, subln_w, tq, tk):
    b, s, width = q.shape
    heads = width // LANES
    v_col0 = 2 * heads
    return pl.pallas_call(
        functools.partial(_attn_kernel, tq=tq),
        grid=(b, heads, s // tq, s // tk),
        in_specs=[pl.BlockSpec((4, HEAD_DIM), lambda bi, h, qi, ki: (0, 0)),
                  pl.BlockSpec((None, tq, LANES), lambda bi, h, qi, ki: (bi, qi, h)),
                  pl.BlockSpec((None, tk, LANES), lambda bi, h, qi, ki: (bi, ki, h)),
                  pl.BlockSpec((None, tk, LANES), lambda bi, h, qi, ki: (bi, ki, v_col0 + h)),
                  pl.BlockSpec((1, LANES), lambda bi, h, qi, ki: (0, 0))],
        out_specs=pl.BlockSpec((None, tq, LANES), lambda bi, h, qi, ki: (bi, qi, h)),
        out_shape=jax.ShapeDtypeStruct((b, s, width), BF16),
        scratch_shapes=[pltpu.VMEM((2 * tq, LANES), BF16),
                        pltpu.VMEM((2 * tq, 1), F32),
                        pltpu.VMEM((2 * tq, 1), F32),
                        pltpu.VMEM((2 * tq, LANES), F32)],
        compiler_params=_params(("parallel", "parallel", "parallel", "arbitrary")),
        name="diff_attention",
    )(lam_rows, q, k, za, subln_w)


HALO = 16


def _rwkv_prep_kernel(z_ref, zp_ref, zn_ref, mu_ref, w0_ref, w2_ref, a0_ref, a2_ref, g2_ref,
                      kk_w_ref, ka_w_ref, rk_ref,
                      r_ref, v_ref, kk_ref, kd0_ref, kd1_ref, ka0_ref, ka1_ref, lw0_ref, lw1_ref,
                      g_ref, bv_ref, *, c):
    i = pl.program_id(1)
    z = z_ref[...].astype(F32)
    tm = z.shape[0]
    row = lax.broadcasted_iota(jnp.int32, z.shape, 0)
    prev_row = jnp.where(i == 0, 0.0, zp_ref[HALO - 1:HALO, :].astype(F32))
    next_row = jnp.where(i == pl.num_programs(1) - 1, 0.0, zn_ref[0:1, :].astype(F32))
    z_up = jnp.where(row == 0, prev_row, pltpu.roll(z, 1, 0))
    z_dn = jnp.where(row == tm - 1, next_row, pltpu.roll(z, tm - 1, 0))
    zs = z + mu_ref[...] * (0.5 * (z_up + z_dn) - z)

    r = zs[:, 0:c]
    k = zs[:, c:2 * c]
    v = zs[:, 2 * c:3 * c]
    wl = zs[:, 3 * c:3 * c + LANES]
    al = zs[:, 3 * c + LANES:3 * c + 2 * LANES]
    gl = zs[:, 3 * c + 2 * LANES:3 * c + 3 * LANES]

    lane = lax.broadcasted_iota(jnp.int32, wl.shape, 1)
    twl = jnp.tanh(wl)
    e, et = _seg_mats(c, HEAD_DIM)

    kk = k * kk_w_ref[...]
    kk = kk * lax.rsqrt(_seg_sum(kk * kk, e, et) + 1e-12)
    kk_ref[...] = kk.astype(kk_ref.dtype)
    r_ref[...] = r.astype(r_ref.dtype)
    v_ref[...] = v.astype(v_ref.dtype)
    g_ref[...] = _dot(_sigmoid(gl).astype(BF16), g2_ref[...]).astype(g_ref.dtype)

    ksum = jnp.zeros_like(k)
    for d, (kd_ref, ka_ref, lw_ref) in enumerate(((kd0_ref, ka0_ref, lw0_ref), (kd1_ref, ka1_ref, lw1_ref))):
        sel = (lane < HEAD_DIM) if d == 0 else (lane >= HEAD_DIM)
        wpre = _dot(jnp.where(sel, twl, 0.0).astype(BF16), w2_ref[...]) + w0_ref[d:d + 1, :]
        lw_ref[...] = (-math.exp(-0.5)) * _sigmoid(wpre)
        a = _sigmoid(_dot(jnp.where(sel, al, 0.0).astype(BF16), a2_ref[...]) + a0_ref[d:d + 1, :])
        kd = k * (1.0 + (a - 1.0) * ka_w_ref[...])
        kd_ref[...] = kd.astype(kd_ref.dtype)
        ka_ref[...] = (kk * a).astype(ka_ref.dtype)
        ksum = ksum + kd
    bonus = _seg_sum(r * ksum * rk_ref[...], e, et)
    bv_ref[...] = (bonus * v).astype(bv_ref.dtype)


def _rwkv_prep(zr, mu, w0, w2, a0, a2, g2, kk_w, ka_w, rk, c, tm):
    b, s, cols = zr.shape
    nh = tm // HALO
    last = s // HALO - 1
    tok = lambda bi, i: (bi, i, 0)
    const = lambda bi, i: (0, 0)
    out = lambda dt: jax.ShapeDtypeStruct((b, s, c), dt)
    ospec = pl.BlockSpec((None, tm, c), tok)
    return pl.pallas_call(
        functools.partial(_rwkv_prep_kernel, c=c),
        grid=(b, s // tm),
        in_specs=[pl.BlockSpec((None, tm, cols), tok),
                  pl.BlockSpec((None, HALO, cols), lambda bi, i: (bi, jnp.maximum(i * nh - 1, 0), 0)),
                  pl.BlockSpec((None, HALO, cols), lambda bi, i: (bi, jnp.minimum((i + 1) * nh, last), 0)),
                  pl.BlockSpec((1, cols), const),
                  pl.BlockSpec((2, c), const),
                  pl.BlockSpec((LANES, c), const),
                  pl.BlockSpec((2, c), const),
                  pl.BlockSpec((LANES, c), const),
                  pl.BlockSpec((LANES, c), const),
                  pl.BlockSpec((1, c), const),
                  pl.BlockSpec((1, c), const),
                  pl.BlockSpec((1, c), const)],
        out_specs=[ospec] * 11,
        out_shape=[out(BF16)] * 7 + [out(F32)] * 2 + [out(BF16)] * 2,
        compiler_params=_params(("parallel", "parallel")),
        name="rwkv_prep",
    )(zr, zr, zr, mu, w0, w2, a0, a2, g2, kk_w, ka_w, rk)


def _wkv_scan_kernel(lwf_ref, rf_ref, vf_ref, kkf_ref, kaf_ref, kdf_ref,
                     lwb_ref, rb_ref, vb_ref, kkb_ref, kab_ref, kdb_ref,
                     yf_ref, yb_ref, st_ref):
    L = SCAN_CHUNK
    P2 = 2 * L
    ci = pl.program_id(1)

    @pl.when(ci == 0)
    def _():
        st_ref[...] = jnp.zeros(st_ref.shape, F32)

    row = lax.broadcasted_iota(jnp.int32, (P2, P2), 0)
    col = lax.broadcasted_iota(jnp.int32, (P2, P2), 1)
    same_head = (row // L) == (col // L)
    rt = row % L
    ct = col % L
    eye = row == col
    ti = lax.broadcasted_iota(jnp.int32, (L, L), 0)
    tj = lax.broadcasted_iota(jnp.int32, (L, L), 1)
    lane_a = lax.broadcasted_iota(jnp.int32, (L, LANES), 1) < HEAD_DIM

    def expand(x):
        return jnp.concatenate([jnp.where(lane_a, x, 0.0), jnp.where(lane_a, 0.0, x)], axis=0)

    dirs = ((lwf_ref, rf_ref, vf_ref, kkf_ref, kaf_ref, kdf_ref, yf_ref),
            (lwb_ref, rb_ref, vb_ref, kkb_ref, kab_ref, kdb_ref, yb_ref))
    for d, (lw_ref, r_ref, v_ref, kk_ref, ka_ref, kd_ref, y_ref) in enumerate(dirs):
        if d == 0:
            m_strict = same_head & (rt > ct)
            m_incl = same_head & (rt >= ct)
            tri = jnp.where(tj <= ti, 1.0, 0.0).astype(BF16)
        else:
            m_strict = same_head & (rt < ct)
            m_incl = same_head & (rt <= ct)
            tri = jnp.where(tj >= ti, 1.0, 0.0).astype(BF16)
        lw = lw_ref[...]
        hi = lw.astype(BF16)
        r1 = lw - hi.astype(F32)
        mid = r1.astype(BF16)
        lo = (r1 - mid.astype(F32)).astype(BF16)
        cum = _dot(tri, hi) + _dot(tri, mid) + _dot(tri, lo)
        cum_last = cum[L - 1:L, :] if d == 0 else cum[0:1, :]
        ec = jnp.exp(cum)
        enc = jnp.exp(-cum)
        p_last = jnp.exp(cum_last)
        a_t = -kk_ref[...].astype(F32) * jnp.exp(cum - lw)
        r_t = r_ref[...].astype(F32) * ec
        b_t = ka_ref[...].astype(F32) * enc
        k_t = kd_ref[...].astype(F32) * enc
        b_h = b_t * p_last
        k_h = k_t * p_last
        vv = v_ref[...].astype(F32)

        for p in range(lw.shape[1] // LANES):
            sl = slice(p * LANES, (p + 1) * LANES)
            ea = expand(a_t[:, sl])
            er = expand(r_t[:, sl])
            bt2 = jnp.concatenate([b_t[:, sl]] * 2, axis=0)
            kt2 = jnp.concatenate([k_t[:, sl]] * 2, axis=0)
            g = _dot_nt(jnp.concatenate([ea, er], axis=0).astype(BF16),
                        jnp.concatenate([bt2, kt2], axis=0).astype(BF16))
            xab = jnp.where(m_strict, g[0:P2, 0:P2], 0.0)
            xak = jnp.where(m_strict, g[0:P2, P2:2 * P2], 0.0)
            xrb = jnp.where(m_incl, g[P2:2 * P2, 0:P2], 0.0)
            xrk = jnp.where(m_incl, g[P2:2 * P2, P2:2 * P2], 0.0)
            tinv = jnp.where(eye, 1.0, xab)
            pw = xab
            for _ in range(L.bit_length() - 2):
                pwb = pw.astype(BF16)
                pw = _dot(pwb, pwb)
                tinv = tinv + _dot(tinv.astype(BF16), pw.astype(BF16))
            vexp = expand(vv[:, sl])
            xv = _dot(xak.astype(BF16), vexp.astype(BF16))
            wu0 = _dot(tinv.astype(BF16), jnp.concatenate([ea, xv], axis=1).astype(BF16))
            st = st_ref[d, p]
            ws = _dot(jnp.concatenate([wu0[:, 0:LANES], er], axis=0).astype(BF16), st.astype(BF16))
            u = ws[0:P2] + wu0[:, LANES:2 * LANES]
            uv = jnp.concatenate([u, vexp], axis=0).astype(BF16)
            yexp = ws[P2:2 * P2] + _dot(jnp.concatenate([xrb, xrk], axis=1).astype(BF16), uv)
            y_ref[:, sl] = (yexp[0:L] + yexp[L:P2]).astype(y_ref.dtype)
            bkt = jnp.concatenate([expand(b_h[:, sl]).T, expand(k_h[:, sl]).T], axis=1).astype(BF16)
            p_col = jnp.broadcast_to(p_last[:, sl], (P2, LANES)).T
            st_ref[d, p] = p_col * st + _dot(bkt, uv)


def _wkv_scan(lw0, lw1, r, v, kk, ka0, ka1, kd0, kd1):
    b, s, c = r.shape
    L = SCAN_CHUNK
    nc = s // L
    fwd = lambda bi, i: (bi, i, 0)
    bwd = lambda bi, i: (bi, nc - 1 - i, 0)
    fs = pl.BlockSpec((None, L, c), fwd)
    bs = pl.BlockSpec((None, L, c), bwd)
    return pl.pallas_call(
        _wkv_scan_kernel,
        grid=(b, nc),
        in_specs=[fs] * 6 + [bs] * 6,
        out_specs=[fs, bs],
        out_shape=[jax.ShapeDtypeStruct((b, s, c), F32)] * 2,
        scratch_shapes=[pltpu.VMEM((2, c // LANES, LANES, LANES), F32)],
        compiler_params=_params(("parallel", "arbitrary")),
        name="wkv_scan",
    )(lw0, r, v, kk, ka0, kd0, lw1, r, v, kk, ka1, kd1)


def _post_kernel(x_ref, oa_ref, yf_ref, yb_ref, g_ref, bv_ref, zg_ref, lnw_ref, lnb_ref,
                 woa_ref, wor_ref, wout_ref, n2w_ref, wq_ref, x1_ref, xn_ref, qp_ref):
    y = yf_ref[...] + yb_ref[...]
    c = y.shape[1]
    e, et = _seg_mats(c, HEAD_DIM)
    mu = _seg_sum(y, e, et) * (1.0 / HEAD_DIM)
    dlt = y - mu
    var = _seg_sum(dlt * dlt, e, et) * (1.0 / HEAD_DIM)
    yn = dlt * lax.rsqrt(var + RWKV_LN_EPS) * lnw_ref[...] + lnb_ref[...]
    out = ((yn + bv_ref[...].astype(F32)) * g_ref[...].astype(F32)).astype(BF16)
    h_rwkv = _dot(out, wor_ref[...])
    h_attn = _dot(oa_ref[...], woa_ref[...])
    zg = zg_ref[...].astype(F32)
    merged = _sigmoid(zg[:, 0:c]) * h_attn + _sigmoid(zg[:, c:2 * c]) * h_rwkv
    x1 = x_ref[...] + _dot(merged.astype(BF16), wout_ref[...])
    x1_ref[...] = x1
    ms = jnp.mean(x1 * x1, axis=-1, keepdims=True)
    xn = (x1 * lax.rsqrt(ms + NORM_EPS) * n2w_ref[...]).astype(BF16)
    xn_ref[...] = xn
    qp = _dot(xn, wq_ref[...]).astype(BF16)
    for j in range(qp_ref.shape[0]):
        qp_ref[j] = qp[:, j * LANES:(j + 1) * LANES]


def _post(x, oa, yf, yb, g, bv, zg, lnw, lnb, woa, wor, wout, n2w, wq, tm):
    t, d = x.shape
    nq = wq.shape[1] // LANES
    tok = lambda i: (i, 0)
    const = lambda i: (0, 0)
    tspec = lambda w: pl.BlockSpec((tm, w), tok)
    cspec = lambda a: pl.BlockSpec(a.shape, const)
    return pl.pallas_call(
        _post_kernel,
        grid=(t // tm,),
        in_specs=[tspec(d), tspec(d), tspec(d), tspec(d), tspec(d), tspec(d), tspec(2 * d),
                  cspec(lnw), cspec(lnb), cspec(woa), cspec(wor), cspec(wout), cspec(n2w), cspec(wq)],
        out_specs=[tspec(d), tspec(d), pl.BlockSpec((nq, tm, LANES), lambda i: (0, i, 0))],
        out_shape=[jax.ShapeDtypeStruct((t, d), F32), jax.ShapeDtypeStruct((t, d), BF16),
                   jax.ShapeDtypeStruct((nq, t, LANES), BF16)],
        compiler_params=_params(("parallel",)),
        name="post_merge",
    )(x, oa, yf, yb, g, bv, zg, lnw, lnb, woa, wor, wout, n2w, wq)


def _top_values(cur, k):
    rows = lax.broadcasted_iota(jnp.int32, cur.shape, 0)
    vals = []
    for _ in range(k):
        m = jnp.max(cur, axis=0, keepdims=True)
        vals.append(m)
        first = jnp.min(jnp.where(cur == m, rows, cur.shape[0]), axis=0, keepdims=True)
        cur = jnp.where(rows == first, -jnp.inf, cur)
    return vals


def _peer_kernel(xn_ref, qp_ref, sk_ref, u_ref, vt_ref, x1_ref, o_ref,
                 s0_ref, e0_ref, s1_ref, e1_ref, thr_ref, g_ref, acc_ref):
    jb = pl.program_id(1)
    tt = xn_ref.shape[0]
    K = PEER_TOPK
    rows_per_blk = g_ref.shape[0] // PEER_NKEYS

    @pl.when(jb == 0)
    def _():
        acc_ref[...] = jnp.zeros(acc_ref.shape, F32)

        def head_body(h, carry):
            s0 = _dot_nt(sk_ref[2 * h], qp_ref[2 * h])
            s1 = _dot_nt(sk_ref[2 * h + 1], qp_ref[2 * h + 1])
            a = _top_values(s0, K)
            b = _top_values(s1, K)
            bmat = jnp.concatenate(b, axis=0)
            cand = jnp.concatenate([a[i] + bmat for i in range(K)], axis=0)
            best = _top_values(cand, K)
            top = best[0]
            z = jnp.zeros_like(top)
            for val in best:
                z = z + jnp.exp(val - top)
            s0_ref[h] = s0
            e0_ref[h] = jnp.exp(s0 - a[0]) / z
            s1_ref[h] = s1
            e1_ref[h] = jnp.exp(s1 - b[0])
            thr_ref[h] = jnp.broadcast_to(best[K - 1], (8, tt))
            return carry

        lax.fori_loop(0, PEER_HEADS, head_body, 0)

    def row_body(il, carry):
        i = jb * rows_per_blk + il
        acc = jnp.zeros((PEER_NKEYS, tt), F32)
        for h in range(PEER_HEADS):
            s0row = s0_ref[h, pl.ds(i, 1), :]
            e0row = e0_ref[h, pl.ds(i, 1), :]
            sel = (s0row + s1_ref[h]) >= thr_ref[h, 0:1, :]
            acc = acc + jnp.where(sel, e0row * e1_ref[h], 0.0)
        g_ref[pl.ds(pl.multiple_of(il * PEER_NKEYS, PEER_NKEYS), PEER_NKEYS), :] = acc
        return carry

    lax.fori_loop(0, rows_per_blk, row_body, 0)

    hpre = _dot_nt(u_ref[...], xn_ref[...])
    gelu = 0.5 * hpre * (1.0 + lax.erf(hpre * (2.0 ** -0.5)))
    hg = (gelu * g_ref[...]).astype(BF16)
    acc_ref[...] += _dot(vt_ref[...], hg)

    @pl.when(jb == pl.num_programs(1) - 1)
    def _():
        o_ref[...] = x1_ref[...] + acc_ref[...].T


def _peer(xn, qp, sk, u, vt, x1, tt, eb):
    t, d = xn.shape
    n_exp = u.shape[0]
    nq = qp.shape[0]
    return pl.pallas_call(
        _peer_kernel,
        grid=(t // tt, n_exp // eb),
        in_specs=[pl.BlockSpec((tt, d), lambda i, j: (i, 0)),
                  pl.BlockSpec((nq, tt, LANES), lambda i, j: (0, i, 0)),
                  pl.BlockSpec(sk.shape, lambda i, j: (0, 0, 0)),
                  pl.BlockSpec((eb, d), lambda i, j: (j, 0)),
                  pl.BlockSpec((d, eb), lambda i, j: (0, j)),
                  pl.BlockSpec((tt, d), lambda i, j: (i, 0))],
        out_specs=pl.BlockSpec((tt, d), lambda i, j: (i, 0)),
        out_shape=jax.ShapeDtypeStruct((t, d), F32),
        scratch_shapes=[pltpu.VMEM((PEER_HEADS, PEER_NKEYS, tt), F32),
                        pltpu.VMEM((PEER_HEADS, PEER_NKEYS, tt), F32),
                        pltpu.VMEM((PEER_HEADS, PEER_NKEYS, tt), F32),
                        pltpu.VMEM((PEER_HEADS, PEER_NKEYS, tt), F32),
                        pltpu.VMEM((PEER_HEADS, 8, tt), F32),
                        pltpu.VMEM((eb, tt), F32),
                        pltpu.VMEM((d, tt), F32)],
        compiler_params=_params(("parallel", "arbitrary")),
        name="peer",
    )(xn, qp, sk, u, vt, x1)


def _rope_tables(s):
    inv = 1.0 / (ROPE_THETA ** (jnp.arange(0, HEAD_DIM, 2, dtype=F32) / HEAD_DIM))
    ang = jnp.arange(s, dtype=F32)[:, None] * inv[None, :]
    cos, sin = jnp.cos(ang), jnp.sin(ang)
    reps = LANES // HEAD_DIM
    cos_t = jnp.tile(jnp.concatenate([cos, cos], axis=1), (1, reps))
    sin_t = jnp.tile(jnp.concatenate([-sin, sin], axis=1), (1, reps))
    return cos_t, sin_t


def _tile(n, pref):
    return pref if n % pref == 0 else n


def _trunk(x, P):
    b, s, d = x.shape
    t = b * s
    c = d
    xf = x.reshape(t, d)
    tm = _tile(t, 512)
    za = _norm_matmul(xf, P["norm1_w"], P["w_a"], tm).reshape(b, s, -1)
    zr = _norm_matmul(xf, P["norm1_w"], P["w_r"], tm).reshape(b, s, -1)
    zg = _norm_matmul(xf, P["norm1_w"], P["w_g"], tm)

    cos_t, sin_t = _rope_tables(s)
    ts = _tile(s, 512)
    q = _attn_prep(za, cos_t, sin_t, P["q_norm_w"], 0, HEAD_DIM ** -0.5, ts)
    k = _attn_prep(za, cos_t, sin_t, P["k_norm_w"], 1, 1.0, ts)
    oa = _attention(P["lam_rows"], q, k, za, P["subln_w"], ts, ts)

    tp = _tile(s, 256)
    (r, v, kk, kd0, kd1, ka0, ka1, lw0, lw1, g, bv) = _rwkv_prep(
        zr, P["shift_mu"], P["decay_w0"], P["decay_w2"], P["iclr_a0"], P["iclr_a2"], P["gate_g2"],
        P["k_k"], P["k_a"], P["r_k"], c, tp)
    yf, yb = _wkv_scan(lw0, lw1, r, v, kk, ka0, ka1, kd0, kd1)

    flat = lambda a: a.reshape(t, a.shape[-1])
    x1, xn, qp = _post(xf, flat(oa), flat(yf), flat(yb), flat(g), flat(bv), zg, P["lnx_w"], P["lnx_b"],
                       P["w_o_attn"], P["w_o_rwkv"], P["w_out"], P["norm2_w"], P["peer_wq"], _tile(t, 256))
    y = _peer(xn, qp, P["peer_subkeys"], P["peer_u"], P["peer_vt"], x1, _tile(t, 512), 1024)
    return y.reshape(b, s, d)


def kernel(x_prompt, x_sample, norm1_w, w_in, q_norm_w, k_norm_w, lambda_q1, lambda_k1, lambda_q2, lambda_k2, subln_w, w_o_attn, shift_mu, decay_w0, decay_w2, iclr_a0, iclr_a2, gate_g2, k_k, k_a, r_k, lnx_w, lnx_b, w_o_rwkv, w_out, norm2_w, peer_wq, peer_subkeys, peer_u, peer_v):
    d = x_prompt.shape[-1]
    c = d
    att_cols = 3 * d
    rwkv_cols = shift_mu.shape[-1]
    l = 0
    w = w_in[l].astype(BF16)
    nsub = d // HEAD_DIM
    P = {
        "norm1_w": norm1_w[l][None, :],
        "w_a": w[:, :att_cols],
        "w_r": w[:, att_cols:att_cols + rwkv_cols],
        "w_g": w[:, att_cols + rwkv_cols:],
        "q_norm_w": jnp.tile(q_norm_w[l], nsub)[None, :],
        "k_norm_w": jnp.tile(k_norm_w[l], nsub)[None, :],
        "lam_rows": jnp.stack([lambda_q1[l], lambda_k1[l], lambda_q2[l], lambda_k2[l]]),
        "subln_w": subln_w[l][None, :],
        "w_o_attn": w_o_attn[l].astype(BF16),
        "shift_mu": shift_mu[l][None, :],
        "decay_w0": decay_w0[l],
        "decay_w2": decay_w2[l].reshape(-1, c).astype(BF16),
        "iclr_a0": iclr_a0[l],
        "iclr_a2": iclr_a2[l].reshape(-1, c).astype(BF16),
        "gate_g2": gate_g2[l].astype(BF16),
        "k_k": k_k[l][None, :],
        "k_a": k_a[l][None, :],
        "r_k": r_k[l].reshape(1, c),
        "lnx_w": lnx_w[l][None, :],
        "lnx_b": lnx_b[l][None, :],
        "w_o_rwkv": w_o_rwkv[l].astype(BF16),
        "w_out": w_out[l].astype(BF16),
        "norm2_w": norm2_w[l][None, :],
        "peer_wq": peer_wq[l].astype(BF16),
        "peer_subkeys": peer_subkeys[l].reshape(-1, PEER_NKEYS, peer_subkeys.shape[-1]).astype(BF16),
        "peer_u": peer_u[l].astype(BF16),
        "peer_vt": peer_v[l].astype(BF16).T,
    }
    return (_trunk(x_prompt, P), _trunk(x_sample, P))
```

```python
import functools
import math

import jax
import jax.numpy as jnp
from jax import lax
from jax.experimental import pallas as pl
from jax.experimental.pallas import tpu as pltpu

F32 = jnp.float32
BF16 = jnp.bfloat16

LANES = 128
HEAD_DIM = 64
ATT_HEADS = 8
NORM_EPS = 1e-6
RWKV_LN_EPS = 64e-5
ROPE_THETA = 10000.0
LAMBDA_INIT = 0.8 - 0.6 * math.exp(-0.3 * 0)
SCAN_CHUNK = 64
PEER_NKEYS = 128
PEER_TOPK = 16
PEER_HEADS = 8
VMEM_LIMIT = 56 * 1024 * 1024


def _params(sem):
    return pltpu.CompilerParams(dimension_semantics=sem, vmem_limit_bytes=VMEM_LIMIT)


def _dot(a, b):
    return jnp.dot(a, b, preferred_element_type=F32)


def _dot_nt(a, b):
    return lax.dot_general(a, b, (((1,), (1,)), ((), ())), preferred_element_type=F32)


def _split_dot(x, m):
    hi = x.astype(BF16)
    lo = (x - hi.astype(F32)).astype(BF16)
    return _dot(hi, m) + _dot(lo, m)


def _seg_mats(width, seg):
    lane = lax.broadcasted_iota(jnp.int32, (width, LANES), 0)
    grp = lax.broadcasted_iota(jnp.int32, (width, LANES), 1)
    e = jnp.where(lane // seg == grp, 1.0, 0.0).astype(BF16)
    grp_t = lax.broadcasted_iota(jnp.int32, (LANES, width), 0)
    lane_t = lax.broadcasted_iota(jnp.int32, (LANES, width), 1)
    et = jnp.where(lane_t // seg == grp_t, 1.0, 0.0).astype(BF16)
    return e, et


def _seg_sum(x, e, et):
    return _split_dot(_split_dot(x, e), et)


def _sigmoid(x):
    return 1.0 / (1.0 + jnp.exp(-x))


def _norm_matmul_kernel(x_ref, nw_ref, w_ref, o_ref):
    x = x_ref[...]
    ms = jnp.mean(x * x, axis=-1, keepdims=True)
    h = (x * lax.rsqrt(ms + NORM_EPS) * nw_ref[...]).astype(BF16)
    o_ref[...] = _dot(h, w_ref[...]).astype(o_ref.dtype)


def _norm_matmul(x, nw, w, tm):
    t, d = x.shape
    n = w.shape[1]
    return pl.pallas_call(
        _norm_matmul_kernel,
        grid=(t // tm,),
        in_specs=[pl.BlockSpec((tm, d), lambda i: (i, 0)),
                  pl.BlockSpec((1, d), lambda i: (0, 0)),
                  pl.BlockSpec((d, n), lambda i: (0, 0))],
        out_specs=pl.BlockSpec((tm, n), lambda i: (i, 0)),
        out_shape=jax.ShapeDtypeStruct((t, n), BF16),
        compiler_params=_params(("parallel",)),
        name="norm_matmul",
    )(x, nw, w)


def _attn_prep_kernel(z_ref, cos_ref, sin_ref, w_ref, o_ref, *, scale, transpose_out):
    x = z_ref[...].astype(F32)
    width = x.shape[1]
    e, et = _seg_mats(width, HEAD_DIM)
    ms = _seg_sum(x * x, e, et) * (1.0 / HEAD_DIM)
    y = x * lax.rsqrt(ms + NORM_EPS) * w_ref[...]
    reps = width // LANES
    c = jnp.concatenate([cos_ref[...]] * reps, axis=1)
    s = jnp.concatenate([sin_ref[...]] * reps, axis=1)
    lane = lax.broadcasted_iota(jnp.int32, y.shape, 1)
    half = HEAD_DIM // 2
    partner = jnp.where(lane % HEAD_DIM < half, pltpu.roll(y, width - half, 1), pltpu.roll(y, half, 1))
    out = (y * c + partner * s) * scale
    if transpose_out:
        for j in range(reps):
            o_ref[j * LANES:(j + 1) * LANES, :] = out[:, j * LANES:(j + 1) * LANES].T.astype(o_ref.dtype)
    else:
        o_ref[...] = out.astype(o_ref.dtype)


def _attn_prep(za, cos_t, sin_t, w_tiled, col_block, scale, tm, transpose_out):
    b, s, _ = za.shape
    width = w_tiled.shape[1]
    if transpose_out:
        out_spec = pl.BlockSpec((None, width, tm), lambda bi, i: (bi, 0, i))
        out_shape = jax.ShapeDtypeStruct((b, width, s), BF16)
    else:
        out_spec = pl.BlockSpec((None, tm, width), lambda bi, i: (bi, i, 0))
        out_shape = jax.ShapeDtypeStruct((b, s, width), BF16)
    return pl.pallas_call(
        functools.partial(_attn_prep_kernel, scale=scale, transpose_out=transpose_out),
        grid=(b, s // tm),
        in_specs=[pl.BlockSpec((None, tm, width), lambda bi, i: (bi, i, col_block)),
                  pl.BlockSpec((tm, LANES), lambda bi, i: (i, 0)),
                  pl.BlockSpec((tm, LANES), lambda bi, i: (i, 0)),
                  pl.BlockSpec((1, width), lambda bi, i: (0, 0))],
        out_specs=out_spec,
        out_shape=out_shape,
        compiler_params=_params(("parallel", "parallel")),
        name="attn_prep",
    )(za, cos_t, sin_t, w_tiled)


ATTN_ROW_CHUNK = 256


def _attn_kernel(lam_ref, q_ref, kt_ref, v_ref, sw_ref, o_ref, qs_ref, m_ref, acc_ref, *, tq):
    ki = pl.program_id(3)
    tk = kt_ref.shape[1]

    @pl.when(ki == 0)
    def _():
        q = q_ref[...]
        lane = lax.broadcasted_iota(jnp.int32, q.shape, 1)
        zero = jnp.zeros_like(q)
        qs_ref[0:tq, :] = jnp.where(lane < HEAD_DIM, q, zero)
        qs_ref[tq:2 * tq, :] = jnp.where(lane < HEAD_DIM, zero, q)
        m_ref[...] = jnp.full(m_ref.shape, -jnp.inf, F32)
        acc_ref[...] = jnp.zeros(acc_ref.shape, F32)

    v = v_ref[...]
    v_ones = jnp.concatenate([v, jnp.ones_like(v)], axis=1)
    kt = kt_ref[...]
    rc = min(ATTN_ROW_CHUNK, 2 * tq)
    for c in range(2 * tq // rc):
        rows = slice(c * rc, (c + 1) * rc)
        s = _dot(qs_ref[rows, :], kt)
        m_prev = m_ref[rows, :]
        m_new = jnp.maximum(m_prev, jnp.max(s, axis=1, keepdims=True))
        alpha = jnp.exp2(m_prev - m_new)
        p = jnp.exp2(s - jnp.concatenate([m_new] * (tk // LANES), axis=1))
        acc_ref[rows, :] = (jnp.concatenate([alpha, alpha], axis=1) * acc_ref[rows, :]
                            + _dot(p.astype(BF16), v_ones))
        m_ref[rows, :] = m_new

    @pl.when(ki == pl.num_programs(3) - 1)
    def _():
        lv = lam_ref[...]
        lam = (jnp.exp(jnp.sum(lv[0:1] * lv[1:2], axis=1, keepdims=True))
               - jnp.exp(jnp.sum(lv[2:3] * lv[3:4], axis=1, keepdims=True)) + LAMBDA_INIT)
        acc = acc_ref[...]
        o = (acc[0:tq, 0:LANES] / acc[0:tq, LANES:2 * LANES]
             - lam * (acc[tq:2 * tq, 0:LANES] / acc[tq:2 * tq, LANES:2 * LANES]))
        ms = jnp.mean(o * o, axis=-1, keepdims=True)
        o = o * lax.rsqrt(ms + NORM_EPS) * sw_ref[...] * (1.0 - LAMBDA_INIT)
        o_ref[...] = o.astype(o_ref.dtype)


def _attention(lam_rows, q, kt, za, subln_w, tq, tk):
    b, s, width = q.shape
    heads = width // LANES
    v_col0 = 2 * heads
    return pl.pallas_call(
        functools.partial(_attn_kernel, tq=tq),
        grid=(b, heads, s // tq, s // tk),
        in_specs=[pl.BlockSpec((4, HEAD_DIM), lambda bi, h, qi, ki: (0, 0)),
                  pl.BlockSpec((None, tq, LANES), lambda bi, h, qi, ki: (bi, qi, h)),
                  pl.BlockSpec((None, LANES, tk), lambda bi, h, qi, ki: (bi, h, ki)),
                  pl.BlockSpec((None, tk, LANES), lambda bi, h, qi, ki: (bi, ki, v_col0 + h)),
                  pl.BlockSpec((1, LANES), lambda bi, h, qi, ki: (0, 0))],
        out_specs=pl.BlockSpec((None, tq, LANES), lambda bi, h, qi, ki: (bi, qi, h)),
        out_shape=jax.ShapeDtypeStruct((b, s, width), BF16),
        scratch_shapes=[pltpu.VMEM((2 * tq, LANES), BF16),
                        pltpu.VMEM((2 * tq, LANES), F32),
                        pltpu.VMEM((2 * tq, 2 * LANES), F32)],
        compiler_params=_params(("parallel", "parallel", "parallel", "arbitrary")),
        name="diff_attention",
    )(lam_rows, q, kt, za, subln_w)


HALO = 16


def _rwkv_prep_kernel(z_ref, zp_ref, zn_ref, mu_ref, w0_ref, w2_ref, a0_ref, a2_ref, g2_ref,
                      kk_w_ref, ka_w_ref, rk_ref,
                      r_ref, v_ref, kk_ref, kd0_ref, kd1_ref, ka0_ref, ka1_ref, lw0_ref, lw1_ref,
                      g_ref, bv_ref, *, c):
    i = pl.program_id(1)
    z = z_ref[...].astype(F32)
    tm = z.shape[0]
    row = lax.broadcasted_iota(jnp.int32, z.shape, 0)
    prev_row = jnp.where(i == 0, 0.0, zp_ref[HALO - 1:HALO, :].astype(F32))
    next_row = jnp.where(i == pl.num_programs(1) - 1, 0.0, zn_ref[0:1, :].astype(F32))
    z_up = jnp.where(row == 0, prev_row, pltpu.roll(z, 1, 0))
    z_dn = jnp.where(row == tm - 1, next_row, pltpu.roll(z, tm - 1, 0))
    zs = z + mu_ref[...] * (0.5 * (z_up + z_dn) - z)

    r = zs[:, 0:c]
    k = zs[:, c:2 * c]
    v = zs[:, 2 * c:3 * c]
    wl = zs[:, 3 * c:3 * c + LANES]
    al = zs[:, 3 * c + LANES:3 * c + 2 * LANES]
    gl = zs[:, 3 * c + 2 * LANES:3 * c + 3 * LANES]

    lane = lax.broadcasted_iota(jnp.int32, wl.shape, 1)
    twl = jnp.tanh(wl)
    e, et = _seg_mats(c, HEAD_DIM)

    kk = k * kk_w_ref[...]
    kk = kk * lax.rsqrt(_seg_sum(kk * kk, e, et) + 1e-12)
    kk_ref[...] = kk.astype(kk_ref.dtype)
    r_ref[...] = r.astype(r_ref.dtype)
    v_ref[...] = v.astype(v_ref.dtype)
    g_ref[...] = _dot(_sigmoid(gl).astype(BF16), g2_ref[...]).astype(g_ref.dtype)

    ksum = jnp.zeros_like(k)
    for d, (kd_ref, ka_ref, lw_ref) in enumerate(((kd0_ref, ka0_ref, lw0_ref), (kd1_ref, ka1_ref, lw1_ref))):
        sel = (lane < HEAD_DIM) if d == 0 else (lane >= HEAD_DIM)
        wpre = _dot(jnp.where(sel, twl, 0.0).astype(BF16), w2_ref[...]) + w0_ref[d:d + 1, :]
        lw_ref[...] = (-math.exp(-0.5)) * _sigmoid(wpre)
        a = _sigmoid(_dot(jnp.where(sel, al, 0.0).astype(BF16), a2_ref[...]) + a0_ref[d:d + 1, :])
        kd = k * (1.0 + (a - 1.0) * ka_w_ref[...])
        kd_ref[...] = kd.astype(kd_ref.dtype)
        ka_ref[...] = (kk * a).astype(ka_ref.dtype)
        ksum = ksum + kd
    bonus = _seg_sum(r * ksum * rk_ref[...], e, et)
    bv_ref[...] = (bonus * v).astype(bv_ref.dtype)


def _rwkv_prep(zr, mu, w0, w2, a0, a2, g2, kk_w, ka_w, rk, c, tm):
    b, s, cols = zr.shape
    nh = tm // HALO
    last = s // HALO - 1
    tok = lambda bi, i: (bi, i, 0)
    const = lambda bi, i: (0, 0)
    out = lambda dt: jax.ShapeDtypeStruct((b, s, c), dt)
    ospec = pl.BlockSpec((None, tm, c), tok)
    return pl.pallas_call(
        functools.partial(_rwkv_prep_kernel, c=c),
        grid=(b, s // tm),
        in_specs=[pl.BlockSpec((None, tm, cols), tok),
                  pl.BlockSpec((None, HALO, cols), lambda bi, i: (bi, jnp.maximum(i * nh - 1, 0), 0)),
                  pl.BlockSpec((None, HALO, cols), lambda bi, i: (bi, jnp.minimum((i + 1) * nh, last), 0)),
                  pl.BlockSpec((1, cols), const),
                  pl.BlockSpec((2, c), const),
                  pl.BlockSpec((LANES, c), const),
                  pl.BlockSpec((2, c), const),
                  pl.BlockSpec((LANES, c), const),
                  pl.BlockSpec((LANES, c), const),
                  pl.BlockSpec((1, c), const),
                  pl.BlockSpec((1, c), const),
                  pl.BlockSpec((1, c), const)],
        out_specs=[ospec] * 11,
        out_shape=[out(BF16)] * 7 + [out(F32)] * 2 + [out(BF16)] * 2,
        compiler_params=_params(("parallel", "parallel")),
        name="rwkv_prep",
    )(zr, zr, zr, mu, w0, w2, a0, a2, g2, kk_w, ka_w, rk)


def _wkv_scan_kernel(lwf_ref, rf_ref, vf_ref, kkf_ref, kaf_ref, kdf_ref,
                     lwb_ref, rb_ref, vb_ref, kkb_ref, kab_ref, kdb_ref,
                     yf_ref, yb_ref, st_ref):
    L = SCAN_CHUNK
    P2 = 2 * L
    ci = pl.program_id(1)

    @pl.when(ci == 0)
    def _():
        st_ref[...] = jnp.zeros(st_ref.shape, F32)

    row = lax.broadcasted_iota(jnp.int32, (P2, P2), 0)
    col = lax.broadcasted_iota(jnp.int32, (P2, P2), 1)
    same_head = (row // L) == (col // L)
    rt = row % L
    ct = col % L
    eye = row == col
    ti = lax.broadcasted_iota(jnp.int32, (L, L), 0)
    tj = lax.broadcasted_iota(jnp.int32, (L, L), 1)
    lane_a = lax.broadcasted_iota(jnp.int32, (L, LANES), 1) < HEAD_DIM

    def expand(x):
        return jnp.concatenate([jnp.where(lane_a, x, 0.0), jnp.where(lane_a, 0.0, x)], axis=0)

    units = []
    dirs = ((lwf_ref, rf_ref, vf_ref, kkf_ref, kaf_ref, kdf_ref, yf_ref),
            (lwb_ref, rb_ref, vb_ref, kkb_ref, kab_ref, kdb_ref, yb_ref))
    for d, (lw_ref, r_ref, v_ref, kk_ref, ka_ref, kd_ref, y_ref) in enumerate(dirs):
        if d == 0:
            m_strict = same_head & (rt > ct)
            m_incl = same_head & (rt >= ct)
            tri = jnp.where(tj <= ti, 1.0, 0.0).astype(BF16)
        else:
            m_strict = same_head & (rt < ct)
            m_incl = same_head & (rt <= ct)
            tri = jnp.where(tj >= ti, 1.0, 0.0).astype(BF16)
        lw = lw_ref[...]
        hi = lw.astype(BF16)
        r1 = lw - hi.astype(F32)
        mid = r1.astype(BF16)
        lo = (r1 - mid.astype(F32)).astype(BF16)
        cum = _dot(tri, hi) + _dot(tri, mid) + _dot(tri, lo)
        cum_last = cum[L - 1:L, :] if d == 0 else cum[0:1, :]
        ec = jnp.exp(cum)
        enc = jnp.exp(-cum)
        p_last = jnp.exp(cum_last)
        a_t = -kk_ref[...].astype(F32) * jnp.exp(cum - lw)
        r_t = r_ref[...].astype(F32) * ec
        b_t = ka_ref[...].astype(F32) * enc
        k_t = kd_ref[...].astype(F32) * enc
        b_h = b_t * p_last
        k_h = k_t * p_last
        vv = v_ref[...].astype(F32)

        for p in range(lw.shape[1] // LANES):
            sl = slice(p * LANES, (p + 1) * LANES)
            units.append(dict(
                d=d, p=p, sl=sl, y_ref=y_ref, m_strict=m_strict, m_incl=m_incl,
                ea=expand(a_t[:, sl]), er=expand(r_t[:, sl]), vexp=expand(vv[:, sl]),
                bk2=jnp.concatenate([b_t[:, sl]] * 2 + [k_t[:, sl]] * 2, axis=0).astype(BF16),
                bh=b_h[:, sl], kh=k_h[:, sl], p_last=p_last[:, sl]))

    for u in units:
        g = _dot_nt(jnp.concatenate([u["ea"], u["er"]], axis=0).astype(BF16), u["bk2"])
        u["xab"] = jnp.where(u["m_strict"], g[0:P2, 0:P2], 0.0)
        u["xak"] = jnp.where(u["m_strict"], g[0:P2, P2:2 * P2], 0.0).astype(BF16)
        u["xr"] = jnp.concatenate([jnp.where(u["m_incl"], g[P2:2 * P2, 0:P2], 0.0),
                                   jnp.where(u["m_incl"], g[P2:2 * P2, P2:2 * P2], 0.0)], axis=1).astype(BF16)
        u["tinv"] = jnp.where(eye, 1.0, u["xab"])
        u["pw"] = u["xab"].astype(BF16)
    for _ in range(L.bit_length() - 2):
        for u in units:
            u["pw"] = _dot(u["pw"], u["pw"]).astype(BF16)
        for u in units:
            u["tinv"] = u["tinv"] + _dot(u["tinv"].astype(BF16), u["pw"])
    for u in units:
        u["xv"] = _dot(u["xak"], u["vexp"].astype(BF16))
    for u in units:
        u["wu0"] = _dot(u["tinv"].astype(BF16), jnp.concatenate([u["ea"], u["xv"]], axis=1).astype(BF16))
        u["bkt"] = jnp.concatenate([expand(u["bh"]).T, expand(u["kh"]).T], axis=1).astype(BF16)
        u["p_col"] = jnp.broadcast_to(u["p_last"], (P2, LANES)).T
    for u in units:
        u["st"] = st_ref[u["d"], u["p"]]
        u["ws"] = _dot(jnp.concatenate([u["wu0"][:, 0:LANES], u["er"]], axis=0).astype(BF16), u["st"].astype(BF16))
    for u in units:
        uu = u["ws"][0:P2] + u["wu0"][:, LANES:2 * LANES]
        u["uv"] = jnp.concatenate([uu, u["vexp"]], axis=0).astype(BF16)
    for u in units:
        yexp = u["ws"][P2:2 * P2] + _dot(u["xr"], u["uv"])
        u["y_ref"][:, u["sl"]] = (yexp[0:L] + yexp[L:P2]).astype(u["y_ref"].dtype)
    for u in units:
        st_ref[u["d"], u["p"]] = u["p_col"] * u["st"] + _dot(u["bkt"], u["uv"])


def _wkv_scan(lw0, lw1, r, v, kk, ka0, ka1, kd0, kd1):
    b, s, c = r.shape
    L = SCAN_CHUNK
    nc = s // L
    fwd = lambda bi, i: (bi, i, 0)
    bwd = lambda bi, i: (bi, nc - 1 - i, 0)
    fs = pl.BlockSpec((None, L, c), fwd)
    bs = pl.BlockSpec((None, L, c), bwd)
    return pl.pallas_call(
        _wkv_scan_kernel,
        grid=(b, nc),
        in_specs=[fs] * 6 + [bs] * 6,
        out_specs=[fs, bs],
        out_shape=[jax.ShapeDtypeStruct((b, s, c), F32)] * 2,
        scratch_shapes=[pltpu.VMEM((2, c // LANES, LANES, LANES), F32)],
        compiler_params=_params(("parallel", "arbitrary")),
        name="wkv_scan",
    )(lw0, r, v, kk, ka0, kd0, lw1, r, v, kk, ka1, kd1)


def _post_kernel(x_ref, oa_ref, yf_ref, yb_ref, g_ref, bv_ref, zg_ref, lnw_ref, lnb_ref,
                 woa_ref, wor_ref, wout_ref, n2w_ref, wq_ref, x1_ref, xn_ref, qp_ref):
    y = yf_ref[...] + yb_ref[...]
    c = y.shape[1]
    e, et = _seg_mats(c, HEAD_DIM)
    mu = _seg_sum(y, e, et) * (1.0 / HEAD_DIM)
    dlt = y - mu
    var = _seg_sum(dlt * dlt, e, et) * (1.0 / HEAD_DIM)
    yn = dlt * lax.rsqrt(var + RWKV_LN_EPS) * lnw_ref[...] + lnb_ref[...]
    out = ((yn + bv_ref[...].astype(F32)) * g_ref[...].astype(F32)).astype(BF16)
    h_rwkv = _dot(out, wor_ref[...])
    h_attn = _dot(oa_ref[...], woa_ref[...])
    zg = zg_ref[...].astype(F32)
    merged = _sigmoid(zg[:, 0:c]) * h_attn + _sigmoid(zg[:, c:2 * c]) * h_rwkv
    x1 = x_ref[...] + _dot(merged.astype(BF16), wout_ref[...])
    x1_ref[...] = x1
    ms = jnp.mean(x1 * x1, axis=-1, keepdims=True)
    xn = (x1 * lax.rsqrt(ms + NORM_EPS) * n2w_ref[...]).astype(BF16)
    xn_ref[...] = xn
    qp = _dot(xn, wq_ref[...]).astype(BF16)
    for j in range(qp_ref.shape[0]):
        qp_ref[j] = qp[:, j * LANES:(j + 1) * LANES]


def _post(x, oa, yf, yb, g, bv, zg, lnw, lnb, woa, wor, wout, n2w, wq, tm):
    t, d = x.shape
    nq = wq.shape[1] // LANES
    tok = lambda i: (i, 0)
    const = lambda i: (0, 0)
    tspec = lambda w: pl.BlockSpec((tm, w), tok)
    cspec = lambda a: pl.BlockSpec(a.shape, const)
    return pl.pallas_call(
        _post_kernel,
        grid=(t // tm,),
        in_specs=[tspec(d), tspec(d), tspec(d), tspec(d), tspec(d), tspec(d), tspec(2 * d),
                  cspec(lnw), cspec(lnb), cspec(woa), cspec(wor), cspec(wout), cspec(n2w), cspec(wq)],
        out_specs=[tspec(d), tspec(d), pl.BlockSpec((nq, tm, LANES), lambda i: (0, i, 0))],
        out_shape=[jax.ShapeDtypeStruct((t, d), F32), jax.ShapeDtypeStruct((t, d), BF16),
                   jax.ShapeDtypeStruct((nq, t, LANES), BF16)],
        compiler_params=_params(("parallel",)),
        name="post_merge",
    )(x, oa, yf, yb, g, bv, zg, lnw, lnb, woa, wor, wout, n2w, wq)


def _top_values(cur, k):
    rows = lax.broadcasted_iota(jnp.int32, cur.shape, 0)
    vals = []
    for _ in range(k):
        m = jnp.max(cur, axis=0, keepdims=True)
        vals.append(m)
        first = jnp.min(jnp.where(cur == m, rows, cur.shape[0]), axis=0, keepdims=True)
        cur = jnp.where(rows == first, -jnp.inf, cur)
    return vals


def _peer_kernel(xn_ref, qp_ref, sk_ref, u_ref, vt_ref, x1_ref, o_ref,
                 s0_ref, e0_ref, s1_ref, e1_ref, thr_ref, g_ref, acc_ref):
    jb = pl.program_id(1)
    tt = xn_ref.shape[0]
    K = PEER_TOPK
    rows_per_blk = g_ref.shape[0] // PEER_NKEYS

    @pl.when(jb == 0)
    def _():
        acc_ref[...] = jnp.zeros(acc_ref.shape, F32)

        def head_body(h, carry):
            s0 = _dot_nt(sk_ref[2 * h], qp_ref[2 * h])
            s1 = _dot_nt(sk_ref[2 * h + 1], qp_ref[2 * h + 1])
            a = _top_values(s0, K)
            b = _top_values(s1, K)
            bmat = jnp.concatenate(b, axis=0)
            cand = jnp.concatenate([a[i] + bmat for i in range(K)], axis=0)
            best = _top_values(cand, K)
            top = best[0]
            z = jnp.zeros_like(top)
            for val in best:
                z = z + jnp.exp(val - top)
            s0_ref[h] = s0
            e0_ref[h] = jnp.exp(s0 - a[0]) / z
            s1_ref[h] = s1
            e1_ref[h] = jnp.exp(s1 - b[0])
            thr_ref[h] = jnp.broadcast_to(best[K - 1], (8, tt))
            return carry

        lax.fori_loop(0, PEER_HEADS, head_body, 0)

    def row_body(il, carry):
        i = jb * rows_per_blk + il
        acc = jnp.zeros((PEER_NKEYS, tt), F32)
        for h in range(PEER_HEADS):
            s0row = s0_ref[h, pl.ds(i, 1), :]
            e0row = e0_ref[h, pl.ds(i, 1), :]
            sel = (s0row + s1_ref[h]) >= thr_ref[h, 0:1, :]
            acc = acc + jnp.where(sel, e0row * e1_ref[h], 0.0)
        g_ref[pl.ds(pl.multiple_of(il * PEER_NKEYS, PEER_NKEYS), PEER_NKEYS), :] = acc
        return carry

    lax.fori_loop(0, rows_per_blk, row_body, 0)

    hpre = _dot_nt(u_ref[...], xn_ref[...])
    gelu = 0.5 * hpre * (1.0 + lax.erf(hpre * (2.0 ** -0.5)))
    hg = (gelu * g_ref[...]).astype(BF16)
    acc_ref[...] += _dot(vt_ref[...], hg)

    @pl.when(jb == pl.num_programs(1) - 1)
    def _():
        o_ref[...] = x1_ref[...] + acc_ref[...].T


def _peer(xn, qp, sk, u, vt, x1, tt, eb):
    t, d = xn.shape
    n_exp = u.shape[0]
    nq = qp.shape[0]
    return pl.pallas_call(
        _peer_kernel,
        grid=(t // tt, n_exp // eb),
        in_specs=[pl.BlockSpec((tt, d), lambda i, j: (i, 0)),
                  pl.BlockSpec((nq, tt, LANES), lambda i, j: (0, i, 0)),
                  pl.BlockSpec(sk.shape, lambda i, j: (0, 0, 0)),
                  pl.BlockSpec((eb, d), lambda i, j: (j, 0)),
                  pl.BlockSpec((d, eb), lambda i, j: (0, j)),
                  pl.BlockSpec((tt, d), lambda i, j: (i, 0))],
        out_specs=pl.BlockSpec((tt, d), lambda i, j: (i, 0)),
        out_shape=jax.ShapeDtypeStruct((t, d), F32),
        scratch_shapes=[pltpu.VMEM((PEER_HEADS, PEER_NKEYS, tt), F32),
                        pltpu.VMEM((PEER_HEADS, PEER_NKEYS, tt), F32),
                        pltpu.VMEM((PEER_HEADS, PEER_NKEYS, tt), F32),
                        pltpu.VMEM((PEER_HEADS, PEER_NKEYS, tt), F32),
                        pltpu.VMEM((PEER_HEADS, 8, tt), F32),
                        pltpu.VMEM((eb, tt), F32),
                        pltpu.VMEM((d, tt), F32)],
        compiler_params=_params(("parallel", "arbitrary")),
        name="peer",
    )(xn, qp, sk, u, vt, x1)


def _rope_tables(s):
    inv = 1.0 / (ROPE_THETA ** (jnp.arange(0, HEAD_DIM, 2, dtype=F32) / HEAD_DIM))
    ang = jnp.arange(s, dtype=F32)[:, None] * inv[None, :]
    cos, sin = jnp.cos(ang), jnp.sin(ang)
    reps = LANES // HEAD_DIM
    cos_t = jnp.tile(jnp.concatenate([cos, cos], axis=1), (1, reps))
    sin_t = jnp.tile(jnp.concatenate([-sin, sin], axis=1), (1, reps))
    return cos_t, sin_t


def _tile(n, pref):
    return pref if n % pref == 0 else n


def _trunk(x, P):
    b, s, d = x.shape
    t = b * s
    c = d
    xf = x.reshape(t, d)
    tm = _tile(t, 512)
    za = _norm_matmul(xf, P["norm1_w"], P["w_a"], tm).reshape(b, s, -1)
    zr = _norm_matmul(xf, P["norm1_w"], P["w_r"], tm).reshape(b, s, -1)
    zg = _norm_matmul(xf, P["norm1_w"], P["w_g"], tm)

    cos_t, sin_t = _rope_tables(s)
    ts = _tile(s, 512)
    q = _attn_prep(za, cos_t, sin_t, P["q_norm_w"], 0, HEAD_DIM ** -0.5 * math.log2(math.e), ts, False)
    kt = _attn_prep(za, cos_t, sin_t, P["k_norm_w"], 1, 1.0, ts, True)
    oa = _attention(P["lam_rows"], q, kt, za, P["subln_w"], ts, _tile(s, 1024))

    tp = _tile(s, 256)
    (r, v, kk, kd0, kd1, ka0, ka1, lw0, lw1, g, bv) = _rwkv_prep(
        zr, P["shift_mu"], P["decay_w0"], P["decay_w2"], P["iclr_a0"], P["iclr_a2"], P["gate_g2"],
        P["k_k"], P["k_a"], P["r_k"], c, tp)
    yf, yb = _wkv_scan(lw0, lw1, r, v, kk, ka0, ka1, kd0, kd1)

    flat = lambda a: a.reshape(t, a.shape[-1])
    x1, xn, qp = _post(xf, flat(oa), flat(yf), flat(yb), flat(g), flat(bv), zg, P["lnx_w"], P["lnx_b"],
                       P["w_o_attn"], P["w_o_rwkv"], P["w_out"], P["norm2_w"], P["peer_wq"], _tile(t, 256))
    y = _peer(xn, qp, P["peer_subkeys"], P["peer_u"], P["peer_vt"], x1, _tile(t, 512), 1024)
    return y.reshape(b, s, d)


def kernel(x_prompt, x_sample, norm1_w, w_in, q_norm_w, k_norm_w, lambda_q1, lambda_k1, lambda_q2, lambda_k2, subln_w, w_o_attn, shift_mu, decay_w0, decay_w2, iclr_a0, iclr_a2, gate_g2, k_k, k_a, r_k, lnx_w, lnx_b, w_o_rwkv, w_out, norm2_w, peer_wq, peer_subkeys, peer_u, peer_v):
    d = x_prompt.shape[-1]
    c = d
    att_cols = 3 * d
    rwkv_cols = shift_mu.shape[-1]
    l = 0
    w = w_in[l].astype(BF16)
    nsub = d // HEAD_DIM
    P = {
        "norm1_w": norm1_w[l][None, :],
        "w_a": w[:, :att_cols],
        "w_r": w[:, att_cols:att_cols + rwkv_cols],
        "w_g": w[:, att_cols + rwkv_cols:],
        "q_norm_w": jnp.tile(q_norm_w[l], nsub)[None, :],
        "k_norm_w": jnp.tile(k_norm_w[l], nsub)[None, :],
        "lam_rows": jnp.stack([lambda_q1[l], lambda_k1[l], lambda_q2[l], lambda_k2[l]]),
        "subln_w": subln_w[l][None, :],
        "w_o_attn": w_o_attn[l].astype(BF16),
        "shift_mu": shift_mu[l][None, :],
        "decay_w0": decay_w0[l],
        "decay_w2": decay_w2[l].reshape(-1, c).astype(BF16),
        "iclr_a0": iclr_a0[l],
        "iclr_a2": iclr_a2[l].reshape(-1, c).astype(BF16),
        "gate_g2": gate_g2[l].astype(BF16),
        "k_k": k_k[l][None, :],
        "k_a": k_a[l][None, :],
        "r_k": r_k[l].reshape(1, c),
        "lnx_w": lnx_w[l][None, :],
        "lnx_b": lnx_b[l][None, :],
        "w_o_rwkv": w_o_rwkv[l].astype(BF16),
        "w_out": w_out[l].astype(BF16),
        "norm2_w": norm2_w[l][None, :],
        "peer_wq": peer_wq[l].astype(BF16),
        "peer_subkeys": peer_subkeys[l].reshape(-1, PEER_NKEYS, peer_subkeys.shape[-1]).astype(BF16),
        "peer_u": peer_u[l].astype(BF16),
        "peer_vt": peer_v[l].astype(BF16).T,
    }
    return (_trunk(x_prompt, P), _trunk(x_sample, P))
```

```python
import functools
import math

import jax
import jax.numpy as jnp
from jax import lax
from jax.experimental import pallas as pl
from jax.experimental.pallas import tpu as pltpu

F32 = jnp.float32
BF16 = jnp.bfloat16

LANES = 128
HEAD_DIM = 64
ATT_HEADS = 8
NORM_EPS = 1e-6
RWKV_LN_EPS = 64e-5
ROPE_THETA = 10000.0
LAMBDA_INIT = 0.8 - 0.6 * math.exp(-0.3 * 0)
SCAN_CHUNK = 64
PEER_NKEYS = 128
PEER_TOPK = 16
PEER_HEADS = 8
VMEM_LIMIT = 56 * 1024 * 1024


def _params(sem):
    return pltpu.CompilerParams(dimension_semantics=sem, vmem_limit_bytes=VMEM_LIMIT)


def _dot(a, b):
    return jnp.dot(a, b, preferred_element_type=F32)


def _dot_nt(a, b):
    return lax.dot_general(a, b, (((1,), (1,)), ((), ())), preferred_element_type=F32)


def _split_dot(x, m):
    hi = x.astype(BF16)
    lo = (x - hi.astype(F32)).astype(BF16)
    return _dot(hi, m) + _dot(lo, m)


def _seg_mats(width, seg):
    lane = lax.broadcasted_iota(jnp.int32, (width, LANES), 0)
    grp = lax.broadcasted_iota(jnp.int32, (width, LANES), 1)
    e = jnp.where(lane // seg == grp, 1.0, 0.0).astype(BF16)
    grp_t = lax.broadcasted_iota(jnp.int32, (LANES, width), 0)
    lane_t = lax.broadcasted_iota(jnp.int32, (LANES, width), 1)
    et = jnp.where(lane_t // seg == grp_t, 1.0, 0.0).astype(BF16)
    return e, et


def _seg_sum(x, e, et):
    return _split_dot(_split_dot(x, e), et)


def _sigmoid(x):
    return 1.0 / (1.0 + jnp.exp(-x))


def _norm_matmul_kernel(x_ref, nw_ref, w_ref, o_ref):
    x = x_ref[...]
    ms = jnp.mean(x * x, axis=-1, keepdims=True)
    h = (x * lax.rsqrt(ms + NORM_EPS) * nw_ref[...]).astype(BF16)
    o_ref[...] = _dot(h, w_ref[...]).astype(o_ref.dtype)


def _norm_matmul(x, nw, w, tm):
    t, d = x.shape
    n = w.shape[1]
    return pl.pallas_call(
        _norm_matmul_kernel,
        grid=(t // tm,),
        in_specs=[pl.BlockSpec((tm, d), lambda i: (i, 0)),
                  pl.BlockSpec((1, d), lambda i: (0, 0)),
                  pl.BlockSpec((d, n), lambda i: (0, 0))],
        out_specs=pl.BlockSpec((tm, n), lambda i: (i, 0)),
        out_shape=jax.ShapeDtypeStruct((t, n), BF16),
        compiler_params=_params(("parallel",)),
        name="norm_matmul",
    )(x, nw, w)


def _attn_prep_kernel(z_ref, cos_ref, sin_ref, w_ref, o_ref, *, scale, transpose_out):
    x = z_ref[...].astype(F32)
    width = x.shape[1]
    e, et = _seg_mats(width, HEAD_DIM)
    ms = _seg_sum(x * x, e, et) * (1.0 / HEAD_DIM)
    y = x * lax.rsqrt(ms + NORM_EPS) * w_ref[...]
    reps = width // LANES
    c = jnp.concatenate([cos_ref[...]] * reps, axis=1)
    s = jnp.concatenate([sin_ref[...]] * reps, axis=1)
    lane = lax.broadcasted_iota(jnp.int32, y.shape, 1)
    half = HEAD_DIM // 2
    partner = jnp.where(lane % HEAD_DIM < half, pltpu.roll(y, width - half, 1), pltpu.roll(y, half, 1))
    out = (y * c + partner * s) * scale
    if transpose_out:
        for j in range(reps):
            o_ref[j * LANES:(j + 1) * LANES, :] = out[:, j * LANES:(j + 1) * LANES].T.astype(o_ref.dtype)
    else:
        o_ref[...] = out.astype(o_ref.dtype)


def _attn_prep(za, cos_t, sin_t, w_tiled, col_block, scale, tm, transpose_out):
    b, s, _ = za.shape
    width = w_tiled.shape[1]
    if transpose_out:
        out_spec = pl.BlockSpec((None, width, tm), lambda bi, i: (bi, 0, i))
        out_shape = jax.ShapeDtypeStruct((b, width, s), BF16)
    else:
        out_spec = pl.BlockSpec((None, tm, width), lambda bi, i: (bi, i, 0))
        out_shape = jax.ShapeDtypeStruct((b, s, width), BF16)
    return pl.pallas_call(
        functools.partial(_attn_prep_kernel, scale=scale, transpose_out=transpose_out),
        grid=(b, s // tm),
        in_specs=[pl.BlockSpec((None, tm, width), lambda bi, i: (bi, i, col_block)),
                  pl.BlockSpec((tm, LANES), lambda bi, i: (i, 0)),
                  pl.BlockSpec((tm, LANES), lambda bi, i: (i, 0)),
                  pl.BlockSpec((1, width), lambda bi, i: (0, 0))],
        out_specs=out_spec,
        out_shape=out_shape,
        compiler_params=_params(("parallel", "parallel")),
        name="attn_prep",
    )(za, cos_t, sin_t, w_tiled)


ATTN_ROW_CHUNK = 256


def _attn_kernel(lam_ref, q_ref, kt_ref, v_ref, sw_ref, o_ref, qs_ref, m_ref, acc_ref, *, tq):
    ki = pl.program_id(3)
    tk = kt_ref.shape[1]

    @pl.when(ki == 0)
    def _():
        q = q_ref[...]
        lane = lax.broadcasted_iota(jnp.int32, q.shape, 1)
        zero = jnp.zeros_like(q)
        qs_ref[0:tq, :] = jnp.where(lane < HEAD_DIM, q, zero)
        qs_ref[tq:2 * tq, :] = jnp.where(lane < HEAD_DIM, zero, q)
        m_ref[...] = jnp.full(m_ref.shape, -jnp.inf, F32)
        acc_ref[...] = jnp.zeros(acc_ref.shape, F32)

    v = v_ref[...]
    v_ones = jnp.concatenate([v, jnp.ones_like(v)], axis=1)
    kt = kt_ref[...]
    rc = min(ATTN_ROW_CHUNK, 2 * tq)
    n_chunks = 2 * tq // rc
    scores = [None] * n_chunks
    scores[0] = _dot(qs_ref[0:rc, :], kt)
    for c in range(n_chunks):
        rows = slice(c * rc, (c + 1) * rc)
        if c + 1 < n_chunks:
            scores[c + 1] = _dot(qs_ref[(c + 1) * rc:(c + 2) * rc, :], kt)
        s = scores[c]
        m_prev = m_ref[rows, :]
        m_new = jnp.maximum(m_prev, jnp.max(s, axis=1, keepdims=True))
        alpha = jnp.exp2(m_prev - m_new)
        p = jnp.exp2((s - jnp.concatenate([m_new] * (tk // LANES), axis=1)).astype(BF16))
        acc_ref[rows, :] = (jnp.concatenate([alpha, alpha], axis=1) * acc_ref[rows, :]
                            + _dot(p, v_ones))
        m_ref[rows, :] = m_new

    @pl.when(ki == pl.num_programs(3) - 1)
    def _():
        lv = lam_ref[...]
        lam = (jnp.exp(jnp.sum(lv[0:1] * lv[1:2], axis=1, keepdims=True))
               - jnp.exp(jnp.sum(lv[2:3] * lv[3:4], axis=1, keepdims=True)) + LAMBDA_INIT)
        acc = acc_ref[...]
        o = (acc[0:tq, 0:LANES] / acc[0:tq, LANES:2 * LANES]
             - lam * (acc[tq:2 * tq, 0:LANES] / acc[tq:2 * tq, LANES:2 * LANES]))
        ms = jnp.mean(o * o, axis=-1, keepdims=True)
        o = o * lax.rsqrt(ms + NORM_EPS) * sw_ref[...] * (1.0 - LAMBDA_INIT)
        o_ref[...] = o.astype(o_ref.dtype)


def _attention(lam_rows, q, kt, za, subln_w, tq, tk):
    b, s, width = q.shape
    heads = width // LANES
    v_col0 = 2 * heads
    return pl.pallas_call(
        functools.partial(_attn_kernel, tq=tq),
        grid=(b, heads, s // tq, s // tk),
        in_specs=[pl.BlockSpec((4, HEAD_DIM), lambda bi, h, qi, ki: (0, 0)),
                  pl.BlockSpec((None, tq, LANES), lambda bi, h, qi, ki: (bi, qi, h)),
                  pl.BlockSpec((None, LANES, tk), lambda bi, h, qi, ki: (bi, h, ki)),
                  pl.BlockSpec((None, tk, LANES), lambda bi, h, qi, ki: (bi, ki, v_col0 + h)),
                  pl.BlockSpec((1, LANES), lambda bi, h, qi, ki: (0, 0))],
        out_specs=pl.BlockSpec((None, tq, LANES), lambda bi, h, qi, ki: (bi, qi, h)),
        out_shape=jax.ShapeDtypeStruct((b, s, width), BF16),
        scratch_shapes=[pltpu.VMEM((2 * tq, LANES), BF16),
                        pltpu.VMEM((2 * tq, LANES), F32),
                        pltpu.VMEM((2 * tq, 2 * LANES), F32)],
        compiler_params=_params(("parallel", "parallel", "parallel", "arbitrary")),
        name="diff_attention",
    )(lam_rows, q, kt, za, subln_w)


HALO = 16


def _rwkv_prep_kernel(z_ref, zp_ref, zn_ref, mu_ref, w0_ref, w2_ref, a0_ref, a2_ref, g2_ref,
                      kk_w_ref, ka_w_ref, rk_ref,
                      r_ref, v_ref, kk_ref, kd0_ref, kd1_ref, ka0_ref, ka1_ref, lw0_ref, lw1_ref,
                      g_ref, bv_ref, *, c):
    i = pl.program_id(1)
    z = z_ref[...].astype(F32)
    tm = z.shape[0]
    row = lax.broadcasted_iota(jnp.int32, z.shape, 0)
    prev_row = jnp.where(i == 0, 0.0, zp_ref[HALO - 1:HALO, :].astype(F32))
    next_row = jnp.where(i == pl.num_programs(1) - 1, 0.0, zn_ref[0:1, :].astype(F32))
    z_up = jnp.where(row == 0, prev_row, pltpu.roll(z, 1, 0))
    z_dn = jnp.where(row == tm - 1, next_row, pltpu.roll(z, tm - 1, 0))
    zs = z + mu_ref[...] * (0.5 * (z_up + z_dn) - z)

    r = zs[:, 0:c]
    k = zs[:, c:2 * c]
    v = zs[:, 2 * c:3 * c]
    wl = zs[:, 3 * c:3 * c + LANES]
    al = zs[:, 3 * c + LANES:3 * c + 2 * LANES]
    gl = zs[:, 3 * c + 2 * LANES:3 * c + 3 * LANES]

    lane = lax.broadcasted_iota(jnp.int32, wl.shape, 1)
    twl = jnp.tanh(wl)
    e, et = _seg_mats(c, HEAD_DIM)

    kk = k * kk_w_ref[...]
    kk = kk * lax.rsqrt(_seg_sum(kk * kk, e, et) + 1e-12)
    kk_ref[...] = kk.astype(kk_ref.dtype)
    r_ref[...] = r.astype(r_ref.dtype)
    v_ref[...] = v.astype(v_ref.dtype)
    g_ref[...] = _dot(_sigmoid(gl).astype(BF16), g2_ref[...]).astype(g_ref.dtype)

    ksum = jnp.zeros_like(k)
    for d, (kd_ref, ka_ref, lw_ref) in enumerate(((kd0_ref, ka0_ref, lw0_ref), (kd1_ref, ka1_ref, lw1_ref))):
        sel = (lane < HEAD_DIM) if d == 0 else (lane >= HEAD_DIM)
        wpre = _dot(jnp.where(sel, twl, 0.0).astype(BF16), w2_ref[...]) + w0_ref[d:d + 1, :]
        lw_ref[...] = (-math.exp(-0.5)) * _sigmoid(wpre)
        a = _sigmoid(_dot(jnp.where(sel, al, 0.0).astype(BF16), a2_ref[...]) + a0_ref[d:d + 1, :])
        kd = k * (1.0 + (a - 1.0) * ka_w_ref[...])
        kd_ref[...] = kd.astype(kd_ref.dtype)
        ka_ref[...] = (kk * a).astype(ka_ref.dtype)
        ksum = ksum + kd
    bonus = _seg_sum(r * ksum * rk_ref[...], e, et)
    bv_ref[...] = (bonus * v).astype(bv_ref.dtype)


def _rwkv_prep(zr, mu, w0, w2, a0, a2, g2, kk_w, ka_w, rk, c, tm):
    b, s, cols = zr.shape
    nh = tm // HALO
    last = s // HALO - 1
    tok = lambda bi, i: (bi, i, 0)
    const = lambda bi, i: (0, 0)
    out = lambda dt: jax.ShapeDtypeStruct((b, s, c), dt)
    ospec = pl.BlockSpec((None, tm, c), tok)
    return pl.pallas_call(
        functools.partial(_rwkv_prep_kernel, c=c),
        grid=(b, s // tm),
        in_specs=[pl.BlockSpec((None, tm, cols), tok),
                  pl.BlockSpec((None, HALO, cols), lambda bi, i: (bi, jnp.maximum(i * nh - 1, 0), 0)),
                  pl.BlockSpec((None, HALO, cols), lambda bi, i: (bi, jnp.minimum((i + 1) * nh, last), 0)),
                  pl.BlockSpec((1, cols), const),
                  pl.BlockSpec((2, c), const),
                  pl.BlockSpec((LANES, c), const),
                  pl.BlockSpec((2, c), const),
                  pl.BlockSpec((LANES, c), const),
                  pl.BlockSpec((LANES, c), const),
                  pl.BlockSpec((1, c), const),
                  pl.BlockSpec((1, c), const),
                  pl.BlockSpec((1, c), const)],
        out_specs=[ospec] * 11,
        out_shape=[out(BF16)] * 7 + [out(F32)] * 2 + [out(BF16)] * 2,
        compiler_params=_params(("parallel", "parallel")),
        name="rwkv_prep",
    )(zr, zr, zr, mu, w0, w2, a0, a2, g2, kk_w, ka_w, rk)


def _wkv_scan_kernel(lwf_ref, rf_ref, vf_ref, kkf_ref, kaf_ref, kdf_ref,
                     lwb_ref, rb_ref, vb_ref, kkb_ref, kab_ref, kdb_ref,
                     yf_ref, yb_ref, st_ref):
    L = SCAN_CHUNK
    P2 = 2 * L
    ci = pl.program_id(1)

    @pl.when(ci == 0)
    def _():
        st_ref[...] = jnp.zeros(st_ref.shape, F32)

    row = lax.broadcasted_iota(jnp.int32, (P2, P2), 0)
    col = lax.broadcasted_iota(jnp.int32, (P2, P2), 1)
    same_head = (row // L) == (col // L)
    rt = row % L
    ct = col % L
    eye = row == col
    ti = lax.broadcasted_iota(jnp.int32, (L, L), 0)
    tj = lax.broadcasted_iota(jnp.int32, (L, L), 1)
    lane_a = lax.broadcasted_iota(jnp.int32, (L, LANES), 1) < HEAD_DIM

    def expand(x):
        return jnp.concatenate([jnp.where(lane_a, x, 0.0), jnp.where(lane_a, 0.0, x)], axis=0)

    units = []
    dirs = ((lwf_ref, rf_ref, vf_ref, kkf_ref, kaf_ref, kdf_ref, yf_ref),
            (lwb_ref, rb_ref, vb_ref, kkb_ref, kab_ref, kdb_ref, yb_ref))
    for d, (lw_ref, r_ref, v_ref, kk_ref, ka_ref, kd_ref, y_ref) in enumerate(dirs):
        if d == 0:
            m_strict = same_head & (rt > ct)
            m_incl = same_head & (rt >= ct)
            tri = jnp.where(tj <= ti, 1.0, 0.0).astype(BF16)
        else:
            m_strict = same_head & (rt < ct)
            m_incl = same_head & (rt <= ct)
            tri = jnp.where(tj >= ti, 1.0, 0.0).astype(BF16)
        lw = lw_ref[...]
        hi = lw.astype(BF16)
        r1 = lw - hi.astype(F32)
        mid = r1.astype(BF16)
        lo = (r1 - mid.astype(F32)).astype(BF16)
        cum = _dot(tri, hi) + _dot(tri, mid) + _dot(tri, lo)
        cum_last = cum[L - 1:L, :] if d == 0 else cum[0:1, :]
        ec = jnp.exp(cum)
        enc = jnp.exp(-cum)
        p_last = jnp.exp(cum_last)
        a_t = -kk_ref[...].astype(F32) * jnp.exp(cum - lw)
        r_t = r_ref[...].astype(F32) * ec
        b_t = ka_ref[...].astype(F32) * enc
        k_t = kd_ref[...].astype(F32) * enc
        b_h = b_t * p_last
        k_h = k_t * p_last
        vv = v_ref[...].astype(F32)

        for p in range(lw.shape[1] // LANES):
            sl = slice(p * LANES, (p + 1) * LANES)
            units.append(dict(
                d=d, p=p, sl=sl, y_ref=y_ref, m_strict=m_strict, m_incl=m_incl,
                ea=expand(a_t[:, sl]), er=expand(r_t[:, sl]), vexp=expand(vv[:, sl]),
                bk2=jnp.concatenate([b_t[:, sl]] * 2 + [k_t[:, sl]] * 2, axis=0).astype(BF16),
                bh=b_h[:, sl], kh=k_h[:, sl], p_last=p_last[:, sl]))

    for u in units:
        g = _dot_nt(jnp.concatenate([u["ea"], u["er"]], axis=0).astype(BF16), u["bk2"])
        u["xab"] = jnp.where(u["m_strict"], g[0:P2, 0:P2], 0.0)
        u["xak"] = jnp.where(u["m_strict"], g[0:P2, P2:2 * P2], 0.0).astype(BF16)
        u["xr"] = jnp.concatenate([jnp.where(u["m_incl"], g[P2:2 * P2, 0:P2], 0.0),
                                   jnp.where(u["m_incl"], g[P2:2 * P2, P2:2 * P2], 0.0)], axis=1).astype(BF16)
        u["tinv"] = jnp.where(eye, 1.0, u["xab"])
        u["pw"] = u["xab"].astype(BF16)
    for _ in range(L.bit_length() - 2):
        for u in units:
            u["pw"] = _dot(u["pw"], u["pw"]).astype(BF16)
        for u in units:
            u["tinv"] = u["tinv"] + _dot(u["tinv"].astype(BF16), u["pw"])
    for u in units:
        u["xv"] = _dot(u["xak"], u["vexp"].astype(BF16))
    for u in units:
        u["wu0"] = _dot(u["tinv"].astype(BF16), jnp.concatenate([u["ea"], u["xv"]], axis=1).astype(BF16))
        u["bkt"] = jnp.concatenate([expand(u["bh"]).T, expand(u["kh"]).T], axis=1).astype(BF16)
        u["p_col"] = jnp.broadcast_to(u["p_last"], (P2, LANES)).T
    for u in units:
        u["st"] = st_ref[u["d"], u["p"]]
        u["ws"] = _dot(jnp.concatenate([u["wu0"][:, 0:LANES], u["er"]], axis=0).astype(BF16), u["st"].astype(BF16))
    for u in units:
        uu = u["ws"][0:P2] + u["wu0"][:, LANES:2 * LANES]
        u["uv"] = jnp.concatenate([uu, u["vexp"]], axis=0).astype(BF16)
    for u in units:
        yexp = u["ws"][P2:2 * P2] + _dot(u["xr"], u["uv"])
        u["y_ref"][:, u["sl"]] = (yexp[0:L] + yexp[L:P2]).astype(u["y_ref"].dtype)
    for u in units:
        st_ref[u["d"], u["p"]] = u["p_col"] * u["st"] + _dot(u["bkt"], u["uv"])


def _wkv_scan(lw0, lw1, r, v, kk, ka0, ka1, kd0, kd1):
    b, s, c = r.shape
    L = SCAN_CHUNK
    nc = s // L
    fwd = lambda bi, i: (bi, i, 0)
    bwd = lambda bi, i: (bi, nc - 1 - i, 0)
    fs = pl.BlockSpec((None, L, c), fwd)
    bs = pl.BlockSpec((None, L, c), bwd)
    return pl.pallas_call(
        _wkv_scan_kernel,
        grid=(b, nc),
        in_specs=[fs] * 6 + [bs] * 6,
        out_specs=[fs, bs],
        out_shape=[jax.ShapeDtypeStruct((b, s, c), F32)] * 2,
        scratch_shapes=[pltpu.VMEM((2, c // LANES, LANES, LANES), F32)],
        compiler_params=_params(("parallel", "arbitrary")),
        name="wkv_scan",
    )(lw0, r, v, kk, ka0, kd0, lw1, r, v, kk, ka1, kd1)


def _post_kernel(x_ref, oa_ref, yf_ref, yb_ref, g_ref, bv_ref, zg_ref, lnw_ref, lnb_ref,
                 woa_ref, wor_ref, wout_ref, n2w_ref, wq_ref, x1_ref, xn_ref, qp_ref):
    y = yf_ref[...] + yb_ref[...]
    c = y.shape[1]
    e, et = _seg_mats(c, HEAD_DIM)
    mu = _seg_sum(y, e, et) * (1.0 / HEAD_DIM)
    dlt = y - mu
    var = _seg_sum(dlt * dlt, e, et) * (1.0 / HEAD_DIM)
    yn = dlt * lax.rsqrt(var + RWKV_LN_EPS) * lnw_ref[...] + lnb_ref[...]
    out = ((yn + bv_ref[...].astype(F32)) * g_ref[...].astype(F32)).astype(BF16)
    h_rwkv = _dot(out, wor_ref[...])
    h_attn = _dot(oa_ref[...], woa_ref[...])
    zg = zg_ref[...].astype(F32)
    merged = _sigmoid(zg[:, 0:c]) * h_attn + _sigmoid(zg[:, c:2 * c]) * h_rwkv
    x1 = x_ref[...] + _dot(merged.astype(BF16), wout_ref[...])
    x1_ref[...] = x1
    ms = jnp.mean(x1 * x1, axis=-1, keepdims=True)
    xn = (x1 * lax.rsqrt(ms + NORM_EPS) * n2w_ref[...]).astype(BF16)
    xn_ref[...] = xn
    qp = _dot(xn, wq_ref[...]).astype(BF16)
    for j in range(qp_ref.shape[0]):
        qp_ref[j] = qp[:, j * LANES:(j + 1) * LANES]


def _post(x, oa, yf, yb, g, bv, zg, lnw, lnb, woa, wor, wout, n2w, wq, tm):
    t, d = x.shape
    nq = wq.shape[1] // LANES
    tok = lambda i: (i, 0)
    const = lambda i: (0, 0)
    tspec = lambda w: pl.BlockSpec((tm, w), tok)
    cspec = lambda a: pl.BlockSpec(a.shape, const)
    return pl.pallas_call(
        _post_kernel,
        grid=(t // tm,),
        in_specs=[tspec(d), tspec(d), tspec(d), tspec(d), tspec(d), tspec(d), tspec(2 * d),
                  cspec(lnw), cspec(lnb), cspec(woa), cspec(wor), cspec(wout), cspec(n2w), cspec(wq)],
        out_specs=[tspec(d), tspec(d), pl.BlockSpec((nq, tm, LANES), lambda i: (0, i, 0))],
        out_shape=[jax.ShapeDtypeStruct((t, d), F32), jax.ShapeDtypeStruct((t, d), BF16),
                   jax.ShapeDtypeStruct((nq, t, LANES), BF16)],
        compiler_params=_params(("parallel",)),
        name="post_merge",
    )(x, oa, yf, yb, g, bv, zg, lnw, lnb, woa, wor, wout, n2w, wq)


def _top_values(cur, k, exact):
    rows = lax.broadcasted_iota(jnp.int32, cur.shape, 0).astype(F32)
    vals = []
    for _ in range(k):
        m = jnp.max(cur, axis=0, keepdims=True)
        vals.append(m)
        if exact:
            first = jnp.min(jnp.where(cur == m, rows, float(cur.shape[0])), axis=0, keepdims=True)
            cur = jnp.where(rows == first, -jnp.inf, cur)
        else:
            cur = jnp.where(cur == m, -jnp.inf, cur)
    removed = jnp.sum(jnp.where(cur == -jnp.inf, 1.0, 0.0), axis=0, keepdims=True)
    return vals, removed


def _peer_select(s0, s1, k, exact):
    a, ra = _top_values(s0, k, exact)
    b, rb = _top_values(s1, k, exact)
    amat = jnp.concatenate(a, axis=0)
    bmat = jnp.concatenate(b, axis=0)
    h8 = k // 2
    cand = jnp.concatenate([a[0] + bmat, a[1] + bmat[0:h8], amat[h8:k] + b[0]]
                           + [a[i] + bmat[0:h8] for i in range(2, h8)], axis=0)
    best, rc = _top_values(cand, k, exact)
    z = jnp.zeros_like(best[0])
    for val in best:
        z = z + jnp.exp(val - best[0])
    return bmat, a[0], best[k - 1], z, jnp.maximum(jnp.maximum(ra, rb), rc)


def _peer_kernel(xn_ref, qp_ref, sk_ref, u_ref, vt_ref, x1_ref, o_ref,
                 cnt_ref, e0_ref, rank_ref, e1_ref, sel_ref, acc_ref):
    jb = pl.program_id(1)
    tt = xn_ref.shape[0]
    K = PEER_TOPK
    rows_per_blk = u_ref.shape[0] // PEER_NKEYS

    @pl.when(jb == 0)
    def _():
        acc_ref[...] = jnp.zeros(acc_ref.shape, F32)

        def head_body(h, carry):
            s0 = _dot_nt(sk_ref[2 * h], qp_ref[2 * h])
            s1 = _dot_nt(sk_ref[2 * h + 1], qp_ref[2 * h + 1])

            def store(sel):
                bmat, a0, thr, z, removed = sel
                sel_ref[0:K] = bmat
                sel_ref[K:K + 8] = jnp.broadcast_to(a0, (8, tt))
                sel_ref[K + 8:K + 16] = jnp.broadcast_to(thr, (8, tt))
                sel_ref[K + 16:K + 24] = jnp.broadcast_to(z, (8, tt))
                return removed

            removed = store(_peer_select(s0, s1, K, exact=False))

            @pl.when(jnp.max(removed) > K)
            def _():
                store(_peer_select(s0, s1, K, exact=True))

            bmat = sel_ref[0:K]
            a0 = sel_ref[K:K + 1]
            thr = sel_ref[K + 8:K + 9]
            z = sel_ref[K + 16:K + 17]
            cnt = jnp.zeros(s0.shape, F32)
            rank = jnp.full(s1.shape, float(PEER_NKEYS), F32)
            for kk in range(K):
                cnt = cnt + jnp.where(s0 + bmat[kk:kk + 1] >= thr, 1.0, 0.0)
            for kk in reversed(range(K)):
                rank = jnp.where(s1 == bmat[kk:kk + 1], float(kk + 1), rank)
            cnt_ref[h] = cnt
            e0_ref[h] = jnp.exp(s0 - a0) / z
            rank_ref[h] = rank.astype(BF16)
            e1_ref[h] = jnp.exp(s1 - bmat[0:1]).astype(BF16)
            return carry

        lax.fori_loop(0, PEER_HEADS, head_body, 0)

    sub = 2 * PEER_NKEYS
    xn = xn_ref[...]
    out = None
    n_sub = u_ref.shape[0] // sub
    hpres = [None] * n_sub
    hpres[0] = _dot_nt(u_ref[0:sub, :], xn)
    for c in range(n_sub):
        if c + 1 < n_sub:
            hpres[c + 1] = _dot_nt(u_ref[(c + 1) * sub:(c + 2) * sub, :], xn)
        g_rows = []
        for il in range(c * sub // PEER_NKEYS, (c + 1) * sub // PEER_NKEYS):
            i = jb * rows_per_blk + il
            acc = None
            for h in range(PEER_HEADS):
                crow = cnt_ref[h, pl.ds(i, 1), :].astype(BF16)
                e0row = e0_ref[h, pl.ds(i, 1), :].astype(BF16)
                contrib = jnp.where(rank_ref[h] <= crow, e0row * e1_ref[h], jnp.zeros((), BF16))
                acc = contrib if acc is None else acc + contrib
            g_rows.append(acc)
        gates = jnp.concatenate(g_rows, axis=0)
        hpre = hpres[c]
        gelu = 0.5 * hpre * (1.0 + lax.erf(hpre * (2.0 ** -0.5)))
        hg = gelu.astype(BF16) * gates
        part = _dot(vt_ref[:, c * sub:(c + 1) * sub], hg)
        out = part if out is None else out + part
    acc_ref[...] += out

    @pl.when(jb == pl.num_programs(1) - 1)
    def _():
        o_ref[...] = x1_ref[...] + acc_ref[...].T


def _peer(xn, qp, sk, u, vt, x1, tt, eb):
    t, d = xn.shape
    n_exp = u.shape[0]
    nq = qp.shape[0]
    return pl.pallas_call(
        _peer_kernel,
        grid=(t // tt, n_exp // eb),
        in_specs=[pl.BlockSpec((tt, d), lambda i, j: (i, 0)),
                  pl.BlockSpec((nq, tt, LANES), lambda i, j: (0, i, 0)),
                  pl.BlockSpec(sk.shape, lambda i, j: (0, 0, 0)),
                  pl.BlockSpec((eb, d), lambda i, j: (j, 0)),
                  pl.BlockSpec((d, eb), lambda i, j: (0, j)),
                  pl.BlockSpec((tt, d), lambda i, j: (i, 0))],
        out_specs=pl.BlockSpec((tt, d), lambda i, j: (i, 0)),
        out_shape=jax.ShapeDtypeStruct((t, d), F32),
        scratch_shapes=[pltpu.VMEM((PEER_HEADS, PEER_NKEYS, tt), F32),
                        pltpu.VMEM((PEER_HEADS, PEER_NKEYS, tt), F32),
                        pltpu.VMEM((PEER_HEADS, PEER_NKEYS, tt), BF16),
                        pltpu.VMEM((PEER_HEADS, PEER_NKEYS, tt), BF16),
                        pltpu.VMEM((PEER_TOPK + 24, tt), F32),
                        pltpu.VMEM((d, tt), F32)],
        compiler_params=_params(("parallel", "arbitrary")),
        name="peer",
    )(xn, qp, sk, u, vt, x1)


def _rope_tables(s):
    inv = 1.0 / (ROPE_THETA ** (jnp.arange(0, HEAD_DIM, 2, dtype=F32) / HEAD_DIM))
    ang = jnp.arange(s, dtype=F32)[:, None] * inv[None, :]
    cos, sin = jnp.cos(ang), jnp.sin(ang)
    reps = LANES // HEAD_DIM
    cos_t = jnp.tile(jnp.concatenate([cos, cos], axis=1), (1, reps))
    sin_t = jnp.tile(jnp.concatenate([-sin, sin], axis=1), (1, reps))
    return cos_t, sin_t


def _tile(n, pref):
    return pref if n % pref == 0 else n


def _trunk(x, P):
    b, s, d = x.shape
    t = b * s
    c = d
    xf = x.reshape(t, d)
    tm = _tile(t, 512)
    za = _norm_matmul(xf, P["norm1_w"], P["w_a"], tm).reshape(b, s, -1)
    zr = _norm_matmul(xf, P["norm1_w"], P["w_r"], tm).reshape(b, s, -1)
    zg = _norm_matmul(xf, P["norm1_w"], P["w_g"], tm)

    cos_t, sin_t = _rope_tables(s)
    ts = _tile(s, 512)
    q = _attn_prep(za, cos_t, sin_t, P["q_norm_w"], 0, HEAD_DIM ** -0.5 * math.log2(math.e), ts, False)
    kt = _attn_prep(za, cos_t, sin_t, P["k_norm_w"], 1, 1.0, ts, True)
    oa = _attention(P["lam_rows"], q, kt, za, P["subln_w"], ts, _tile(s, 1024))

    tp = _tile(s, 256)
    (r, v, kk, kd0, kd1, ka0, ka1, lw0, lw1, g, bv) = _rwkv_prep(
        zr, P["shift_mu"], P["decay_w0"], P["decay_w2"], P["iclr_a0"], P["iclr_a2"], P["gate_g2"],
        P["k_k"], P["k_a"], P["r_k"], c, tp)
    yf, yb = _wkv_scan(lw0, lw1, r, v, kk, ka0, ka1, kd0, kd1)

    flat = lambda a: a.reshape(t, a.shape[-1])
    x1, xn, qp = _post(xf, flat(oa), flat(yf), flat(yb), flat(g), flat(bv), zg, P["lnx_w"], P["lnx_b"],
                       P["w_o_attn"], P["w_o_rwkv"], P["w_out"], P["norm2_w"], P["peer_wq"], _tile(t, 256))
    y = _peer(xn, qp, P["peer_subkeys"], P["peer_u"], P["peer_vt"], x1, _tile(t, 512), 1024)
    return y.reshape(b, s, d)


def kernel(x_prompt, x_sample, norm1_w, w_in, q_norm_w, k_norm_w, lambda_q1, lambda_k1, lambda_q2, lambda_k2, subln_w, w_o_attn, shift_mu, decay_w0, decay_w2, iclr_a0, iclr_a2, gate_g2, k_k, k_a, r_k, lnx_w, lnx_b, w_o_rwkv, w_out, norm2_w, peer_wq, peer_subkeys, peer_u, peer_v):
    d = x_prompt.shape[-1]
    c = d
    att_cols = 3 * d
    rwkv_cols = shift_mu.shape[-1]
    l = 0
    w = w_in[l].astype(BF16)
    nsub = d // HEAD_DIM
    P = {
        "norm1_w": norm1_w[l][None, :],
        "w_a": w[:, :att_cols],
        "w_r": w[:, att_cols:att_cols + rwkv_cols],
        "w_g": w[:, att_cols + rwkv_cols:],
        "q_norm_w": jnp.tile(q_norm_w[l], nsub)[None, :],
        "k_norm_w": jnp.tile(k_norm_w[l], nsub)[None, :],
        "lam_rows": jnp.stack([lambda_q1[l], lambda_k1[l], lambda_q2[l], lambda_k2[l]]),
        "subln_w": subln_w[l][None, :],
        "w_o_attn": w_o_attn[l].astype(BF16),
        "shift_mu": shift_mu[l][None, :],
        "decay_w0": decay_w0[l],
        "decay_w2": decay_w2[l].reshape(-1, c).astype(BF16),
        "iclr_a0": iclr_a0[l],
        "iclr_a2": iclr_a2[l].reshape(-1, c).astype(BF16),
        "gate_g2": gate_g2[l].astype(BF16),
        "k_k": k_k[l][None, :],
        "k_a": k_a[l][None, :],
        "r_k": r_k[l].reshape(1, c),
        "lnx_w": lnx_w[l][None, :],
        "lnx_b": lnx_b[l][None, :],
        "w_o_rwkv": w_o_rwkv[l].astype(BF16),
        "w_out": w_out[l].astype(BF16),
        "norm2_w": norm2_w[l][None, :],
        "peer_wq": peer_wq[l].astype(BF16),
        "peer_subkeys": peer_subkeys[l].reshape(-1, PEER_NKEYS, peer_subkeys.shape[-1]).astype(BF16),
        "peer_u": peer_u[l].astype(BF16),
        "peer_vt": peer_v[l].astype(BF16).T,
    }
    return (_trunk(x_prompt, P), _trunk(x_sample, P))
```

```python
import functools
import math

import jax
import jax.numpy as jnp
from jax import lax
from jax.experimental import pallas as pl
from jax.experimental.pallas import tpu as pltpu

F32 = jnp.float32
BF16 = jnp.bfloat16

LANES = 128
HEAD_DIM = 64
ATT_HEADS = 8
NORM_EPS = 1e-6
RWKV_LN_EPS = 64e-5
ROPE_THETA = 10000.0
LAMBDA_INIT = 0.8 - 0.6 * math.exp(-0.3 * 0)
SCAN_CHUNK = 64
PEER_NKEYS = 128
PEER_TOPK = 16
PEER_HEADS = 8
VMEM_LIMIT = 56 * 1024 * 1024


def _params(sem):
    return pltpu.CompilerParams(dimension_semantics=sem, vmem_limit_bytes=VMEM_LIMIT)


def _dot(a, b):
    return jnp.dot(a, b, preferred_element_type=F32)


def _dot_nt(a, b):
    return lax.dot_general(a, b, (((1,), (1,)), ((), ())), preferred_element_type=F32)


def _split_dot(x, m):
    hi = x.astype(BF16)
    lo = (x - hi.astype(F32)).astype(BF16)
    return _dot(hi, m) + _dot(lo, m)


def _seg_mats(width, seg):
    lane = lax.broadcasted_iota(jnp.int32, (width, LANES), 0)
    grp = lax.broadcasted_iota(jnp.int32, (width, LANES), 1)
    e = jnp.where(lane // seg == grp, 1.0, 0.0).astype(BF16)
    grp_t = lax.broadcasted_iota(jnp.int32, (LANES, width), 0)
    lane_t = lax.broadcasted_iota(jnp.int32, (LANES, width), 1)
    et = jnp.where(lane_t // seg == grp_t, 1.0, 0.0).astype(BF16)
    return e, et


def _seg_sum(x, e, et):
    return _split_dot(_split_dot(x, e), et)


def _sigmoid(x):
    return 1.0 / (1.0 + jnp.exp(-x))


def _norm_matmul_kernel(x_ref, nw_ref, w_ref, o_ref):
    x = x_ref[...]
    ms = jnp.mean(x * x, axis=-1, keepdims=True)
    h = (x * lax.rsqrt(ms + NORM_EPS) * nw_ref[...]).astype(BF16)
    o_ref[...] = _dot(h, w_ref[...]).astype(o_ref.dtype)


def _norm_matmul(x, nw, w, tm):
    t, d = x.shape
    n = w.shape[1]
    return pl.pallas_call(
        _norm_matmul_kernel,
        grid=(t // tm,),
        in_specs=[pl.BlockSpec((tm, d), lambda i: (i, 0)),
                  pl.BlockSpec((1, d), lambda i: (0, 0)),
                  pl.BlockSpec((d, n), lambda i: (0, 0))],
        out_specs=pl.BlockSpec((tm, n), lambda i: (i, 0)),
        out_shape=jax.ShapeDtypeStruct((t, n), BF16),
        compiler_params=_params(("parallel",)),
        name="norm_matmul",
    )(x, nw, w)


def _attn_prep_kernel(z_ref, cos_ref, sin_ref, w_ref, o_ref, *, scale, transpose_out):
    x = z_ref[...].astype(F32)
    width = x.shape[1]
    e, et = _seg_mats(width, HEAD_DIM)
    ms = _seg_sum(x * x, e, et) * (1.0 / HEAD_DIM)
    y = x * lax.rsqrt(ms + NORM_EPS) * w_ref[...]
    reps = width // LANES
    c = jnp.concatenate([cos_ref[...]] * reps, axis=1)
    s = jnp.concatenate([sin_ref[...]] * reps, axis=1)
    lane = lax.broadcasted_iota(jnp.int32, y.shape, 1)
    half = HEAD_DIM // 2
    partner = jnp.where(lane % HEAD_DIM < half, pltpu.roll(y, width - half, 1), pltpu.roll(y, half, 1))
    out = (y * c + partner * s) * scale
    if transpose_out:
        for j in range(reps):
            o_ref[j * LANES:(j + 1) * LANES, :] = out[:, j * LANES:(j + 1) * LANES].T.astype(o_ref.dtype)
    else:
        o_ref[...] = out.astype(o_ref.dtype)


def _attn_prep(za, cos_t, sin_t, w_tiled, col_block, scale, tm, transpose_out):
    b, s, _ = za.shape
    width = w_tiled.shape[1]
    if transpose_out:
        out_spec = pl.BlockSpec((None, width, tm), lambda bi, i: (bi, 0, i))
        out_shape = jax.ShapeDtypeStruct((b, width, s), BF16)
    else:
        out_spec = pl.BlockSpec((None, tm, width), lambda bi, i: (bi, i, 0))
        out_shape = jax.ShapeDtypeStruct((b, s, width), BF16)
    return pl.pallas_call(
        functools.partial(_attn_prep_kernel, scale=scale, transpose_out=transpose_out),
        grid=(b, s // tm),
        in_specs=[pl.BlockSpec((None, tm, width), lambda bi, i: (bi, i, col_block)),
                  pl.BlockSpec((tm, LANES), lambda bi, i: (i, 0)),
                  pl.BlockSpec((tm, LANES), lambda bi, i: (i, 0)),
                  pl.BlockSpec((1, width), lambda bi, i: (0, 0))],
        out_specs=out_spec,
        out_shape=out_shape,
        compiler_params=_params(("parallel", "parallel")),
        name="attn_prep",
    )(za, cos_t, sin_t, w_tiled)


ATTN_ROW_CHUNK = 256


def _attn_kernel(lam_ref, q_ref, kt_ref, v_ref, sw_ref, o_ref, qs_ref, m_ref, acc_ref, s0_ref, *, tq, tk):
    q = q_ref[...]
    lane = lax.broadcasted_iota(jnp.int32, q.shape, 1)
    zero = jnp.zeros_like(q)
    qs_ref[0:tq, :] = jnp.where(lane < HEAD_DIM, q, zero)
    qs_ref[tq:2 * tq, :] = jnp.where(lane < HEAD_DIM, zero, q)
    m_ref[...] = jnp.full(m_ref.shape, -jnp.inf, F32)
    acc_ref[...] = jnp.zeros(acc_ref.shape, F32)

    rc = min(ATTN_ROW_CHUNK, 2 * tq)
    n_chunks = 2 * tq // rc
    n_tiles = kt_ref.shape[1] // tk
    s0_ref[...] = _dot(qs_ref[0:rc, :], kt_ref[:, 0:tk])

    def key_tile(j, carry):
        off = pl.multiple_of(j * tk, tk)
        v = v_ref[pl.ds(off, tk), :]
        v_ones = jnp.concatenate([v, jnp.ones_like(v)], axis=1)
        kt = kt_ref[:, pl.ds(off, tk)]
        scores = [None] * n_chunks
        scores[0] = s0_ref[...]
        for c in range(n_chunks):
            rows = slice(c * rc, (c + 1) * rc)
            if c + 1 < n_chunks:
                scores[c + 1] = _dot(qs_ref[(c + 1) * rc:(c + 2) * rc, :], kt)
            else:
                off_next = pl.multiple_of(jnp.minimum(j + 1, n_tiles - 1) * tk, tk)
                s0_ref[...] = _dot(qs_ref[0:rc, :], kt_ref[:, pl.ds(off_next, tk)])
            s = scores[c]
            m_prev = m_ref[rows, :]
            m_new = jnp.maximum(m_prev, jnp.max(s, axis=1, keepdims=True))
            alpha = jnp.exp2(m_prev - m_new)
            p = jnp.exp2((s - jnp.concatenate([m_new] * (tk // LANES), axis=1)).astype(BF16))
            acc_ref[rows, :] = (jnp.concatenate([alpha, alpha], axis=1) * acc_ref[rows, :]
                                + _dot(p, v_ones))
            m_ref[rows, :] = m_new
        return carry

    lax.fori_loop(0, n_tiles, key_tile, 0)

    lv = lam_ref[...]
    lam = (jnp.exp(jnp.sum(lv[0:1] * lv[1:2], axis=1, keepdims=True))
           - jnp.exp(jnp.sum(lv[2:3] * lv[3:4], axis=1, keepdims=True)) + LAMBDA_INIT)
    acc = acc_ref[...]
    o = (acc[0:tq, 0:LANES] / acc[0:tq, LANES:2 * LANES]
         - lam * (acc[tq:2 * tq, 0:LANES] / acc[tq:2 * tq, LANES:2 * LANES]))
    ms = jnp.mean(o * o, axis=-1, keepdims=True)
    o = o * lax.rsqrt(ms + NORM_EPS) * sw_ref[...] * (1.0 - LAMBDA_INIT)
    o_ref[...] = o.astype(o_ref.dtype)


def _attention(lam_rows, q, kt, za, subln_w, tq, tk):
    b, s, width = q.shape
    heads = width // LANES
    v_col0 = 2 * heads
    return pl.pallas_call(
        functools.partial(_attn_kernel, tq=tq, tk=tk),
        grid=(b, heads, s // tq),
        in_specs=[pl.BlockSpec((4, HEAD_DIM), lambda bi, h, qi: (0, 0)),
                  pl.BlockSpec((None, tq, LANES), lambda bi, h, qi: (bi, qi, h)),
                  pl.BlockSpec((None, LANES, s), lambda bi, h, qi: (bi, h, 0)),
                  pl.BlockSpec((None, s, LANES), lambda bi, h, qi: (bi, 0, v_col0 + h)),
                  pl.BlockSpec((1, LANES), lambda bi, h, qi: (0, 0))],
        out_specs=pl.BlockSpec((None, tq, LANES), lambda bi, h, qi: (bi, qi, h)),
        out_shape=jax.ShapeDtypeStruct((b, s, width), BF16),
        scratch_shapes=[pltpu.VMEM((2 * tq, LANES), BF16),
                        pltpu.VMEM((2 * tq, LANES), F32),
                        pltpu.VMEM((2 * tq, 2 * LANES), F32),
                        pltpu.VMEM((min(ATTN_ROW_CHUNK, 2 * tq), tk), F32)],
        compiler_params=_params(("parallel", "parallel", "arbitrary")),
        name="diff_attention",
    )(lam_rows, q, kt, za, subln_w)


HALO = 16


def _rwkv_prep_kernel(z_ref, zp_ref, zn_ref, mu_ref, w0_ref, w2_ref, a0_ref, a2_ref, g2_ref,
                      kk_w_ref, ka_w_ref, rk_ref,
                      r_ref, v_ref, kk_ref, kd0_ref, kd1_ref, ka0_ref, ka1_ref, lw0_ref, lw1_ref,
                      g_ref, bv_ref, *, c):
    i = pl.program_id(1)
    z = z_ref[...].astype(F32)
    tm = z.shape[0]
    row = lax.broadcasted_iota(jnp.int32, z.shape, 0)
    prev_row = jnp.where(i == 0, 0.0, zp_ref[HALO - 1:HALO, :].astype(F32))
    next_row = jnp.where(i == pl.num_programs(1) - 1, 0.0, zn_ref[0:1, :].astype(F32))
    z_up = jnp.where(row == 0, prev_row, pltpu.roll(z, 1, 0))
    z_dn = jnp.where(row == tm - 1, next_row, pltpu.roll(z, tm - 1, 0))
    zs = z + mu_ref[...] * (0.5 * (z_up + z_dn) - z)

    r = zs[:, 0:c]
    k = zs[:, c:2 * c]
    v = zs[:, 2 * c:3 * c]
    wl = zs[:, 3 * c:3 * c + LANES]
    al = zs[:, 3 * c + LANES:3 * c + 2 * LANES]
    gl = zs[:, 3 * c + 2 * LANES:3 * c + 3 * LANES]

    lane = lax.broadcasted_iota(jnp.int32, wl.shape, 1)
    twl = jnp.tanh(wl)
    e, et = _seg_mats(c, HEAD_DIM)

    kk = k * kk_w_ref[...]
    kk = kk * lax.rsqrt(_seg_sum(kk * kk, e, et) + 1e-12)
    kk_ref[...] = kk.astype(kk_ref.dtype)
    r_ref[...] = r.astype(r_ref.dtype)
    v_ref[...] = v.astype(v_ref.dtype)
    g_ref[...] = _dot(_sigmoid(gl).astype(BF16), g2_ref[...]).astype(g_ref.dtype)

    ksum = jnp.zeros_like(k)
    for d, (kd_ref, ka_ref, lw_ref) in enumerate(((kd0_ref, ka0_ref, lw0_ref), (kd1_ref, ka1_ref, lw1_ref))):
        sel = (lane < HEAD_DIM) if d == 0 else (lane >= HEAD_DIM)
        wpre = _dot(jnp.where(sel, twl, 0.0).astype(BF16), w2_ref[...]) + w0_ref[d:d + 1, :]
        lw_ref[...] = (-math.exp(-0.5)) * _sigmoid(wpre)
        a = _sigmoid(_dot(jnp.where(sel, al, 0.0).astype(BF16), a2_ref[...]) + a0_ref[d:d + 1, :])
        kd = k * (1.0 + (a - 1.0) * ka_w_ref[...])
        kd_ref[...] = kd.astype(kd_ref.dtype)
        ka_ref[...] = (kk * a).astype(ka_ref.dtype)
        ksum = ksum + kd
    bonus = _seg_sum(r * ksum * rk_ref[...], e, et)
    bv_ref[...] = (bonus * v).astype(bv_ref.dtype)


def _rwkv_prep(zr, mu, w0, w2, a0, a2, g2, kk_w, ka_w, rk, c, tm):
    b, s, cols = zr.shape
    nh = tm // HALO
    last = s // HALO - 1
    tok = lambda bi, i: (bi, i, 0)
    const = lambda bi, i: (0, 0)
    out = lambda dt: jax.ShapeDtypeStruct((b, s, c), dt)
    ospec = pl.BlockSpec((None, tm, c), tok)
    return pl.pallas_call(
        functools.partial(_rwkv_prep_kernel, c=c),
        grid=(b, s // tm),
        in_specs=[pl.BlockSpec((None, tm, cols), tok),
                  pl.BlockSpec((None, HALO, cols), lambda bi, i: (bi, jnp.maximum(i * nh - 1, 0), 0)),
                  pl.BlockSpec((None, HALO, cols), lambda bi, i: (bi, jnp.minimum((i + 1) * nh, last), 0)),
                  pl.BlockSpec((1, cols), const),
                  pl.BlockSpec((2, c), const),
                  pl.BlockSpec((LANES, c), const),
                  pl.BlockSpec((2, c), const),
                  pl.BlockSpec((LANES, c), const),
                  pl.BlockSpec((LANES, c), const),
                  pl.BlockSpec((1, c), const),
                  pl.BlockSpec((1, c), const),
                  pl.BlockSpec((1, c), const)],
        out_specs=[ospec] * 11,
        out_shape=[out(BF16)] * 7 + [out(F32)] * 2 + [out(BF16)] * 2,
        compiler_params=_params(("parallel", "parallel")),
        name="rwkv_prep",
    )(zr, zr, zr, mu, w0, w2, a0, a2, g2, kk_w, ka_w, rk)


def _wkv_scan_kernel(lwf_ref, rf_ref, vf_ref, kkf_ref, kaf_ref, kdf_ref,
                     lwb_ref, rb_ref, vb_ref, kkb_ref, kab_ref, kdb_ref,
                     yf_ref, yb_ref, st_ref):
    L = SCAN_CHUNK
    P2 = 2 * L
    ci = pl.program_id(1)

    @pl.when(ci == 0)
    def _():
        st_ref[...] = jnp.zeros(st_ref.shape, F32)

    row = lax.broadcasted_iota(jnp.int32, (P2, P2), 0)
    col = lax.broadcasted_iota(jnp.int32, (P2, P2), 1)
    same_head = (row // L) == (col // L)
    rt = row % L
    ct = col % L
    eye = row == col
    ti = lax.broadcasted_iota(jnp.int32, (L, L), 0)
    tj = lax.broadcasted_iota(jnp.int32, (L, L), 1)
    lane_a = lax.broadcasted_iota(jnp.int32, (L, LANES), 1) < HEAD_DIM

    def expand(x):
        return jnp.concatenate([jnp.where(lane_a, x, 0.0), jnp.where(lane_a, 0.0, x)], axis=0)

    units = []
    dirs = ((lwf_ref, rf_ref, vf_ref, kkf_ref, kaf_ref, kdf_ref, yf_ref),
            (lwb_ref, rb_ref, vb_ref, kkb_ref, kab_ref, kdb_ref, yb_ref))
    for d, (lw_ref, r_ref, v_ref, kk_ref, ka_ref, kd_ref, y_ref) in enumerate(dirs):
        if d == 0:
            m_strict = same_head & (rt > ct)
            m_incl = same_head & (rt >= ct)
            tri = jnp.where(tj <= ti, 1.0, 0.0).astype(BF16)
        else:
            m_strict = same_head & (rt < ct)
            m_incl = same_head & (rt <= ct)
            tri = jnp.where(tj >= ti, 1.0, 0.0).astype(BF16)
        lw = lw_ref[...]
        hi = lw.astype(BF16)
        r1 = lw - hi.astype(F32)
        mid = r1.astype(BF16)
        lo = (r1 - mid.astype(F32)).astype(BF16)
        cum = _dot(tri, hi) + _dot(tri, mid) + _dot(tri, lo)
        cum_last = cum[L - 1:L, :] if d == 0 else cum[0:1, :]
        ec = jnp.exp(cum)
        enc = jnp.exp(-cum)
        p_last = jnp.exp(cum_last)
        a_t = -kk_ref[...].astype(F32) * jnp.exp(cum - lw)
        r_t = r_ref[...].astype(F32) * ec
        b_t = ka_ref[...].astype(F32) * enc
        k_t = kd_ref[...].astype(F32) * enc
        b_h = b_t * p_last
        k_h = k_t * p_last
        vv = v_ref[...].astype(F32)

        for p in range(lw.shape[1] // LANES):
            sl = slice(p * LANES, (p + 1) * LANES)
            units.append(dict(
                d=d, p=p, sl=sl, y_ref=y_ref, m_strict=m_strict, m_incl=m_incl,
                ea=expand(a_t[:, sl]), er=expand(r_t[:, sl]), vexp=expand(vv[:, sl]),
                bk2=jnp.concatenate([b_t[:, sl]] * 2 + [k_t[:, sl]] * 2, axis=0).astype(BF16),
                bh=b_h[:, sl], kh=k_h[:, sl], p_last=p_last[:, sl]))

    for u in units:
        g = _dot_nt(jnp.concatenate([u["ea"], u["er"]], axis=0).astype(BF16), u["bk2"])
        u["xab"] = jnp.where(u["m_strict"], g[0:P2, 0:P2], 0.0)
        u["xak"] = jnp.where(u["m_strict"], g[0:P2, P2:2 * P2], 0.0).astype(BF16)
        u["xr"] = jnp.concatenate([jnp.where(u["m_incl"], g[P2:2 * P2, 0:P2], 0.0),
                                   jnp.where(u["m_incl"], g[P2:2 * P2, P2:2 * P2], 0.0)], axis=1).astype(BF16)
        u["tinv"] = jnp.where(eye, 1.0, u["xab"])
        u["pw"] = u["xab"].astype(BF16)
    for _ in range(L.bit_length() - 2):
        for u in units:
            u["pw"] = _dot(u["pw"], u["pw"]).astype(BF16)
        for u in units:
            u["tinv"] = u["tinv"] + _dot(u["tinv"].astype(BF16), u["pw"])
    for u in units:
        u["xv"] = _dot(u["xak"], u["vexp"].astype(BF16))
    for u in units:
        u["wu0"] = _dot(u["tinv"].astype(BF16), jnp.concatenate([u["ea"], u["xv"]], axis=1).astype(BF16))
        u["bkt"] = jnp.concatenate([expand(u["bh"]).T, expand(u["kh"]).T], axis=1).astype(BF16)
        u["p_col"] = jnp.broadcast_to(u["p_last"], (P2, LANES)).T
    for u in units:
        u["st"] = st_ref[u["d"], u["p"]]
        u["ws"] = _dot(jnp.concatenate([u["wu0"][:, 0:LANES], u["er"]], axis=0).astype(BF16), u["st"].astype(BF16))
    for u in units:
        uu = u["ws"][0:P2] + u["wu0"][:, LANES:2 * LANES]
        u["uv"] = jnp.concatenate([uu, u["vexp"]], axis=0).astype(BF16)
    for u in units:
        yexp = u["ws"][P2:2 * P2] + _dot(u["xr"], u["uv"])
        u["y_ref"][:, u["sl"]] = (yexp[0:L] + yexp[L:P2]).astype(u["y_ref"].dtype)
    for u in units:
        st_ref[u["d"], u["p"]] = u["p_col"] * u["st"] + _dot(u["bkt"], u["uv"])


def _wkv_scan(lw0, lw1, r, v, kk, ka0, ka1, kd0, kd1):
    b, s, c = r.shape
    L = SCAN_CHUNK
    nc = s // L
    fwd = lambda bi, i: (bi, i, 0)
    bwd = lambda bi, i: (bi, nc - 1 - i, 0)
    fs = pl.BlockSpec((None, L, c), fwd)
    bs = pl.BlockSpec((None, L, c), bwd)
    return pl.pallas_call(
        _wkv_scan_kernel,
        grid=(b, nc),
        in_specs=[fs] * 6 + [bs] * 6,
        out_specs=[fs, bs],
        out_shape=[jax.ShapeDtypeStruct((b, s, c), F32)] * 2,
        scratch_shapes=[pltpu.VMEM((2, c // LANES, LANES, LANES), F32)],
        compiler_params=_params(("parallel", "arbitrary")),
        name="wkv_scan",
    )(lw0, r, v, kk, ka0, kd0, lw1, r, v, kk, ka1, kd1)


def _post_kernel(x_ref, oa_ref, yf_ref, yb_ref, g_ref, bv_ref, zg_ref, lnw_ref, lnb_ref,
                 woa_ref, wor_ref, wout_ref, n2w_ref, wq_ref, x1_ref, xn_ref, qp_ref):
    y = yf_ref[...] + yb_ref[...]
    c = y.shape[1]
    e, et = _seg_mats(c, HEAD_DIM)
    mu = _seg_sum(y, e, et) * (1.0 / HEAD_DIM)
    dlt = y - mu
    var = _seg_sum(dlt * dlt, e, et) * (1.0 / HEAD_DIM)
    yn = dlt * lax.rsqrt(var + RWKV_LN_EPS) * lnw_ref[...] + lnb_ref[...]
    out = ((yn + bv_ref[...].astype(F32)) * g_ref[...].astype(F32)).astype(BF16)
    h_rwkv = _dot(out, wor_ref[...])
    h_attn = _dot(oa_ref[...], woa_ref[...])
    zg = zg_ref[...].astype(F32)
    merged = _sigmoid(zg[:, 0:c]) * h_attn + _sigmoid(zg[:, c:2 * c]) * h_rwkv
    x1 = x_ref[...] + _dot(merged.astype(BF16), wout_ref[...])
    x1_ref[...] = x1
    ms = jnp.mean(x1 * x1, axis=-1, keepdims=True)
    xn = (x1 * lax.rsqrt(ms + NORM_EPS) * n2w_ref[...]).astype(BF16)
    xn_ref[...] = xn
    qp = _dot(xn, wq_ref[...]).astype(BF16)
    for j in range(qp_ref.shape[0]):
        qp_ref[j] = qp[:, j * LANES:(j + 1) * LANES]


def _post(x, oa, yf, yb, g, bv, zg, lnw, lnb, woa, wor, wout, n2w, wq, tm):
    t, d = x.shape
    nq = wq.shape[1] // LANES
    tok = lambda i: (i, 0)
    const = lambda i: (0, 0)
    tspec = lambda w: pl.BlockSpec((tm, w), tok)
    cspec = lambda a: pl.BlockSpec(a.shape, const)
    return pl.pallas_call(
        _post_kernel,
        grid=(t // tm,),
        in_specs=[tspec(d), tspec(d), tspec(d), tspec(d), tspec(d), tspec(d), tspec(2 * d),
                  cspec(lnw), cspec(lnb), cspec(woa), cspec(wor), cspec(wout), cspec(n2w), cspec(wq)],
        out_specs=[tspec(d), tspec(d), pl.BlockSpec((nq, tm, LANES), lambda i: (0, i, 0))],
        out_shape=[jax.ShapeDtypeStruct((t, d), F32), jax.ShapeDtypeStruct((t, d), BF16),
                   jax.ShapeDtypeStruct((nq, t, LANES), BF16)],
        compiler_params=_params(("parallel",)),
        name="post_merge",
    )(x, oa, yf, yb, g, bv, zg, lnw, lnb, woa, wor, wout, n2w, wq)


def _top_values(cur, k, exact):
    rows = lax.broadcasted_iota(jnp.int32, cur.shape, 0).astype(F32)
    vals = []
    for _ in range(k):
        m = jnp.max(cur, axis=0, keepdims=True)
        vals.append(m)
        if exact:
            first = jnp.min(jnp.where(cur == m, rows, float(cur.shape[0])), axis=0, keepdims=True)
            cur = jnp.where(rows == first, -jnp.inf, cur)
        else:
            cur = jnp.where(cur == m, -jnp.inf, cur)
    removed = jnp.sum(jnp.where(cur == -jnp.inf, 1.0, 0.0), axis=0, keepdims=True)
    return vals, removed


def _peer_select(s0, s1, k, exact):
    a, ra = _top_values(s0, k, exact)
    b, rb = _top_values(s1, k, exact)
    amat = jnp.concatenate(a, axis=0)
    bmat = jnp.concatenate(b, axis=0)
    h8 = k // 2
    cand = jnp.concatenate([a[0] + bmat, a[1] + bmat[0:h8], amat[h8:k] + b[0]]
                           + [a[i] + bmat[0:h8] for i in range(2, h8)], axis=0)
    best, rc = _top_values(cand, k, exact)
    z = jnp.zeros_like(best[0])
    for val in best:
        z = z + jnp.exp(val - best[0])
    return bmat, a[0], best[k - 1], z, jnp.maximum(jnp.maximum(ra, rb), rc)


def _peer_kernel(xn_ref, qp_ref, sk_ref, u_ref, vt_ref, x1_ref, o_ref,
                 cnt_ref, e0_ref, rank_ref, e1_ref, sel_ref, acc_ref):
    jb = pl.program_id(1)
    tt = xn_ref.shape[0]
    K = PEER_TOPK
    rows_per_blk = u_ref.shape[0] // PEER_NKEYS

    @pl.when(jb == 0)
    def _():
        acc_ref[...] = jnp.zeros(acc_ref.shape, F32)

        def head_body(h, carry):
            s0 = _dot_nt(sk_ref[2 * h], qp_ref[2 * h])
            s1 = _dot_nt(sk_ref[2 * h + 1], qp_ref[2 * h + 1])

            def store(sel):
                bmat, a0, thr, z, removed = sel
                sel_ref[0:K] = bmat
                sel_ref[K:K + 8] = jnp.broadcast_to(a0, (8, tt))
                sel_ref[K + 8:K + 16] = jnp.broadcast_to(thr, (8, tt))
                sel_ref[K + 16:K + 24] = jnp.broadcast_to(z, (8, tt))
                return removed

            removed = store(_peer_select(s0, s1, K, exact=False))

            @pl.when(jnp.max(removed) > K)
            def _():
                store(_peer_select(s0, s1, K, exact=True))

            bmat = sel_ref[0:K]
            a0 = sel_ref[K:K + 1]
            thr = sel_ref[K + 8:K + 9]
            z = sel_ref[K + 16:K + 17]
            cnt = jnp.zeros(s0.shape, F32)
            rank = jnp.full(s1.shape, float(PEER_NKEYS), F32)
            for kk in range(K):
                cnt = cnt + jnp.where(s0 + bmat[kk:kk + 1] >= thr, 1.0, 0.0)
            for kk in reversed(range(K)):
                rank = jnp.where(s1 == bmat[kk:kk + 1], float(kk + 1), rank)
            cnt_ref[h] = cnt
            e0_ref[h] = jnp.exp(s0 - a0) / z
            rank_ref[h] = rank.astype(BF16)
            e1_ref[h] = jnp.exp(s1 - bmat[0:1]).astype(BF16)
            return carry

        lax.fori_loop(0, PEER_HEADS, head_body, 0)

    sub = 2 * PEER_NKEYS
    xn = xn_ref[...]
    n_sub = u_ref.shape[0] // sub
    hpres = [None] * n_sub
    hpres[0] = _dot_nt(u_ref[0:sub, :], xn)
    for c in range(n_sub):
        if c + 1 < n_sub:
            hpres[c + 1] = _dot_nt(u_ref[(c + 1) * sub:(c + 2) * sub, :], xn)
        g_rows = []
        for il in range(c * sub // PEER_NKEYS, (c + 1) * sub // PEER_NKEYS):
            i = jb * rows_per_blk + il
            acc = None
            for h in range(PEER_HEADS):
                crow = cnt_ref[h, pl.ds(i, 1), :].astype(BF16)
                e0row = e0_ref[h, pl.ds(i, 1), :].astype(BF16)
                contrib = jnp.where(rank_ref[h] <= crow, e0row * e1_ref[h], jnp.zeros((), BF16))
                acc = contrib if acc is None else acc + contrib
            g_rows.append(acc)
        gates = jnp.concatenate(g_rows, axis=0)
        hpre = hpres[c]
        gelu = 0.5 * hpre * (1.0 + lax.erf(hpre * (2.0 ** -0.5)))
        hg = gelu.astype(BF16) * gates
        acc_ref[...] += _dot(vt_ref[:, c * sub:(c + 1) * sub], hg)

    @pl.when(jb == pl.num_programs(1) - 1)
    def _():
        o_ref[...] = x1_ref[...] + acc_ref[...].T


def _peer(xn, qp, sk, u, vt, x1, tt, eb):
    t, d = xn.shape
    n_exp = u.shape[0]
    nq = qp.shape[0]
    return pl.pallas_call(
        _peer_kernel,
        grid=(t // tt, n_exp // eb),
        in_specs=[pl.BlockSpec((tt, d), lambda i, j: (i, 0)),
                  pl.BlockSpec((nq, tt, LANES), lambda i, j: (0, i, 0)),
                  pl.BlockSpec(sk.shape, lambda i, j: (0, 0, 0)),
                  pl.BlockSpec((eb, d), lambda i, j: (j, 0)),
                  pl.BlockSpec((d, eb), lambda i, j: (0, j)),
                  pl.BlockSpec((tt, d), lambda i, j: (i, 0))],
        out_specs=pl.BlockSpec((tt, d), lambda i, j: (i, 0)),
        out_shape=jax.ShapeDtypeStruct((t, d), F32),
        scratch_shapes=[pltpu.VMEM((PEER_HEADS, PEER_NKEYS, tt), F32),
                        pltpu.VMEM((PEER_HEADS, PEER_NKEYS, tt), F32),
                        pltpu.VMEM((PEER_HEADS, PEER_NKEYS, tt), BF16),
                        pltpu.VMEM((PEER_HEADS, PEER_NKEYS, tt), BF16),
                        pltpu.VMEM((PEER_TOPK + 24, tt), F32),
                        pltpu.VMEM((d, tt), F32)],
        compiler_params=_params(("parallel", "arbitrary")),
        name="peer",
    )(xn, qp, sk, u, vt, x1)


def _rope_tables(s):
    inv = 1.0 / (ROPE_THETA ** (jnp.arange(0, HEAD_DIM, 2, dtype=F32) / HEAD_DIM))
    ang = jnp.arange(s, dtype=F32)[:, None] * inv[None, :]
    cos, sin = jnp.cos(ang), jnp.sin(ang)
    reps = LANES // HEAD_DIM
    cos_t = jnp.tile(jnp.concatenate([cos, cos], axis=1), (1, reps))
    sin_t = jnp.tile(jnp.concatenate([-sin, sin], axis=1), (1, reps))
    return cos_t, sin_t


def _tile(n, pref):
    return pref if n % pref == 0 else n


def _trunk(x, P):
    b, s, d = x.shape
    t = b * s
    c = d
    xf = x.reshape(t, d)
    tm = _tile(t, 512)
    za = _norm_matmul(xf, P["norm1_w"], P["w_a"], tm).reshape(b, s, -1)
    zr = _norm_matmul(xf, P["norm1_w"], P["w_r"], tm).reshape(b, s, -1)
    zg = _norm_matmul(xf, P["norm1_w"], P["w_g"], tm)

    cos_t, sin_t = _rope_tables(s)
    ts = _tile(s, 512)
    q = _attn_prep(za, cos_t, sin_t, P["q_norm_w"], 0, HEAD_DIM ** -0.5 * math.log2(math.e), ts, False)
    kt = _attn_prep(za, cos_t, sin_t, P["k_norm_w"], 1, 1.0, ts, True)
    oa = _attention(P["lam_rows"], q, kt, za, P["subln_w"], _tile(s, 1024), _tile(s, 1024))

    tp = _tile(s, 256)
    (r, v, kk, kd0, kd1, ka0, ka1, lw0, lw1, g, bv) = _rwkv_prep(
        zr, P["shift_mu"], P["decay_w0"], P["decay_w2"], P["iclr_a0"], P["iclr_a2"], P["gate_g2"],
        P["k_k"], P["k_a"], P["r_k"], c, tp)
    yf, yb = _wkv_scan(lw0, lw1, r, v, kk, ka0, ka1, kd0, kd1)

    flat = lambda a: a.reshape(t, a.shape[-1])
    x1, xn, qp = _post(xf, flat(oa), flat(yf), flat(yb), flat(g), flat(bv), zg, P["lnx_w"], P["lnx_b"],
                       P["w_o_attn"], P["w_o_rwkv"], P["w_out"], P["norm2_w"], P["peer_wq"], _tile(t, 256))
    y = _peer(xn, qp, P["peer_subkeys"], P["peer_u"], P["peer_vt"], x1, _tile(t, 512), 1024)
    return y.reshape(b, s, d)


def kernel(x_prompt, x_sample, norm1_w, w_in, q_norm_w, k_norm_w, lambda_q1, lambda_k1, lambda_q2, lambda_k2, subln_w, w_o_attn, shift_mu, decay_w0, decay_w2, iclr_a0, iclr_a2, gate_g2, k_k, k_a, r_k, lnx_w, lnx_b, w_o_rwkv, w_out, norm2_w, peer_wq, peer_subkeys, peer_u, peer_v):
    d = x_prompt.shape[-1]
    c = d
    att_cols = 3 * d
    rwkv_cols = shift_mu.shape[-1]
    l = 0
    w = w_in[l].astype(BF16)
    nsub = d // HEAD_DIM
    P = {
        "norm1_w": norm1_w[l][None, :],
        "w_a": w[:, :att_cols],
        "w_r": w[:, att_cols:att_cols + rwkv_cols],
        "w_g": w[:, att_cols + rwkv_cols:],
        "q_norm_w": jnp.tile(q_norm_w[l], nsub)[None, :],
        "k_norm_w": jnp.tile(k_norm_w[l], nsub)[None, :],
        "lam_rows": jnp.stack([lambda_q1[l], lambda_k1[l], lambda_q2[l], lambda_k2[l]]),
        "subln_w": subln_w[l][None, :],
        "w_o_attn": w_o_attn[l].astype(BF16),
        "shift_mu": shift_mu[l][None, :],
        "decay_w0": decay_w0[l],
        "decay_w2": decay_w2[l].reshape(-1, c).astype(BF16),
        "iclr_a0": iclr_a0[l],
        "iclr_a2": iclr_a2[l].reshape(-1, c).astype(BF16),
        "gate_g2": gate_g2[l].astype(BF16),
        "k_k": k_k[l][None, :],
        "k_a": k_a[l][None, :],
        "r_k": r_k[l].reshape(1, c),
        "lnx_w": lnx_w[l][None, :],
        "lnx_b": lnx_b[l][None, :],
        "w_o_rwkv": w_o_rwkv[l].astype(BF16),
        "w_out": w_out[l].astype(BF16),
        "norm2_w": norm2_w[l][None, :],
        "peer_wq": peer_wq[l].astype(BF16),
        "peer_subkeys": peer_subkeys[l].reshape(-1, PEER_NKEYS, peer_subkeys.shape[-1]).astype(BF16),
        "peer_u": peer_u[l].astype(BF16),
        "peer_vt": peer_v[l].astype(BF16).T,
    }
    return (_trunk(x_prompt, P), _trunk(x_sample, P))
```

```python
import functools
import math

import jax
import jax.numpy as jnp
from jax import lax
from jax.experimental import pallas as pl
from jax.experimental.pallas import tpu as pltpu

F32 = jnp.float32
BF16 = jnp.bfloat16

LANES = 128
HEAD_DIM = 64
ATT_HEADS = 8
NORM_EPS = 1e-6
RWKV_LN_EPS = 64e-5
ROPE_THETA = 10000.0
LAMBDA_INIT = 0.8 - 0.6 * math.exp(-0.3 * 0)
SCAN_CHUNK = 64
PEER_NKEYS = 128
PEER_TOPK = 16
PEER_HEADS = 8
VMEM_LIMIT = 56 * 1024 * 1024


def _params(sem):
    return pltpu.CompilerParams(dimension_semantics=sem, vmem_limit_bytes=VMEM_LIMIT)


def _dot(a, b):
    return jnp.dot(a, b, preferred_element_type=F32)


def _dot_nt(a, b):
    return lax.dot_general(a, b, (((1,), (1,)), ((), ())), preferred_element_type=F32)


def _split_dot(x, m):
    hi = x.astype(BF16)
    lo = (x - hi.astype(F32)).astype(BF16)
    return _dot(hi, m) + _dot(lo, m)


def _seg_mats(width, seg):
    lane = lax.broadcasted_iota(jnp.int32, (width, LANES), 0)
    grp = lax.broadcasted_iota(jnp.int32, (width, LANES), 1)
    e = jnp.where(lane // seg == grp, 1.0, 0.0).astype(BF16)
    grp_t = lax.broadcasted_iota(jnp.int32, (LANES, width), 0)
    lane_t = lax.broadcasted_iota(jnp.int32, (LANES, width), 1)
    et = jnp.where(lane_t // seg == grp_t, 1.0, 0.0).astype(BF16)
    return e, et


def _seg_sum(x, e, et):
    return _split_dot(_split_dot(x, e), et)


def _sigmoid(x):
    return 1.0 / (1.0 + jnp.exp(-x))


def _norm_matmul_kernel(x_ref, nw_ref, w_ref, o_ref):
    x = x_ref[...]
    ms = jnp.mean(x * x, axis=-1, keepdims=True)
    h = (x * lax.rsqrt(ms + NORM_EPS) * nw_ref[...]).astype(BF16)
    o_ref[...] = _dot(h, w_ref[...]).astype(o_ref.dtype)


def _norm_matmul(x, nw, w, tm):
    t, d = x.shape
    n = w.shape[1]
    return pl.pallas_call(
        _norm_matmul_kernel,
        grid=(t // tm,),
        in_specs=[pl.BlockSpec((tm, d), lambda i: (i, 0)),
                  pl.BlockSpec((1, d), lambda i: (0, 0)),
                  pl.BlockSpec((d, n), lambda i: (0, 0))],
        out_specs=pl.BlockSpec((tm, n), lambda i: (i, 0)),
        out_shape=jax.ShapeDtypeStruct((t, n), BF16),
        compiler_params=_params(("parallel",)),
        name="norm_matmul",
    )(x, nw, w)


def _norm_rope(x, w, c, s, e, et, scale):
    width = x.shape[1]
    ms = _seg_sum(x * x, e, et) * (1.0 / HEAD_DIM)
    y = x * lax.rsqrt(ms + NORM_EPS) * w
    lane = lax.broadcasted_iota(jnp.int32, y.shape, 1)
    half = HEAD_DIM // 2
    partner = jnp.where(lane % HEAD_DIM < half, pltpu.roll(y, width - half, 1), pltpu.roll(y, half, 1))
    return (y * c + partner * s) * scale


def _inproj_attn_kernel(x_ref, nw_ref, w_ref, cos_ref, sin_ref, qw_ref, kw_ref, q_ref, kt_ref, v_ref, *, q_scale):
    x = x_ref[...]
    ms = jnp.mean(x * x, axis=-1, keepdims=True)
    h = (x * lax.rsqrt(ms + NORM_EPS) * nw_ref[...]).astype(BF16)
    z = _dot(h, w_ref[...])
    width = q_ref.shape[1]
    reps = width // LANES
    e, et = _seg_mats(width, HEAD_DIM)
    c = jnp.concatenate([cos_ref[...]] * reps, axis=1)
    s = jnp.concatenate([sin_ref[...]] * reps, axis=1)
    q_ref[...] = _norm_rope(z[:, 0:width], qw_ref[...], c, s, e, et, q_scale).astype(q_ref.dtype)
    k = _norm_rope(z[:, width:2 * width], kw_ref[...], c, s, e, et, 1.0)
    for j in range(reps):
        kt_ref[j * LANES:(j + 1) * LANES, :] = k[:, j * LANES:(j + 1) * LANES].T.astype(kt_ref.dtype)
    v_ref[...] = z[:, 2 * width:3 * width].astype(v_ref.dtype)


def _inproj_attn(x, nw, w_a, cos_t, sin_t, qw, kw, q_scale, tm):
    b, s, d = x.shape
    width = qw.shape[1]
    tok = lambda bi, i: (bi, i, 0)
    const = lambda bi, i: (0, 0)
    return pl.pallas_call(
        functools.partial(_inproj_attn_kernel, q_scale=q_scale),
        grid=(b, s // tm),
        in_specs=[pl.BlockSpec((None, tm, d), tok),
                  pl.BlockSpec((1, d), const),
                  pl.BlockSpec(w_a.shape, const),
                  pl.BlockSpec((tm, LANES), lambda bi, i: (i, 0)),
                  pl.BlockSpec((tm, LANES), lambda bi, i: (i, 0)),
                  pl.BlockSpec((1, width), const),
                  pl.BlockSpec((1, width), const)],
        out_specs=[pl.BlockSpec((None, tm, width), tok),
                   pl.BlockSpec((None, width, tm), lambda bi, i: (bi, 0, i)),
                   pl.BlockSpec((None, tm, width), tok)],
        out_shape=[jax.ShapeDtypeStruct((b, s, width), BF16),
                   jax.ShapeDtypeStruct((b, width, s), BF16),
                   jax.ShapeDtypeStruct((b, s, width), BF16)],
        compiler_params=_params(("parallel", "parallel")),
        name="inproj_attn",
    )(x, nw, w_a, cos_t, sin_t, qw, kw)


ATTN_ROW_CHUNK = 256


def _attn_kernel(lam_ref, q_ref, kt_ref, v_ref, sw_ref, o_ref, qs_ref, m_ref, acc_ref, s0_ref, *, tq, tk):
    q = q_ref[...]
    lane = lax.broadcasted_iota(jnp.int32, q.shape, 1)
    zero = jnp.zeros_like(q)
    qs_ref[0:tq, :] = jnp.where(lane < HEAD_DIM, q, zero)
    qs_ref[tq:2 * tq, :] = jnp.where(lane < HEAD_DIM, zero, q)
    m_ref[...] = jnp.full(m_ref.shape, -jnp.inf, F32)
    acc_ref[...] = jnp.zeros(acc_ref.shape, F32)

    rc = min(ATTN_ROW_CHUNK, 2 * tq)
    n_chunks = 2 * tq // rc
    n_tiles = kt_ref.shape[1] // tk
    s0_ref[...] = _dot(qs_ref[0:rc, :], kt_ref[:, 0:tk])

    def key_tile(j, carry):
        off = pl.multiple_of(j * tk, tk)
        v = v_ref[pl.ds(off, tk), :]
        v_ones = jnp.concatenate([v, jnp.ones_like(v)], axis=1)
        kt = kt_ref[:, pl.ds(off, tk)]
        scores = [None] * n_chunks
        scores[0] = s0_ref[...]
        for c in range(n_chunks):
            rows = slice(c * rc, (c + 1) * rc)
            if c + 1 < n_chunks:
                scores[c + 1] = _dot(qs_ref[(c + 1) * rc:(c + 2) * rc, :], kt)
            else:
                off_next = pl.multiple_of(jnp.minimum(j + 1, n_tiles - 1) * tk, tk)
                s0_ref[...] = _dot(qs_ref[0:rc, :], kt_ref[:, pl.ds(off_next, tk)])
            s = scores[c]
            m_prev = m_ref[rows, :]
            m_new = jnp.maximum(m_prev, jnp.max(s, axis=1, keepdims=True))
            alpha = jnp.exp2(m_prev - m_new)
            p = jnp.exp2((s - jnp.concatenate([m_new] * (tk // LANES), axis=1)).astype(BF16))
            acc_ref[rows, :] = (jnp.concatenate([alpha, alpha], axis=1) * acc_ref[rows, :]
                                + _dot(p, v_ones))
            m_ref[rows, :] = m_new
        return carry

    lax.fori_loop(0, n_tiles, key_tile, 0)

    lv = lam_ref[...]
    lam = (jnp.exp(jnp.sum(lv[0:1] * lv[1:2], axis=1, keepdims=True))
           - jnp.exp(jnp.sum(lv[2:3] * lv[3:4], axis=1, keepdims=True)) + LAMBDA_INIT)
    acc = acc_ref[...]
    o = (acc[0:tq, 0:LANES] / acc[0:tq, LANES:2 * LANES]
         - lam * (acc[tq:2 * tq, 0:LANES] / acc[tq:2 * tq, LANES:2 * LANES]))
    ms = jnp.mean(o * o, axis=-1, keepdims=True)
    o = o * lax.rsqrt(ms + NORM_EPS) * sw_ref[...] * (1.0 - LAMBDA_INIT)
    o_ref[...] = o.astype(o_ref.dtype)


def _attention(lam_rows, q, kt, v, subln_w, tq, tk):
    b, s, width = q.shape
    heads = width // LANES
    return pl.pallas_call(
        functools.partial(_attn_kernel, tq=tq, tk=tk),
        grid=(b, heads, s // tq),
        in_specs=[pl.BlockSpec((4, HEAD_DIM), lambda bi, h, qi: (0, 0)),
                  pl.BlockSpec((None, tq, LANES), lambda bi, h, qi: (bi, qi, h)),
                  pl.BlockSpec((None, LANES, s), lambda bi, h, qi: (bi, h, 0)),
                  pl.BlockSpec((None, s, LANES), lambda bi, h, qi: (bi, 0, h)),
                  pl.BlockSpec((1, LANES), lambda bi, h, qi: (0, 0))],
        out_specs=pl.BlockSpec((None, tq, LANES), lambda bi, h, qi: (bi, qi, h)),
        out_shape=jax.ShapeDtypeStruct((b, s, width), BF16),
        scratch_shapes=[pltpu.VMEM((2 * tq, LANES), BF16),
                        pltpu.VMEM((2 * tq, LANES), F32),
                        pltpu.VMEM((2 * tq, 2 * LANES), F32),
                        pltpu.VMEM((min(ATTN_ROW_CHUNK, 2 * tq), tk), F32)],
        compiler_params=_params(("parallel", "parallel", "arbitrary")),
        name="diff_attention",
    )(lam_rows, q, kt, v, subln_w)


HALO = 16


def _rwkv_prep_kernel(x_ref, xp_ref, xn_ref, nw_ref, w_ref, mu_ref, w0_ref, w2_ref, a0_ref, a2_ref, g2_ref,
                      kk_w_ref, ka_w_ref, rk_ref,
                      r_ref, v_ref, kk_ref, kd0_ref, kd1_ref, ka0_ref, ka1_ref, lw0_ref, lw1_ref,
                      g_ref, bv_ref, *, c):
    i = pl.program_id(1)

    def project(x):
        ms = jnp.mean(x * x, axis=-1, keepdims=True)
        return _dot((x * lax.rsqrt(ms + NORM_EPS) * nw_ref[...]).astype(BF16), w_ref[...])

    z = project(x_ref[...])
    tm = z.shape[0]
    row = lax.broadcasted_iota(jnp.int32, z.shape, 0)
    prev_row = jnp.where(i == 0, 0.0, project(xp_ref[...])[HALO - 1:HALO, :])
    next_row = jnp.where(i == pl.num_programs(1) - 1, 0.0, project(xn_ref[...])[0:1, :])
    z_up = jnp.where(row == 0, prev_row, pltpu.roll(z, 1, 0))
    z_dn = jnp.where(row == tm - 1, next_row, pltpu.roll(z, tm - 1, 0))
    zs = z + mu_ref[...] * (0.5 * (z_up + z_dn) - z)

    r = zs[:, 0:c]
    k = zs[:, c:2 * c]
    v = zs[:, 2 * c:3 * c]
    wl = zs[:, 3 * c:3 * c + LANES]
    al = zs[:, 3 * c + LANES:3 * c + 2 * LANES]
    gl = zs[:, 3 * c + 2 * LANES:3 * c + 3 * LANES]

    lane = lax.broadcasted_iota(jnp.int32, wl.shape, 1)
    twl = jnp.tanh(wl)
    e, et = _seg_mats(c, HEAD_DIM)

    kk = k * kk_w_ref[...]
    kk = kk * lax.rsqrt(_seg_sum(kk * kk, e, et) + 1e-12)
    kk_ref[...] = kk.astype(kk_ref.dtype)
    r_ref[...] = r.astype(r_ref.dtype)
    v_ref[...] = v.astype(v_ref.dtype)
    g_ref[...] = _dot(_sigmoid(gl).astype(BF16), g2_ref[...]).astype(g_ref.dtype)

    ksum = jnp.zeros_like(k)
    for d, (kd_ref, ka_ref, lw_ref) in enumerate(((kd0_ref, ka0_ref, lw0_ref), (kd1_ref, ka1_ref, lw1_ref))):
        sel = (lane < HEAD_DIM) if d == 0 else (lane >= HEAD_DIM)
        wpre = _dot(jnp.where(sel, twl, 0.0).astype(BF16), w2_ref[...]) + w0_ref[d:d + 1, :]
        lw_ref[...] = (-math.exp(-0.5)) * _sigmoid(wpre)
        a = _sigmoid(_dot(jnp.where(sel, al, 0.0).astype(BF16), a2_ref[...]) + a0_ref[d:d + 1, :])
        kd = k * (1.0 + (a - 1.0) * ka_w_ref[...])
        kd_ref[...] = kd.astype(kd_ref.dtype)
        ka_ref[...] = (kk * a).astype(ka_ref.dtype)
        ksum = ksum + kd
    bonus = _seg_sum(r * ksum * rk_ref[...], e, et)
    bv_ref[...] = (bonus * v).astype(bv_ref.dtype)


def _rwkv_prep(x, nw, w_r, mu, w0, w2, a0, a2, g2, kk_w, ka_w, rk, c, tm):
    b, s, d = x.shape
    cols = w_r.shape[1]
    nh = tm // HALO
    last = s // HALO - 1
    tok = lambda bi, i: (bi, i, 0)
    const = lambda bi, i: (0, 0)
    out = lambda dt: jax.ShapeDtypeStruct((b, s, c), dt)
    ospec = pl.BlockSpec((None, tm, c), tok)
    return pl.pallas_call(
        functools.partial(_rwkv_prep_kernel, c=c),
        grid=(b, s // tm),
        in_specs=[pl.BlockSpec((None, tm, d), tok),
                  pl.BlockSpec((None, HALO, d), lambda bi, i: (bi, jnp.maximum(i * nh - 1, 0), 0)),
                  pl.BlockSpec((None, HALO, d), lambda bi, i: (bi, jnp.minimum((i + 1) * nh, last), 0)),
                  pl.BlockSpec((1, d), const),
                  pl.BlockSpec((d, cols), const),
                  pl.BlockSpec((1, cols), const),
                  pl.BlockSpec((2, c), const),
                  pl.BlockSpec((LANES, c), const),
                  pl.BlockSpec((2, c), const),
                  pl.BlockSpec((LANES, c), const),
                  pl.BlockSpec((LANES, c), const),
                  pl.BlockSpec((1, c), const),
                  pl.BlockSpec((1, c), const),
                  pl.BlockSpec((1, c), const)],
        out_specs=[ospec] * 11,
        out_shape=[out(BF16)] * 7 + [out(F32)] * 2 + [out(BF16)] * 2,
        compiler_params=_params(("parallel", "parallel")),
        name="rwkv_prep",
    )(x, x, x, nw, w_r, mu, w0, w2, a0, a2, g2, kk_w, ka_w, rk)


def _wkv_scan_kernel(lwf_ref, rf_ref, vf_ref, kkf_ref, kaf_ref, kdf_ref,
                     lwb_ref, rb_ref, vb_ref, kkb_ref, kab_ref, kdb_ref,
                     yf_ref, yb_ref, st_ref):
    L = SCAN_CHUNK
    P2 = 2 * L
    ci = pl.program_id(1)

    @pl.when(ci == 0)
    def _():
        st_ref[...] = jnp.zeros(st_ref.shape, F32)

    row = lax.broadcasted_iota(jnp.int32, (P2, P2), 0)
    col = lax.broadcasted_iota(jnp.int32, (P2, P2), 1)
    same_head = (row // L) == (col // L)
    rt = row % L
    ct = col % L
    eye = row == col
    ti = lax.broadcasted_iota(jnp.int32, (L, L), 0)
    tj = lax.broadcasted_iota(jnp.int32, (L, L), 1)
    lane_a = lax.broadcasted_iota(jnp.int32, (L, LANES), 1) < HEAD_DIM

    def expand(x):
        return jnp.concatenate([jnp.where(lane_a, x, 0.0), jnp.where(lane_a, 0.0, x)], axis=0)

    units = []
    dirs = ((lwf_ref, rf_ref, vf_ref, kkf_ref, kaf_ref, kdf_ref, yf_ref),
            (lwb_ref, rb_ref, vb_ref, kkb_ref, kab_ref, kdb_ref, yb_ref))
    for d, (lw_ref, r_ref, v_ref, kk_ref, ka_ref, kd_ref, y_ref) in enumerate(dirs):
        if d == 0:
            m_strict = same_head & (rt > ct)
            m_incl = same_head & (rt >= ct)
            tri = jnp.where(tj <= ti, 1.0, 0.0).astype(BF16)
        else:
            m_strict = same_head & (rt < ct)
            m_incl = same_head & (rt <= ct)
            tri = jnp.where(tj >= ti, 1.0, 0.0).astype(BF16)
        lw = lw_ref[...]
        hi = lw.astype(BF16)
        r1 = lw - hi.astype(F32)
        mid = r1.astype(BF16)
        lo = (r1 - mid.astype(F32)).astype(BF16)
        cum = _dot(tri, hi) + _dot(tri, mid) + _dot(tri, lo)
        cum_last = cum[L - 1:L, :] if d == 0 else cum[0:1, :]
        ec = jnp.exp(cum)
        enc = jnp.exp(-cum)
        p_last = jnp.exp(cum_last)
        a_t = -kk_ref[...].astype(F32) * jnp.exp(cum - lw)
        r_t = r_ref[...].astype(F32) * ec
        b_t = ka_ref[...].astype(F32) * enc
        k_t = kd_ref[...].astype(F32) * enc
        b_h = b_t * p_last
        k_h = k_t * p_last
        vv = v_ref[...].astype(F32)

        for p in range(lw.shape[1] // LANES):
            sl = slice(p * LANES, (p + 1) * LANES)
            units.append(dict(
                d=d, p=p, sl=sl, y_ref=y_ref, m_strict=m_strict, m_incl=m_incl,
                ea=expand(a_t[:, sl]), er=expand(r_t[:, sl]), vexp=expand(vv[:, sl]),
                bk2=jnp.concatenate([b_t[:, sl]] * 2 + [k_t[:, sl]] * 2, axis=0).astype(BF16),
                bh=b_h[:, sl], kh=k_h[:, sl], p_last=p_last[:, sl]))

    for u in units:
        g = _dot_nt(jnp.concatenate([u["ea"], u["er"]], axis=0).astype(BF16), u["bk2"])
        u["xab"] = jnp.where(u["m_strict"], g[0:P2, 0:P2], 0.0)
        u["xak"] = jnp.where(u["m_strict"], g[0:P2, P2:2 * P2], 0.0).astype(BF16)
        u["xr"] = jnp.concatenate([jnp.where(u["m_incl"], g[P2:2 * P2, 0:P2], 0.0),
                                   jnp.where(u["m_incl"], g[P2:2 * P2, P2:2 * P2], 0.0)], axis=1).astype(BF16)
        u["tinv"] = jnp.where(eye, 1.0, u["xab"])
        u["pw"] = u["xab"].astype(BF16)
    for _ in range(L.bit_length() - 2):
        for u in units:
            u["pw"] = _dot(u["pw"], u["pw"]).astype(BF16)
        for u in units:
            u["tinv"] = u["tinv"] + _dot(u["tinv"].astype(BF16), u["pw"])
    for u in units:
        u["xv"] = _dot(u["xak"], u["vexp"].astype(BF16))
    for u in units:
        u["wu0"] = _dot(u["tinv"].astype(BF16), jnp.concatenate([u["ea"], u["xv"]], axis=1).astype(BF16))
        u["bkt"] = jnp.concatenate([expand(u["bh"]).T, expand(u["kh"]).T], axis=1).astype(BF16)
        u["p_col"] = jnp.broadcast_to(u["p_last"], (P2, LANES)).T
    for u in units:
        u["st"] = st_ref[u["d"], u["p"]]
        u["ws"] = _dot(jnp.concatenate([u["wu0"][:, 0:LANES], u["er"]], axis=0).astype(BF16), u["st"].astype(BF16))
    for u in units:
        uu = u["ws"][0:P2] + u["wu0"][:, LANES:2 * LANES]
        u["uv"] = jnp.concatenate([uu, u["vexp"]], axis=0).astype(BF16)
    for u in units:
        yexp = u["ws"][P2:2 * P2] + _dot(u["xr"], u["uv"])
        u["y_ref"][:, u["sl"]] = (yexp[0:L] + yexp[L:P2]).astype(u["y_ref"].dtype)
    for u in units:
        st_ref[u["d"], u["p"]] = u["p_col"] * u["st"] + _dot(u["bkt"], u["uv"])


def _wkv_scan(lw0, lw1, r, v, kk, ka0, ka1, kd0, kd1):
    b, s, c = r.shape
    L = SCAN_CHUNK
    nc = s // L
    fwd = lambda bi, i: (bi, i, 0)
    bwd = lambda bi, i: (bi, nc - 1 - i, 0)
    fs = pl.BlockSpec((None, L, c), fwd)
    bs = pl.BlockSpec((None, L, c), bwd)
    return pl.pallas_call(
        _wkv_scan_kernel,
        grid=(b, nc),
        in_specs=[fs] * 6 + [bs] * 6,
        out_specs=[fs, bs],
        out_shape=[jax.ShapeDtypeStruct((b, s, c), F32)] * 2,
        scratch_shapes=[pltpu.VMEM((2, c // LANES, LANES, LANES), F32)],
        compiler_params=_params(("parallel", "arbitrary")),
        name="wkv_scan",
    )(lw0, r, v, kk, ka0, kd0, lw1, r, v, kk, ka1, kd1)


def _post_kernel(x_ref, oa_ref, yf_ref, yb_ref, g_ref, bv_ref, zg_ref, lnw_ref, lnb_ref,
                 woa_ref, wor_ref, wout_ref, n2w_ref, wq_ref, x1_ref, xn_ref, qp_ref):
    y = yf_ref[...] + yb_ref[...]
    c = y.shape[1]
    e, et = _seg_mats(c, HEAD_DIM)
    mu = _seg_sum(y, e, et) * (1.0 / HEAD_DIM)
    dlt = y - mu
    var = _seg_sum(dlt * dlt, e, et) * (1.0 / HEAD_DIM)
    yn = dlt * lax.rsqrt(var + RWKV_LN_EPS) * lnw_ref[...] + lnb_ref[...]
    out = ((yn + bv_ref[...].astype(F32)) * g_ref[...].astype(F32)).astype(BF16)
    h_rwkv = _dot(out, wor_ref[...])
    h_attn = _dot(oa_ref[...], woa_ref[...])
    zg = zg_ref[...].astype(F32)
    merged = _sigmoid(zg[:, 0:c]) * h_attn + _sigmoid(zg[:, c:2 * c]) * h_rwkv
    x1 = x_ref[...] + _dot(merged.astype(BF16), wout_ref[...])
    x1_ref[...] = x1
    ms = jnp.mean(x1 * x1, axis=-1, keepdims=True)
    xn = (x1 * lax.rsqrt(ms + NORM_EPS) * n2w_ref[...]).astype(BF16)
    xn_ref[...] = xn
    qp = _dot(xn, wq_ref[...]).astype(BF16)
    for j in range(qp_ref.shape[0]):
        qp_ref[j] = qp[:, j * LANES:(j + 1) * LANES]


def _post(x, oa, yf, yb, g, bv, zg, lnw, lnb, woa, wor, wout, n2w, wq, tm):
    t, d = x.shape
    nq = wq.shape[1] // LANES
    tok = lambda i: (i, 0)
    const = lambda i: (0, 0)
    tspec = lambda w: pl.BlockSpec((tm, w), tok)
    cspec = lambda a: pl.BlockSpec(a.shape, const)
    return pl.pallas_call(
        _post_kernel,
        grid=(t // tm,),
        in_specs=[tspec(d), tspec(d), tspec(d), tspec(d), tspec(d), tspec(d), tspec(2 * d),
                  cspec(lnw), cspec(lnb), cspec(woa), cspec(wor), cspec(wout), cspec(n2w), cspec(wq)],
        out_specs=[tspec(d), tspec(d), pl.BlockSpec((nq, tm, LANES), lambda i: (0, i, 0))],
        out_shape=[jax.ShapeDtypeStruct((t, d), F32), jax.ShapeDtypeStruct((t, d), BF16),
                   jax.ShapeDtypeStruct((nq, t, LANES), BF16)],
        compiler_params=_params(("parallel",)),
        name="post_merge",
    )(x, oa, yf, yb, g, bv, zg, lnw, lnb, woa, wor, wout, n2w, wq)


SUBLANES = 8


def _sort_pairs(n):
    def merge(lo, hi, r):
        step = r * 2
        if step < hi - lo:
            yield from merge(lo, hi, step)
            yield from merge(lo + r, hi, step)
            for i in range(lo + r, hi - r, step):
                yield (i, i + r)
        else:
            yield (lo, lo + r)

    def sort(lo, hi):
        if hi - lo >= 1:
            mid = lo + (hi - lo) // 2
            yield from sort(lo, mid)
            yield from sort(mid + 1, hi)
            yield from merge(lo, hi, 1)

    return list(sort(0, n - 1))


def _exchange(xs, i, j):
    a, b = xs[i], xs[j]
    if b is None:
        return
    if a is None:
        xs[i], xs[j] = b, None
        return
    xs[i], xs[j] = jnp.maximum(a, b), jnp.minimum(a, b)


def _top_sorted(blocks, k):
    xs = list(blocks)
    for i, j in _sort_pairs(k):
        _exchange(xs, i, j)
    shift = SUBLANES // 2
    while shift >= 1:
        ys = [None if x is None else pltpu.roll(x, shift, 0) for x in xs]
        zs = []
        for i in range(k):
            a, b = xs[i], ys[k - 1 - i]
            zs.append(b if a is None else (a if b is None else jnp.maximum(a, b)))
        step = k // 2
        while step >= 1:
            for i in range(k):
                if i & step == 0:
                    _exchange(zs, i, i + step)
            step //= 2
        xs = zs
        shift //= 2
    return [x[0:1, :] for x in xs]


def _row_blocks(x):
    return [x[r:r + SUBLANES, :] for r in range(0, x.shape[0], SUBLANES)]


def _peer_select(s0, s1, k):
    a = _top_sorted(_row_blocks(s0), k)
    b = _top_sorted(_row_blocks(s1), k)
    amat = jnp.concatenate(a, axis=0)
    bmat = jnp.concatenate(b, axis=0)
    h8 = k // 2
    cand = ([a[0] + bmat[0:h8], a[0] + bmat[h8:k], a[1] + bmat[0:h8], amat[h8:k] + b[0]]
            + [a[i] + bmat[0:h8] for i in range(2, h8)])
    best = _top_sorted(cand + [None] * (k - len(cand)), k)
    z = jnp.zeros_like(best[0])
    for val in best:
        z = z + jnp.exp(val - best[0])
    return bmat, a[0], best[k - 1], z


def _peer_kernel(xn_ref, qp_ref, sk_ref, u_ref, vt_ref, x1_ref, o_ref,
                 cnt_ref, e0_ref, rank_ref, e1_ref, acc_ref):
    jb = pl.program_id(1)
    tt = xn_ref.shape[0]
    K = PEER_TOPK
    rows_per_blk = u_ref.shape[0] // PEER_NKEYS

    @pl.when(jb == 0)
    def _():
        acc_ref[...] = jnp.zeros(acc_ref.shape, F32)

        def head_body(h, carry):
            s0 = _dot_nt(sk_ref[2 * h], qp_ref[2 * h])
            s1 = _dot_nt(sk_ref[2 * h + 1], qp_ref[2 * h + 1])

            bmat, a0, thr, z = _peer_select(s0, s1, K)
            cnt = jnp.zeros(s0.shape, F32)
            rank = jnp.full(s1.shape, float(PEER_NKEYS), F32)
            for kk in range(K):
                cnt = cnt + jnp.where(s0 + bmat[kk:kk + 1] >= thr, 1.0, 0.0)
            for kk in reversed(range(K)):
                rank = jnp.where(s1 == bmat[kk:kk + 1], float(kk + 1), rank)
            cnt_ref[h] = cnt
            e0_ref[h] = jnp.exp(s0 - a0) / z
            rank_ref[h] = rank.astype(BF16)
            e1_ref[h] = jnp.exp(s1 - bmat[0:1]).astype(BF16)
            return carry

        lax.fori_loop(0, PEER_HEADS, head_body, 0)

    sub = 2 * PEER_NKEYS
    xn = xn_ref[...]
    n_sub = u_ref.shape[0] // sub
    hpres = [None] * n_sub
    hpres[0] = _dot_nt(u_ref[0:sub, :], xn)
    for c in range(n_sub):
        if c + 1 < n_sub:
            hpres[c + 1] = _dot_nt(u_ref[(c + 1) * sub:(c + 2) * sub, :], xn)
        g_rows = []
        for il in range(c * sub // PEER_NKEYS, (c + 1) * sub // PEER_NKEYS):
            i = jb * rows_per_blk + il
            acc = None
            for h in range(PEER_HEADS):
                crow = cnt_ref[h, pl.ds(i, 1), :].astype(BF16)
                e0row = e0_ref[h, pl.ds(i, 1), :].astype(BF16)
                contrib = jnp.where(rank_ref[h] <= crow, e0row * e1_ref[h], jnp.zeros((), BF16))
                acc = contrib if acc is None else acc + contrib
            g_rows.append(acc)
        gates = jnp.concatenate(g_rows, axis=0)
        hpre = hpres[c]
        gelu = 0.5 * hpre * (1.0 + lax.erf(hpre * (2.0 ** -0.5)))
        hg = gelu.astype(BF16) * gates
        acc_ref[...] += _dot(vt_ref[:, c * sub:(c + 1) * sub], hg)

    @pl.when(jb == pl.num_programs(1) - 1)
    def _():
        o_ref[...] = x1_ref[...] + acc_ref[...].T


def _peer(xn, qp, sk, u, vt, x1, tt, eb):
    t, d = xn.shape
    n_exp = u.shape[0]
    nq = qp.shape[0]
    return pl.pallas_call(
        _peer_kernel,
        grid=(t // tt, n_exp // eb),
        in_specs=[pl.BlockSpec((tt, d), lambda i, j: (i, 0)),
                  pl.BlockSpec((nq, tt, LANES), lambda i, j: (0, i, 0)),
                  pl.BlockSpec(sk.shape, lambda i, j: (0, 0, 0)),
                  pl.BlockSpec((eb, d), lambda i, j: (j, 0)),
                  pl.BlockSpec((d, eb), lambda i, j: (0, j)),
                  pl.BlockSpec((tt, d), lambda i, j: (i, 0))],
        out_specs=pl.BlockSpec((tt, d), lambda i, j: (i, 0)),
        out_shape=jax.ShapeDtypeStruct((t, d), F32),
        scratch_shapes=[pltpu.VMEM((PEER_HEADS, PEER_NKEYS, tt), F32),
                        pltpu.VMEM((PEER_HEADS, PEER_NKEYS, tt), F32),
                        pltpu.VMEM((PEER_HEADS, PEER_NKEYS, tt), BF16),
                        pltpu.VMEM((PEER_HEADS, PEER_NKEYS, tt), BF16),
                        pltpu.VMEM((d, tt), F32)],
        compiler_params=_params(("parallel", "arbitrary")),
        name="peer",
    )(xn, qp, sk, u, vt, x1)


def _rope_tables(s):
    inv = 1.0 / (ROPE_THETA ** (jnp.arange(0, HEAD_DIM, 2, dtype=F32) / HEAD_DIM))
    ang = jnp.arange(s, dtype=F32)[:, None] * inv[None, :]
    cos, sin = jnp.cos(ang), jnp.sin(ang)
    reps = LANES // HEAD_DIM
    cos_t = jnp.tile(jnp.concatenate([cos, cos], axis=1), (1, reps))
    sin_t = jnp.tile(jnp.concatenate([-sin, sin], axis=1), (1, reps))
    return cos_t, sin_t


def _tile(n, pref):
    return pref if n % pref == 0 else n


def _trunk(x, P):
    b, s, d = x.shape
    t = b * s
    c = d
    xf = x.reshape(t, d)
    zg = _norm_matmul(xf, P["norm1_w"], P["w_g"], _tile(t, 512))

    cos_t, sin_t = _rope_tables(s)
    q, kt, va = _inproj_attn(x, P["norm1_w"], P["w_a"], cos_t, sin_t, P["q_norm_w"], P["k_norm_w"],
                             HEAD_DIM ** -0.5 * math.log2(math.e), _tile(s, 512))
    oa = _attention(P["lam_rows"], q, kt, va, P["subln_w"], _tile(s, 1024), _tile(s, 1024))

    (r, v, kk, kd0, kd1, ka0, ka1, lw0, lw1, g, bv) = _rwkv_prep(
        x, P["norm1_w"], P["w_r"], P["shift_mu"], P["decay_w0"], P["decay_w2"], P["iclr_a0"], P["iclr_a2"],
        P["gate_g2"], P["k_k"], P["k_a"], P["r_k"], c, _tile(s, 256))
    yf, yb = _wkv_scan(lw0, lw1, r, v, kk, ka0, ka1, kd0, kd1)

    flat = lambda a: a.reshape(t, a.shape[-1])
    x1, xn, qp = _post(xf, flat(oa), flat(yf), flat(yb), flat(g), flat(bv), zg, P["lnx_w"], P["lnx_b"],
                       P["w_o_attn"], P["w_o_rwkv"], P["w_out"], P["norm2_w"], P["peer_wq"], _tile(t, 256))
    y = _peer(xn, qp, P["peer_subkeys"], P["peer_u"], P["peer_vt"], x1, _tile(t, 512), 1024)
    return y.reshape(b, s, d)


def kernel(x_prompt, x_sample, norm1_w, w_in, q_norm_w, k_norm_w, lambda_q1, lambda_k1, lambda_q2, lambda_k2, subln_w, w_o_attn, shift_mu, decay_w0, decay_w2, iclr_a0, iclr_a2, gate_g2, k_k, k_a, r_k, lnx_w, lnx_b, w_o_rwkv, w_out, norm2_w, peer_wq, peer_subkeys, peer_u, peer_v):
    d = x_prompt.shape[-1]
    c = d
    att_cols = 3 * d
    rwkv_cols = shift_mu.shape[-1]
    l = 0
    w = w_in[l].astype(BF16)
    nsub = d // HEAD_DIM
    P = {
        "norm1_w": norm1_w[l][None, :],
        "w_a": w[:, :att_cols],
        "w_r": w[:, att_cols:att_cols + rwkv_cols],
        "w_g": w[:, att_cols + rwkv_cols:],
        "q_norm_w": jnp.tile(q_norm_w[l], nsub)[None, :],
        "k_norm_w": jnp.tile(k_norm_w[l], nsub)[None, :],
        "lam_rows": jnp.stack([lambda_q1[l], lambda_k1[l], lambda_q2[l], lambda_k2[l]]),
        "subln_w": subln_w[l][None, :],
        "w_o_attn": w_o_attn[l].astype(BF16),
        "shift_mu": shift_mu[l][None, :],
        "decay_w0": decay_w0[l],
        "decay_w2": decay_w2[l].reshape(-1, c).astype(BF16),
        "iclr_a0": iclr_a0[l],
        "iclr_a2": iclr_a2[l].reshape(-1, c).astype(BF16),
        "gate_g2": gate_g2[l].astype(BF16),
        "k_k": k_k[l][None, :],
        "k_a": k_a[l][None, :],
        "r_k": r_k[l].reshape(1, c),
        "lnx_w": lnx_w[l][None, :],
        "lnx_b": lnx_b[l][None, :],
        "w_o_rwkv": w_o_rwkv[l].astype(BF16),
        "w_out": w_out[l].astype(BF16),
        "norm2_w": norm2_w[l][None, :],
        "peer_wq": peer_wq[l].astype(BF16),
        "peer_subkeys": peer_subkeys[l].reshape(-1, PEER_NKEYS, peer_subkeys.shape[-1]).astype(BF16),
        "peer_u": peer_u[l].astype(BF16),
        "peer_vt": peer_v[l].astype(BF16).T,
    }
    return (_trunk(x_prompt, P), _trunk(x_sample, P))
```

```python
import functools
import math

import jax
import jax.numpy as jnp
from jax import lax
from jax.experimental import pallas as pl
from jax.experimental.pallas import tpu as pltpu

F32 = jnp.float32
BF16 = jnp.bfloat16

LANES = 128
HEAD_DIM = 64
ATT_HEADS = 8
NORM_EPS = 1e-6
RWKV_LN_EPS = 64e-5
ROPE_THETA = 10000.0
LAMBDA_INIT = 0.8 - 0.6 * math.exp(-0.3 * 0)
SCAN_CHUNK = 64
PEER_NKEYS = 128
PEER_TOPK = 16
PEER_HEADS = 8
PEER_SUB_ROWS = 4
VMEM_LIMIT = 56 * 1024 * 1024


def _params(sem):
    return pltpu.CompilerParams(dimension_semantics=sem, vmem_limit_bytes=VMEM_LIMIT)


def _dot(a, b):
    return jnp.dot(a, b, preferred_element_type=F32)


def _dot_nt(a, b):
    return lax.dot_general(a, b, (((1,), (1,)), ((), ())), preferred_element_type=F32)


def _split_dot(x, m):
    hi = x.astype(BF16)
    lo = (x - hi.astype(F32)).astype(BF16)
    return _dot(hi, m) + _dot(lo, m)


def _seg_mats(width, seg):
    lane = lax.broadcasted_iota(jnp.int32, (width, LANES), 0)
    grp = lax.broadcasted_iota(jnp.int32, (width, LANES), 1)
    e = jnp.where(lane // seg == grp, 1.0, 0.0).astype(BF16)
    grp_t = lax.broadcasted_iota(jnp.int32, (LANES, width), 0)
    lane_t = lax.broadcasted_iota(jnp.int32, (LANES, width), 1)
    et = jnp.where(lane_t // seg == grp_t, 1.0, 0.0).astype(BF16)
    return e, et


def _seg_sum(x, e, et):
    return _split_dot(_split_dot(x, e), et)


def _sigmoid(x):
    return 1.0 / (1.0 + jnp.exp(-x))


def _norm_matmul_kernel(x_ref, nw_ref, w_ref, o_ref):
    x = x_ref[...]
    ms = jnp.mean(x * x, axis=-1, keepdims=True)
    h = (x * lax.rsqrt(ms + NORM_EPS) * nw_ref[...]).astype(BF16)
    o_ref[...] = _dot(h, w_ref[...]).astype(o_ref.dtype)


def _norm_matmul(x, nw, w, tm):
    t, d = x.shape
    n = w.shape[1]
    return pl.pallas_call(
        _norm_matmul_kernel,
        grid=(t // tm,),
        in_specs=[pl.BlockSpec((tm, d), lambda i: (i, 0)),
                  pl.BlockSpec((1, d), lambda i: (0, 0)),
                  pl.BlockSpec((d, n), lambda i: (0, 0))],
        out_specs=pl.BlockSpec((tm, n), lambda i: (i, 0)),
        out_shape=jax.ShapeDtypeStruct((t, n), BF16),
        compiler_params=_params(("parallel",)),
        name="norm_matmul",
    )(x, nw, w)


def _norm_rope(x, w, c, s, e, et, scale):
    width = x.shape[1]
    ms = _seg_sum(x * x, e, et) * (1.0 / HEAD_DIM)
    y = x * lax.rsqrt(ms + NORM_EPS) * w
    lane = lax.broadcasted_iota(jnp.int32, y.shape, 1)
    half = HEAD_DIM // 2
    partner = jnp.where(lane % HEAD_DIM < half, pltpu.roll(y, width - half, 1), pltpu.roll(y, half, 1))
    return (y * c + partner * s) * scale


def _inproj_attn_kernel(x_ref, nw_ref, w_ref, cos_ref, sin_ref, qw_ref, kw_ref, q_ref, kt_ref, v_ref, *, q_scale):
    x = x_ref[...]
    ms = jnp.mean(x * x, axis=-1, keepdims=True)
    h = (x * lax.rsqrt(ms + NORM_EPS) * nw_ref[...]).astype(BF16)
    z = _dot(h, w_ref[...])
    width = q_ref.shape[1]
    reps = width // LANES
    e, et = _seg_mats(width, HEAD_DIM)
    c = jnp.concatenate([cos_ref[...]] * reps, axis=1)
    s = jnp.concatenate([sin_ref[...]] * reps, axis=1)
    q_ref[...] = _norm_rope(z[:, 0:width], qw_ref[...], c, s, e, et, q_scale).astype(q_ref.dtype)
    k = _norm_rope(z[:, width:2 * width], kw_ref[...], c, s, e, et, 1.0)
    for j in range(reps):
        kt_ref[j * LANES:(j + 1) * LANES, :] = k[:, j * LANES:(j + 1) * LANES].T.astype(kt_ref.dtype)
    v_ref[...] = z[:, 2 * width:3 * width].astype(v_ref.dtype)


def _inproj_attn(x, nw, w_a, cos_t, sin_t, qw, kw, q_scale, tm):
    b, s, d = x.shape
    width = qw.shape[1]
    tok = lambda bi, i: (bi, i, 0)
    const = lambda bi, i: (0, 0)
    return pl.pallas_call(
        functools.partial(_inproj_attn_kernel, q_scale=q_scale),
        grid=(b, s // tm),
        in_specs=[pl.BlockSpec((None, tm, d), tok),
                  pl.BlockSpec((1, d), const),
                  pl.BlockSpec(w_a.shape, const),
                  pl.BlockSpec((tm, LANES), lambda bi, i: (i, 0)),
                  pl.BlockSpec((tm, LANES), lambda bi, i: (i, 0)),
                  pl.BlockSpec((1, width), const),
                  pl.BlockSpec((1, width), const)],
        out_specs=[pl.BlockSpec((None, tm, width), tok),
                   pl.BlockSpec((None, width, tm), lambda bi, i: (bi, 0, i)),
                   pl.BlockSpec((None, tm, width), tok)],
        out_shape=[jax.ShapeDtypeStruct((b, s, width), BF16),
                   jax.ShapeDtypeStruct((b, width, s), BF16),
                   jax.ShapeDtypeStruct((b, s, width), BF16)],
        compiler_params=_params(("parallel", "parallel")),
        name="inproj_attn",
    )(x, nw, w_a, cos_t, sin_t, qw, kw)


ATTN_ROW_CHUNK = 512


def _attn_kernel(lam_ref, q_ref, kt_ref, v_ref, sw_ref, o_ref, qs_ref, m_ref, acc_ref, s0_ref, *, tq, tk):
    q = q_ref[...]
    lane = lax.broadcasted_iota(jnp.int32, q.shape, 1)
    zero = jnp.zeros_like(q)
    qs_ref[0:tq, :] = jnp.where(lane < HEAD_DIM, q, zero)
    qs_ref[tq:2 * tq, :] = jnp.where(lane < HEAD_DIM, zero, q)
    m_ref[...] = jnp.full(m_ref.shape, -jnp.inf, F32)
    acc_ref[...] = jnp.zeros(acc_ref.shape, F32)

    rc = min(ATTN_ROW_CHUNK, 2 * tq)
    n_chunks = 2 * tq // rc
    n_tiles = kt_ref.shape[1] // tk
    s0_ref[...] = _dot(qs_ref[0:rc, :], kt_ref[:, 0:tk])

    def key_tile(j, carry):
        off = pl.multiple_of(j * tk, tk)
        v = v_ref[pl.ds(off, tk), :]
        v_ones = jnp.concatenate([v, jnp.ones_like(v)], axis=1)
        kt = kt_ref[:, pl.ds(off, tk)]
        scores = [None] * n_chunks
        scores[0] = s0_ref[...]
        for c in range(n_chunks):
            rows = slice(c * rc, (c + 1) * rc)
            if c + 1 < n_chunks:
                scores[c + 1] = _dot(qs_ref[(c + 1) * rc:(c + 2) * rc, :], kt)
            else:
                off_next = pl.multiple_of(jnp.minimum(j + 1, n_tiles - 1) * tk, tk)
                s0_ref[...] = _dot(qs_ref[0:rc, :], kt_ref[:, pl.ds(off_next, tk)])
            s = scores[c]
            m_prev = m_ref[rows, :]
            m_new = jnp.maximum(m_prev, jnp.max(s, axis=1, keepdims=True))
            alpha = jnp.exp2(m_prev - m_new)
            p = jnp.exp2((s - jnp.concatenate([m_new] * (tk // LANES), axis=1)).astype(BF16))
            acc_ref[rows, :] = (jnp.concatenate([alpha, alpha], axis=1) * acc_ref[rows, :]
                                + _dot(p, v_ones))
            m_ref[rows, :] = m_new
        return carry

    lax.fori_loop(0, n_tiles, key_tile, 0)

    lv = lam_ref[...]
    lam = (jnp.exp(jnp.sum(lv[0:1] * lv[1:2], axis=1, keepdims=True))
           - jnp.exp(jnp.sum(lv[2:3] * lv[3:4], axis=1, keepdims=True)) + LAMBDA_INIT)
    acc = acc_ref[...]
    o = (acc[0:tq, 0:LANES] / acc[0:tq, LANES:2 * LANES]
         - lam * (acc[tq:2 * tq, 0:LANES] / acc[tq:2 * tq, LANES:2 * LANES]))
    ms = jnp.mean(o * o, axis=-1, keepdims=True)
    o = o * lax.rsqrt(ms + NORM_EPS) * sw_ref[...] * (1.0 - LAMBDA_INIT)
    o_ref[...] = o.astype(o_ref.dtype)


def _attention(lam_rows, q, kt, v, subln_w, tq, tk):
    b, s, width = q.shape
    heads = width // LANES
    return pl.pallas_call(
        functools.partial(_attn_kernel, tq=tq, tk=tk),
        grid=(b, heads, s // tq),
        in_specs=[pl.BlockSpec((4, HEAD_DIM), lambda bi, h, qi: (0, 0)),
                  pl.BlockSpec((None, tq, LANES), lambda bi, h, qi: (bi, qi, h)),
                  pl.BlockSpec((None, LANES, s), lambda bi, h, qi: (bi, h, 0)),
                  pl.BlockSpec((None, s, LANES), lambda bi, h, qi: (bi, 0, h)),
                  pl.BlockSpec((1, LANES), lambda bi, h, qi: (0, 0))],
        out_specs=pl.BlockSpec((None, tq, LANES), lambda bi, h, qi: (bi, qi, h)),
        out_shape=jax.ShapeDtypeStruct((b, s, width), BF16),
        scratch_shapes=[pltpu.VMEM((2 * tq, LANES), BF16),
                        pltpu.VMEM((2 * tq, LANES), F32),
                        pltpu.VMEM((2 * tq, 2 * LANES), F32),
                        pltpu.VMEM((min(ATTN_ROW_CHUNK, 2 * tq), tk), F32)],
        compiler_params=_params(("parallel", "parallel", "arbitrary")),
        name="diff_attention",
    )(lam_rows, q, kt, v, subln_w)


HALO = 16


def _rwkv_prep_kernel(x_ref, xp_ref, xn_ref, nw_ref, w_ref, mu_ref, w0_ref, w2_ref, a0_ref, a2_ref, g2_ref,
                      kk_w_ref, ka_w_ref, rk_ref,
                      r_ref, v_ref, kk_ref, kd0_ref, kd1_ref, ka0_ref, ka1_ref, lw0_ref, lw1_ref,
                      g_ref, bv_ref, *, c):
    i = pl.program_id(1)

    def project(x):
        ms = jnp.mean(x * x, axis=-1, keepdims=True)
        return _dot((x * lax.rsqrt(ms + NORM_EPS) * nw_ref[...]).astype(BF16), w_ref[...])

    z = project(x_ref[...])
    tm = z.shape[0]
    row = lax.broadcasted_iota(jnp.int32, z.shape, 0)
    prev_row = jnp.where(i == 0, 0.0, project(xp_ref[...])[HALO - 1:HALO, :])
    next_row = jnp.where(i == pl.num_programs(1) - 1, 0.0, project(xn_ref[...])[0:1, :])
    z_up = jnp.where(row == 0, prev_row, pltpu.roll(z, 1, 0))
    z_dn = jnp.where(row == tm - 1, next_row, pltpu.roll(z, tm - 1, 0))
    zs = z + mu_ref[...] * (0.5 * (z_up + z_dn) - z)

    r = zs[:, 0:c]
    k = zs[:, c:2 * c]
    v = zs[:, 2 * c:3 * c]
    wl = zs[:, 3 * c:3 * c + LANES]
    al = zs[:, 3 * c + LANES:3 * c + 2 * LANES]
    gl = zs[:, 3 * c + 2 * LANES:3 * c + 3 * LANES]

    lane = lax.broadcasted_iota(jnp.int32, wl.shape, 1)
    twl = jnp.tanh(wl)
    e, et = _seg_mats(c, HEAD_DIM)

    kk = k * kk_w_ref[...]
    kk = kk * lax.rsqrt(_seg_sum(kk * kk, e, et) + 1e-12)
    kk_ref[...] = kk.astype(kk_ref.dtype)
    r_ref[...] = r.astype(r_ref.dtype)
    v_ref[...] = v.astype(v_ref.dtype)
    g_ref[...] = _dot(_sigmoid(gl).astype(BF16), g2_ref[...]).astype(g_ref.dtype)

    ksum = jnp.zeros_like(k)
    for d, (kd_ref, ka_ref, lw_ref) in enumerate(((kd0_ref, ka0_ref, lw0_ref), (kd1_ref, ka1_ref, lw1_ref))):
        sel = (lane < HEAD_DIM) if d == 0 else (lane >= HEAD_DIM)
        wpre = _dot(jnp.where(sel, twl, 0.0).astype(BF16), w2_ref[...]) + w0_ref[d:d + 1, :]
        lw_ref[...] = (-math.exp(-0.5)) * _sigmoid(wpre)
        a = _sigmoid(_dot(jnp.where(sel, al, 0.0).astype(BF16), a2_ref[...]) + a0_ref[d:d + 1, :])
        kd = k * (1.0 + (a - 1.0) * ka_w_ref[...])
        kd_ref[...] = kd.astype(kd_ref.dtype)
        ka_ref[...] = (kk * a).astype(ka_ref.dtype)
        ksum = ksum + kd
    bonus = _seg_sum(r * ksum * rk_ref[...], e, et)
    bv_ref[...] = (bonus * v).astype(bv_ref.dtype)


def _rwkv_prep(x, nw, w_r, mu, w0, w2, a0, a2, g2, kk_w, ka_w, rk, c, tm):
    b, s, d = x.shape
    cols = w_r.shape[1]
    nh = tm // HALO
    last = s // HALO - 1
    tok = lambda bi, i: (bi, i, 0)
    const = lambda bi, i: (0, 0)
    out = lambda dt: jax.ShapeDtypeStruct((b, s, c), dt)
    ospec = pl.BlockSpec((None, tm, c), tok)
    return pl.pallas_call(
        functools.partial(_rwkv_prep_kernel, c=c),
        grid=(b, s // tm),
        in_specs=[pl.BlockSpec((None, tm, d), tok),
                  pl.BlockSpec((None, HALO, d), lambda bi, i: (bi, jnp.maximum(i * nh - 1, 0), 0)),
                  pl.BlockSpec((None, HALO, d), lambda bi, i: (bi, jnp.minimum((i + 1) * nh, last), 0)),
                  pl.BlockSpec((1, d), const),
                  pl.BlockSpec((d, cols), const),
                  pl.BlockSpec((1, cols), const),
                  pl.BlockSpec((2, c), const),
                  pl.BlockSpec((LANES, c), const),
                  pl.BlockSpec((2, c), const),
                  pl.BlockSpec((LANES, c), const),
                  pl.BlockSpec((LANES, c), const),
                  pl.BlockSpec((1, c), const),
                  pl.BlockSpec((1, c), const),
                  pl.BlockSpec((1, c), const)],
        out_specs=[ospec] * 11,
        out_shape=[out(BF16)] * 7 + [out(F32)] * 2 + [out(BF16)] * 2,
        compiler_params=_params(("parallel", "parallel")),
        name="rwkv_prep",
    )(x, x, x, nw, w_r, mu, w0, w2, a0, a2, g2, kk_w, ka_w, rk)


def _wkv_scan_kernel(lwf_ref, rf_ref, vf_ref, kkf_ref, kaf_ref, kdf_ref,
                     lwb_ref, rb_ref, vb_ref, kkb_ref, kab_ref, kdb_ref,
                     yf_ref, yb_ref, st_ref):
    L = SCAN_CHUNK
    P2 = 2 * L
    ci = pl.program_id(1)

    @pl.when(ci == 0)
    def _():
        st_ref[...] = jnp.zeros(st_ref.shape, F32)

    row = lax.broadcasted_iota(jnp.int32, (P2, P2), 0)
    col = lax.broadcasted_iota(jnp.int32, (P2, P2), 1)
    same_head = (row // L) == (col // L)
    rt = row % L
    ct = col % L
    eye = row == col
    ti = lax.broadcasted_iota(jnp.int32, (L, L), 0)
    tj = lax.broadcasted_iota(jnp.int32, (L, L), 1)
    lane_a = lax.broadcasted_iota(jnp.int32, (L, LANES), 1) < HEAD_DIM

    def expand(x):
        return jnp.concatenate([jnp.where(lane_a, x, 0.0), jnp.where(lane_a, 0.0, x)], axis=0)

    units = []
    dirs = ((lwf_ref, rf_ref, vf_ref, kkf_ref, kaf_ref, kdf_ref, yf_ref),
            (lwb_ref, rb_ref, vb_ref, kkb_ref, kab_ref, kdb_ref, yb_ref))
    for d, (lw_ref, r_ref, v_ref, kk_ref, ka_ref, kd_ref, y_ref) in enumerate(dirs):
        if d == 0:
            m_strict = same_head & (rt > ct)
            m_incl = same_head & (rt >= ct)
            tri = jnp.where(tj <= ti, 1.0, 0.0).astype(BF16)
        else:
            m_strict = same_head & (rt < ct)
            m_incl = same_head & (rt <= ct)
            tri = jnp.where(tj >= ti, 1.0, 0.0).astype(BF16)
        m_level = []
        for lvl in range(L.bit_length() - 1):
            joined = same_head & ((rt >> (lvl + 1)) == (ct >> (lvl + 1)))
            r_bit = (rt >> lvl) & 1
            c_bit = (ct >> lvl) & 1
            later_row = (r_bit == 1) & (c_bit == 0) if d == 0 else (r_bit == 0) & (c_bit == 1)
            m_level.append(joined & later_row)
        lw = lw_ref[...]
        hi = lw.astype(BF16)
        r1 = lw - hi.astype(F32)
        mid = r1.astype(BF16)
        lo = (r1 - mid.astype(F32)).astype(BF16)
        cum = _dot(tri, hi) + _dot(tri, mid) + _dot(tri, lo)
        cum_last = cum[L - 1:L, :] if d == 0 else cum[0:1, :]
        ec = jnp.exp(cum)
        enc = jnp.exp(-cum)
        p_last = jnp.exp(cum_last)
        a_t = -kk_ref[...].astype(F32) * jnp.exp(cum - lw)
        r_t = r_ref[...].astype(F32) * ec
        b_t = ka_ref[...].astype(F32) * enc
        k_t = kd_ref[...].astype(F32) * enc
        b_h = b_t * p_last
        k_h = k_t * p_last
        vv = v_ref[...].astype(F32)

        for p in range(lw.shape[1] // LANES):
            sl = slice(p * LANES, (p + 1) * LANES)
            units.append(dict(
                d=d, p=p, sl=sl, y_ref=y_ref, m_strict=m_strict, m_incl=m_incl, m_level=m_level,
                ea=expand(a_t[:, sl]), er=expand(r_t[:, sl]), vexp=expand(vv[:, sl]),
                bk2=jnp.concatenate([b_t[:, sl]] * 2 + [k_t[:, sl]] * 2, axis=0).astype(BF16),
                bh=b_h[:, sl], kh=k_h[:, sl], p_last=p_last[:, sl]))

    for u in units:
        g = _dot_nt(jnp.concatenate([u["ea"], u["er"]], axis=0).astype(BF16), u["bk2"])
        u["xab"] = jnp.where(u["m_strict"], g[0:P2, 0:P2], 0.0)
        u["xak"] = jnp.where(u["m_strict"], g[0:P2, P2:2 * P2], 0.0).astype(BF16)
        u["xr"] = jnp.concatenate([jnp.where(u["m_incl"], g[P2:2 * P2, 0:P2], 0.0),
                                   jnp.where(u["m_incl"], g[P2:2 * P2, P2:2 * P2], 0.0)], axis=1).astype(BF16)
        u["tinv"] = jnp.where(eye, 1.0, jnp.where(u["m_level"][0], u["xab"], 0.0))
    for lvl in range(1, L.bit_length() - 1):
        for u in units:
            xoff = jnp.where(u["m_level"][lvl], u["xab"], 0.0).astype(BF16)
            u["tx"] = _dot(u["tinv"].astype(BF16), xoff).astype(BF16)
        for u in units:
            u["tinv"] = u["tinv"] + _dot(u["tx"], u["tinv"].astype(BF16))
    for u in units:
        u["xv"] = _dot(u["xak"], u["vexp"].astype(BF16))
    for u in units:
        u["wu0"] = _dot(u["tinv"].astype(BF16), jnp.concatenate([u["ea"], u["xv"]], axis=1).astype(BF16))
        u["bkt"] = jnp.concatenate([expand(u["bh"]).T, expand(u["kh"]).T], axis=1).astype(BF16)
        u["p_col"] = jnp.broadcast_to(u["p_last"], (P2, LANES)).T
    for u in units:
        u["st"] = st_ref[u["d"], u["p"]]
        u["ws"] = _dot(jnp.concatenate([u["wu0"][:, 0:LANES], u["er"]], axis=0).astype(BF16), u["st"].astype(BF16))
    for u in units:
        uu = u["ws"][0:P2] + u["wu0"][:, LANES:2 * LANES]
        u["uv"] = jnp.concatenate([uu, u["vexp"]], axis=0).astype(BF16)
    for u in units:
        yexp = u["ws"][P2:2 * P2] + _dot(u["xr"], u["uv"])
        u["y_ref"][:, u["sl"]] = (yexp[0:L] + yexp[L:P2]).astype(u["y_ref"].dtype)
    for u in units:
        st_ref[u["d"], u["p"]] = u["p_col"] * u["st"] + _dot(u["bkt"], u["uv"])


def _wkv_scan(lw0, lw1, r, v, kk, ka0, ka1, kd0, kd1):
    b, s, c = r.shape
    L = SCAN_CHUNK
    nc = s // L
    fwd = lambda bi, i: (bi, i, 0)
    bwd = lambda bi, i: (bi, nc - 1 - i, 0)
    fs = pl.BlockSpec((None, L, c), fwd)
    bs = pl.BlockSpec((None, L, c), bwd)
    return pl.pallas_call(
        _wkv_scan_kernel,
        grid=(b, nc),
        in_specs=[fs] * 6 + [bs] * 6,
        out_specs=[fs, bs],
        out_shape=[jax.ShapeDtypeStruct((b, s, c), F32)] * 2,
        scratch_shapes=[pltpu.VMEM((2, c // LANES, LANES, LANES), F32)],
        compiler_params=_params(("parallel", "arbitrary")),
        name="wkv_scan",
    )(lw0, r, v, kk, ka0, kd0, lw1, r, v, kk, ka1, kd1)


def _post_kernel(x_ref, oa_ref, yf_ref, yb_ref, g_ref, bv_ref, zg_ref, lnw_ref, lnb_ref,
                 woa_ref, wor_ref, wout_ref, n2w_ref, wq_ref, x1_ref, xn_ref, qp_ref):
    y = yf_ref[...] + yb_ref[...]
    c = y.shape[1]
    e, et = _seg_mats(c, HEAD_DIM)
    mu = _seg_sum(y, e, et) * (1.0 / HEAD_DIM)
    dlt = y - mu
    var = _seg_sum(dlt * dlt, e, et) * (1.0 / HEAD_DIM)
    yn = dlt * lax.rsqrt(var + RWKV_LN_EPS) * lnw_ref[...] + lnb_ref[...]
    out = ((yn + bv_ref[...].astype(F32)) * g_ref[...].astype(F32)).astype(BF16)
    h_rwkv = _dot(out, wor_ref[...])
    h_attn = _dot(oa_ref[...], woa_ref[...])
    zg = zg_ref[...].astype(F32)
    merged = _sigmoid(zg[:, 0:c]) * h_attn + _sigmoid(zg[:, c:2 * c]) * h_rwkv
    x1 = x_ref[...] + _dot(merged.astype(BF16), wout_ref[...])
    x1_ref[...] = x1
    ms = jnp.mean(x1 * x1, axis=-1, keepdims=True)
    xn = (x1 * lax.rsqrt(ms + NORM_EPS) * n2w_ref[...]).astype(BF16)
    xn_ref[...] = xn
    qp = _dot(xn, wq_ref[...]).astype(BF16)
    for j in range(qp_ref.shape[0]):
        qp_ref[j] = qp[:, j * LANES:(j + 1) * LANES]


def _post(x, oa, yf, yb, g, bv, zg, lnw, lnb, woa, wor, wout, n2w, wq, tm):
    t, d = x.shape
    nq = wq.shape[1] // LANES
    tok = lambda i: (i, 0)
    const = lambda i: (0, 0)
    tspec = lambda w: pl.BlockSpec((tm, w), tok)
    cspec = lambda a: pl.BlockSpec(a.shape, const)
    return pl.pallas_call(
        _post_kernel,
        grid=(t // tm,),
        in_specs=[tspec(d), tspec(d), tspec(d), tspec(d), tspec(d), tspec(d), tspec(2 * d),
                  cspec(lnw), cspec(lnb), cspec(woa), cspec(wor), cspec(wout), cspec(n2w), cspec(wq)],
        out_specs=[tspec(d), tspec(d), pl.BlockSpec((nq, tm, LANES), lambda i: (0, i, 0))],
        out_shape=[jax.ShapeDtypeStruct((t, d), F32), jax.ShapeDtypeStruct((t, d), BF16),
                   jax.ShapeDtypeStruct((nq, t, LANES), BF16)],
        compiler_params=_params(("parallel",)),
        name="post_merge",
    )(x, oa, yf, yb, g, bv, zg, lnw, lnb, woa, wor, wout, n2w, wq)


SUBLANES = 8


def _sort_pairs(n):
    def merge(lo, hi, r):
        step = r * 2
        if step < hi - lo:
            yield from merge(lo, hi, step)
            yield from merge(lo + r, hi, step)
            for i in range(lo + r, hi - r, step):
                yield (i, i + r)
        else:
            yield (lo, lo + r)

    def sort(lo, hi):
        if hi - lo >= 1:
            mid = lo + (hi - lo) // 2
            yield from sort(lo, mid)
            yield from sort(mid + 1, hi)
            yield from merge(lo, hi, 1)

    return list(sort(0, n - 1))


def _exchange(xs, i, j):
    a, b = xs[i], xs[j]
    if b is None:
        return
    if a is None:
        xs[i], xs[j] = b, None
        return
    xs[i], xs[j] = jnp.maximum(a, b), jnp.minimum(a, b)


def _top_sorted(blocks, k):
    xs = list(blocks)
    for i, j in _sort_pairs(k):
        _exchange(xs, i, j)
    shift = SUBLANES // 2
    while shift >= 1:
        ys = [None if x is None else pltpu.roll(x, shift, 0) for x in xs]
        zs = []
        for i in range(k):
            a, b = xs[i], ys[k - 1 - i]
            zs.append(b if a is None else (a if b is None else jnp.maximum(a, b)))
        step = k // 2
        while step >= 1:
            for i in range(k):
                if i & step == 0:
                    _exchange(zs, i, i + step)
            step //= 2
        xs = zs
        shift //= 2
    return [x[0:1, :] for x in xs]


def _row_blocks(x):
    return [x[r:r + SUBLANES, :] for r in range(0, x.shape[0], SUBLANES)]


def _peer_select(s0, s1, k):
    a = _top_sorted(_row_blocks(s0), k)
    b = _top_sorted(_row_blocks(s1), k)
    amat = jnp.concatenate(a, axis=0)
    bmat = jnp.concatenate(b, axis=0)
    h8 = k // 2
    cand = ([a[0] + bmat[0:h8], a[0] + bmat[h8:k], a[1] + bmat[0:h8], amat[h8:k] + b[0]]
            + [a[i] + bmat[0:h8] for i in range(2, h8)])
    best = _top_sorted(cand + [None] * (k - len(cand)), k)
    z = jnp.zeros_like(best[0])
    for val in best:
        z = z + jnp.exp(val - best[0])
    return bmat, a[0], best[k - 1], z


def _peer_kernel(xn_ref, qp_ref, sk_ref, u_ref, vt_ref, x1_ref, o_ref,
                 cnt_ref, e0_ref, rank_ref, e1_ref, acc_ref):
    jb = pl.program_id(1)
    tt = xn_ref.shape[0]
    K = PEER_TOPK
    rows_per_blk = u_ref.shape[0] // PEER_NKEYS

    @pl.when(jb == 0)
    def _():
        acc_ref[...] = jnp.zeros(acc_ref.shape, F32)

        def head_body(h, carry):
            s0 = _dot_nt(sk_ref[2 * h], qp_ref[2 * h])
            s1 = _dot_nt(sk_ref[2 * h + 1], qp_ref[2 * h + 1])

            bmat, a0, thr, z = _peer_select(s0, s1, K)
            cnt = jnp.zeros(s0.shape, F32)
            rank = jnp.full(s1.shape, float(PEER_NKEYS), F32)
            for kk in range(K):
                cnt = cnt + jnp.where(s0 + bmat[kk:kk + 1] >= thr, 1.0, 0.0)
            for kk in reversed(range(K)):
                rank = jnp.where(s1 == bmat[kk:kk + 1], float(kk + 1), rank)
            cnt_ref[h] = cnt
            e0_ref[h] = jnp.exp(s0 - a0) / z
            rank_ref[h] = rank.astype(BF16)
            e1_ref[h] = jnp.exp(s1 - bmat[0:1]).astype(BF16)
            return carry

        lax.fori_loop(0, PEER_HEADS, head_body, 0)

    sub = PEER_SUB_ROWS * PEER_NKEYS
    xn = xn_ref[...]
    n_sub = u_ref.shape[0] // sub
    hpres = [None] * n_sub
    hpres[0] = _dot_nt(u_ref[0:sub, :], xn)
    zero = jnp.zeros((), BF16)
    for c in range(n_sub):
        if c + 1 < n_sub:
            hpres[c + 1] = _dot_nt(u_ref[(c + 1) * sub:(c + 2) * sub, :], xn)
        row0 = jb * rows_per_blk + c * PEER_SUB_ROWS
        crows = [[cnt_ref[h, pl.ds(row0 + il, 1), :].astype(BF16) for il in range(PEER_SUB_ROWS)]
                 for h in range(PEER_HEADS)]
        e0rows = [[e0_ref[h, pl.ds(row0 + il, 1), :].astype(BF16) for il in range(PEER_SUB_ROWS)]
                  for h in range(PEER_HEADS)]
        g_cols = []
        for lc in range(tt // LANES):
            lanes = slice(lc * LANES, (lc + 1) * LANES)
            accs = [None] * PEER_SUB_ROWS
            for h in range(PEER_HEADS):
                rank_h = rank_ref[h, :, lanes]
                e1_h = e1_ref[h, :, lanes]
                for il in range(PEER_SUB_ROWS):
                    crow = crows[h][il][:, lanes]
                    e0row = e0rows[h][il][:, lanes]
                    contrib = jnp.where(rank_h <= crow, e0row * e1_h, zero)
                    accs[il] = contrib if accs[il] is None else accs[il] + contrib
            g_cols.append(jnp.concatenate(accs, axis=0))
        gates = jnp.concatenate(g_cols, axis=1)
        hpre = hpres[c]
        gelu = 0.5 * hpre * (1.0 + lax.erf(hpre * (2.0 ** -0.5)))
        hg = gelu.astype(BF16) * gates
        acc_ref[...] += _dot(vt_ref[:, c * sub:(c + 1) * sub], hg)

    @pl.when(jb == pl.num_programs(1) - 1)
    def _():
        o_ref[...] = x1_ref[...] + acc_ref[...].T


def _peer(xn, qp, sk, u, vt, x1, tt, eb):
    t, d = xn.shape
    n_exp = u.shape[0]
    nq = qp.shape[0]
    return pl.pallas_call(
        _peer_kernel,
        grid=(t // tt, n_exp // eb),
        in_specs=[pl.BlockSpec((tt, d), lambda i, j: (i, 0)),
                  pl.BlockSpec((nq, tt, LANES), lambda i, j: (0, i, 0)),
                  pl.BlockSpec(sk.shape, lambda i, j: (0, 0, 0)),
                  pl.BlockSpec((eb, d), lambda i, j: (j, 0)),
                  pl.BlockSpec((d, eb), lambda i, j: (0, j)),
                  pl.BlockSpec((tt, d), lambda i, j: (i, 0))],
        out_specs=pl.BlockSpec((tt, d), lambda i, j: (i, 0)),
        out_shape=jax.ShapeDtypeStruct((t, d), F32),
        scratch_shapes=[pltpu.VMEM((PEER_HEADS, PEER_NKEYS, tt), F32),
                        pltpu.VMEM((PEER_HEADS, PEER_NKEYS, tt), F32),
                        pltpu.VMEM((PEER_HEADS, PEER_NKEYS, tt), BF16),
                        pltpu.VMEM((PEER_HEADS, PEER_NKEYS, tt), BF16),
                        pltpu.VMEM((d, tt), F32)],
        compiler_params=_params(("parallel", "arbitrary")),
        name="peer",
    )(xn, qp, sk, u, vt, x1)


def _rope_tables(s):
    inv = 1.0 / (ROPE_THETA ** (jnp.arange(0, HEAD_DIM, 2, dtype=F32) / HEAD_DIM))
    ang = jnp.arange(s, dtype=F32)[:, None] * inv[None, :]
    cos, sin = jnp.cos(ang), jnp.sin(ang)
    reps = LANES // HEAD_DIM
    cos_t = jnp.tile(jnp.concatenate([cos, cos], axis=1), (1, reps))
    sin_t = jnp.tile(jnp.concatenate([-sin, sin], axis=1), (1, reps))
    return cos_t, sin_t


def _tile(n, pref):
    return pref if n % pref == 0 else n


def _trunk(x, P):
    b, s, d = x.shape
    t = b * s
    c = d
    xf = x.reshape(t, d)
    zg = _norm_matmul(xf, P["norm1_w"], P["w_g"], _tile(t, 512))

    cos_t, sin_t = _rope_tables(s)
    q, kt, va = _inproj_attn(x, P["norm1_w"], P["w_a"], cos_t, sin_t, P["q_norm_w"], P["k_norm_w"],
                             HEAD_DIM ** -0.5 * math.log2(math.e), _tile(s, 512))
    oa = _attention(P["lam_rows"], q, kt, va, P["subln_w"], _tile(s, 1024), _tile(s, 1024))

    (r, v, kk, kd0, kd1, ka0, ka1, lw0, lw1, g, bv) = _rwkv_prep(
        x, P["norm1_w"], P["w_r"], P["shift_mu"], P["decay_w0"], P["decay_w2"], P["iclr_a0"], P["iclr_a2"],
        P["gate_g2"], P["k_k"], P["k_a"], P["r_k"], c, _tile(s, 256))
    yf, yb = _wkv_scan(lw0, lw1, r, v, kk, ka0, ka1, kd0, kd1)

    flat = lambda a: a.reshape(t, a.shape[-1])
    x1, xn, qp = _post(xf, flat(oa), flat(yf), flat(yb), flat(g), flat(bv), zg, P["lnx_w"], P["lnx_b"],
                       P["w_o_attn"], P["w_o_rwkv"], P["w_out"], P["norm2_w"], P["peer_wq"], _tile(t, 256))
    y = _peer(xn, qp, P["peer_subkeys"], P["peer_u"], P["peer_vt"], x1, _tile(t, 512), 1024)
    return y.reshape(b, s, d)


def kernel(x_prompt, x_sample, norm1_w, w_in, q_norm_w, k_norm_w, lambda_q1, lambda_k1, lambda_q2, lambda_k2, subln_w, w_o_attn, shift_mu, decay_w0, decay_w2, iclr_a0, iclr_a2, gate_g2, k_k, k_a, r_k, lnx_w, lnx_b, w_o_rwkv, w_out, norm2_w, peer_wq, peer_subkeys, peer_u, peer_v):
    d = x_prompt.shape[-1]
    c = d
    att_cols = 3 * d
    rwkv_cols = shift_mu.shape[-1]
    l = 0
    w = w_in[l].astype(BF16)
    nsub = d // HEAD_DIM
    P = {
        "norm1_w": norm1_w[l][None, :],
        "w_a": w[:, :att_cols],
        "w_r": w[:, att_cols:att_cols + rwkv_cols],
        "w_g": w[:, att_cols + rwkv_cols:],
        "q_norm_w": jnp.tile(q_norm_w[l], nsub)[None, :],
        "k_norm_w": jnp.tile(k_norm_w[l], nsub)[None, :],
        "lam_rows": jnp.stack([lambda_q1[l], lambda_k1[l], lambda_q2[l], lambda_k2[l]]),
        "subln_w": subln_w[l][None, :],
        "w_o_attn": w_o_attn[l].astype(BF16),
        "shift_mu": shift_mu[l][None, :],
        "decay_w0": decay_w0[l],
        "decay_w2": decay_w2[l].reshape(-1, c).astype(BF16),
        "iclr_a0": iclr_a0[l],
        "iclr_a2": iclr_a2[l].reshape(-1, c).astype(BF16),
        "gate_g2": gate_g2[l].astype(BF16),
        "k_k": k_k[l][None, :],
        "k_a": k_a[l][None, :],
        "r_k": r_k[l].reshape(1, c),
        "lnx_w": lnx_w[l][None, :],
        "lnx_b": lnx_b[l][None, :],
        "w_o_rwkv": w_o_rwkv[l].astype(BF16),
        "w_out": w_out[l].astype(BF16),
        "norm2_w": norm2_w[l][None, :],
        "peer_wq": peer_wq[l].astype(BF16),
        "peer_subkeys": peer_subkeys[l].reshape(-1, PEER_NKEYS, peer_subkeys.shape[-1]).astype(BF16),
        "peer_u": peer_u[l].astype(BF16),
        "peer_vt": peer_v[l].astype(BF16).T,
    }
    return (_trunk(x_prompt, P), _trunk(x_sample, P))
```

```python
import functools
import math

import jax
import jax.numpy as jnp
from jax import lax
from jax.experimental import pallas as pl
from jax.experimental.pallas import tpu as pltpu

F32 = jnp.float32
BF16 = jnp.bfloat16

LANES = 128
HEAD_DIM = 64
ATT_HEADS = 8
NORM_EPS = 1e-6
RWKV_LN_EPS = 64e-5
ROPE_THETA = 10000.0
LAMBDA_INIT = 0.8 - 0.6 * math.exp(-0.3 * 0)
SCAN_CHUNK = 64
PEER_NKEYS = 128
PEER_TOPK = 16
PEER_HEADS = 8
PEER_SUB_ROWS = 2
VMEM_LIMIT = 56 * 1024 * 1024


def _params(sem):
    return pltpu.CompilerParams(dimension_semantics=sem, vmem_limit_bytes=VMEM_LIMIT)


def _dot(a, b):
    return jnp.dot(a, b, preferred_element_type=F32)


def _dot_nt(a, b):
    return lax.dot_general(a, b, (((1,), (1,)), ((), ())), preferred_element_type=F32)


def _split_dot(x, m):
    hi = x.astype(BF16)
    lo = (x - hi.astype(F32)).astype(BF16)
    return _dot(hi, m) + _dot(lo, m)


def _seg_mats(width, seg):
    lane = lax.broadcasted_iota(jnp.int32, (width, LANES), 0)
    grp = lax.broadcasted_iota(jnp.int32, (width, LANES), 1)
    e = jnp.where(lane // seg == grp, 1.0, 0.0).astype(BF16)
    grp_t = lax.broadcasted_iota(jnp.int32, (LANES, width), 0)
    lane_t = lax.broadcasted_iota(jnp.int32, (LANES, width), 1)
    et = jnp.where(lane_t // seg == grp_t, 1.0, 0.0).astype(BF16)
    return e, et


def _seg_sum(x, e, et):
    return _split_dot(_split_dot(x, e), et)


def _sigmoid(x):
    return 1.0 / (1.0 + jnp.exp(-x))


def _norm_matmul_kernel(x_ref, nw_ref, w_ref, o_ref):
    x = x_ref[...]
    ms = jnp.mean(x * x, axis=-1, keepdims=True)
    h = (x * lax.rsqrt(ms + NORM_EPS) * nw_ref[...]).astype(BF16)
    o_ref[...] = _dot(h, w_ref[...]).astype(o_ref.dtype)


def _norm_matmul(x, nw, w, tm):
    t, d = x.shape
    n = w.shape[1]
    return pl.pallas_call(
        _norm_matmul_kernel,
        grid=(t // tm,),
        in_specs=[pl.BlockSpec((tm, d), lambda i: (i, 0)),
                  pl.BlockSpec((1, d), lambda i: (0, 0)),
                  pl.BlockSpec((d, n), lambda i: (0, 0))],
        out_specs=pl.BlockSpec((tm, n), lambda i: (i, 0)),
        out_shape=jax.ShapeDtypeStruct((t, n), BF16),
        compiler_params=_params(("parallel",)),
        name="norm_matmul",
    )(x, nw, w)


def _norm_rope(x, w, c, s, e, et, scale):
    width = x.shape[1]
    ms = _seg_sum(x * x, e, et) * (1.0 / HEAD_DIM)
    y = x * lax.rsqrt(ms + NORM_EPS) * w
    lane = lax.broadcasted_iota(jnp.int32, y.shape, 1)
    half = HEAD_DIM // 2
    partner = jnp.where(lane % HEAD_DIM < half, pltpu.roll(y, width - half, 1), pltpu.roll(y, half, 1))
    return (y * c + partner * s) * scale


def _inproj_attn_kernel(x_ref, nw_ref, w_ref, cos_ref, sin_ref, qw_ref, kw_ref, q_ref, kt_ref, v_ref, *, q_scale):
    x = x_ref[...]
    ms = jnp.mean(x * x, axis=-1, keepdims=True)
    h = (x * lax.rsqrt(ms + NORM_EPS) * nw_ref[...]).astype(BF16)
    z = _dot(h, w_ref[...])
    width = q_ref.shape[1]
    reps = width // LANES
    e, et = _seg_mats(width, HEAD_DIM)
    c = jnp.concatenate([cos_ref[...]] * reps, axis=1)
    s = jnp.concatenate([sin_ref[...]] * reps, axis=1)
    q_ref[...] = _norm_rope(z[:, 0:width], qw_ref[...], c, s, e, et, q_scale).astype(q_ref.dtype)
    k = _norm_rope(z[:, width:2 * width], kw_ref[...], c, s, e, et, 1.0)
    for j in range(reps):
        kt_ref[j * LANES:(j + 1) * LANES, :] = k[:, j * LANES:(j + 1) * LANES].T.astype(kt_ref.dtype)
    v_ref[...] = z[:, 2 * width:3 * width].astype(v_ref.dtype)


def _inproj_attn(x, nw, w_a, cos_t, sin_t, qw, kw, q_scale, tm):
    b, s, d = x.shape
    width = qw.shape[1]
    tok = lambda bi, i: (bi, i, 0)
    const = lambda bi, i: (0, 0)
    return pl.pallas_call(
        functools.partial(_inproj_attn_kernel, q_scale=q_scale),
        grid=(b, s // tm),
        in_specs=[pl.BlockSpec((None, tm, d), tok),
                  pl.BlockSpec((1, d), const),
                  pl.BlockSpec(w_a.shape, const),
                  pl.BlockSpec((tm, LANES), lambda bi, i: (i, 0)),
                  pl.BlockSpec((tm, LANES), lambda bi, i: (i, 0)),
                  pl.BlockSpec((1, width), const),
                  pl.BlockSpec((1, width), const)],
        out_specs=[pl.BlockSpec((None, tm, width), tok),
                   pl.BlockSpec((None, width, tm), lambda bi, i: (bi, 0, i)),
                   pl.BlockSpec((None, tm, width), tok)],
        out_shape=[jax.ShapeDtypeStruct((b, s, width), BF16),
                   jax.ShapeDtypeStruct((b, width, s), BF16),
                   jax.ShapeDtypeStruct((b, s, width), BF16)],
        compiler_params=_params(("parallel", "parallel")),
        name="inproj_attn",
    )(x, nw, w_a, cos_t, sin_t, qw, kw)


ATTN_ROW_CHUNK = 512


def _attn_kernel(lam_ref, q_ref, kt_ref, v_ref, sw_ref, o_ref, qs_ref, m_ref, acc_ref, s0_ref, *, tq, tk):
    q = q_ref[...]
    lane = lax.broadcasted_iota(jnp.int32, q.shape, 1)
    zero = jnp.zeros_like(q)
    qs_ref[0:tq, :] = jnp.where(lane < HEAD_DIM, q, zero)
    qs_ref[tq:2 * tq, :] = jnp.where(lane < HEAD_DIM, zero, q)
    m_ref[...] = jnp.full(m_ref.shape, -jnp.inf, F32)
    acc_ref[...] = jnp.zeros(acc_ref.shape, F32)

    rc = min(ATTN_ROW_CHUNK, 2 * tq)
    n_chunks = 2 * tq // rc
    n_tiles = kt_ref.shape[1] // tk
    s0_ref[...] = _dot(qs_ref[0:rc, :], kt_ref[:, 0:tk])

    def key_tile(j, carry):
        off = pl.multiple_of(j * tk, tk)
        v = v_ref[pl.ds(off, tk), :]
        v_ones = jnp.concatenate([v, jnp.ones_like(v)], axis=1)
        kt = kt_ref[:, pl.ds(off, tk)]
        scores = [None] * n_chunks
        scores[0] = s0_ref[...]
        for c in range(n_chunks):
            rows = slice(c * rc, (c + 1) * rc)
            if c + 1 < n_chunks:
                scores[c + 1] = _dot(qs_ref[(c + 1) * rc:(c + 2) * rc, :], kt)
            else:
                off_next = pl.multiple_of(jnp.minimum(j + 1, n_tiles - 1) * tk, tk)
                s0_ref[...] = _dot(qs_ref[0:rc, :], kt_ref[:, pl.ds(off_next, tk)])
            s = scores[c]
            m_prev = m_ref[rows, :]
            m_new = jnp.maximum(m_prev, jnp.max(s, axis=1, keepdims=True))
            alpha = jnp.exp2(m_prev - m_new)
            p = jnp.exp2((s - jnp.concatenate([m_new] * (tk // LANES), axis=1)).astype(BF16))
            acc_ref[rows, :] = (jnp.concatenate([alpha, alpha], axis=1) * acc_ref[rows, :]
                                + _dot(p, v_ones))
            m_ref[rows, :] = m_new
        return carry

    lax.fori_loop(0, n_tiles, key_tile, 0)

    lv = lam_ref[...]
    lam = (jnp.exp(jnp.sum(lv[0:1] * lv[1:2], axis=1, keepdims=True))
           - jnp.exp(jnp.sum(lv[2:3] * lv[3:4], axis=1, keepdims=True)) + LAMBDA_INIT)
    acc = acc_ref[...]
    o = (acc[0:tq, 0:LANES] / acc[0:tq, LANES:2 * LANES]
         - lam * (acc[tq:2 * tq, 0:LANES] / acc[tq:2 * tq, LANES:2 * LANES]))
    ms = jnp.mean(o * o, axis=-1, keepdims=True)
    o = o * lax.rsqrt(ms + NORM_EPS) * sw_ref[...] * (1.0 - LAMBDA_INIT)
    o_ref[...] = o.astype(o_ref.dtype)


def _attention(lam_rows, q, kt, v, subln_w, tq, tk):
    b, s, width = q.shape
    heads = width // LANES
    return pl.pallas_call(
        functools.partial(_attn_kernel, tq=tq, tk=tk),
        grid=(b, heads, s // tq),
        in_specs=[pl.BlockSpec((4, HEAD_DIM), lambda bi, h, qi: (0, 0)),
                  pl.BlockSpec((None, tq, LANES), lambda bi, h, qi: (bi, qi, h)),
                  pl.BlockSpec((None, LANES, s), lambda bi, h, qi: (bi, h, 0)),
                  pl.BlockSpec((None, s, LANES), lambda bi, h, qi: (bi, 0, h)),
                  pl.BlockSpec((1, LANES), lambda bi, h, qi: (0, 0))],
        out_specs=pl.BlockSpec((None, tq, LANES), lambda bi, h, qi: (bi, qi, h)),
        out_shape=jax.ShapeDtypeStruct((b, s, width), BF16),
        scratch_shapes=[pltpu.VMEM((2 * tq, LANES), BF16),
                        pltpu.VMEM((2 * tq, LANES), F32),
                        pltpu.VMEM((2 * tq, 2 * LANES), F32),
                        pltpu.VMEM((min(ATTN_ROW_CHUNK, 2 * tq), tk), F32)],
        compiler_params=_params(("parallel", "parallel", "arbitrary")),
        name="diff_attention",
    )(lam_rows, q, kt, v, subln_w)


HALO = 16


def _rwkv_prep_kernel(z_ref, zp_ref, zn_ref, mu_ref, w0_ref, w2_ref, a0_ref, a2_ref, g2_ref,
                      kk_w_ref, ka_w_ref, rk_ref,
                      r_ref, v_ref, kk_ref, kd0_ref, kd1_ref, ka0_ref, ka1_ref, lw0_ref, lw1_ref,
                      g_ref, bv_ref, *, c):
    i = pl.program_id(1)
    z = z_ref[...].astype(F32)
    tm = z.shape[0]
    row = lax.broadcasted_iota(jnp.int32, z.shape, 0)
    prev_row = jnp.where(i == 0, 0.0, zp_ref[HALO - 1:HALO, :].astype(F32))
    next_row = jnp.where(i == pl.num_programs(1) - 1, 0.0, zn_ref[0:1, :].astype(F32))
    z_up = jnp.where(row == 0, prev_row, pltpu.roll(z, 1, 0))
    z_dn = jnp.where(row == tm - 1, next_row, pltpu.roll(z, tm - 1, 0))
    zs = z + mu_ref[...] * (0.5 * (z_up + z_dn) - z)

    r = zs[:, 0:c]
    k = zs[:, c:2 * c]
    v = zs[:, 2 * c:3 * c]
    wl = zs[:, 3 * c:3 * c + LANES]
    al = zs[:, 3 * c + LANES:3 * c + 2 * LANES]
    gl = zs[:, 3 * c + 2 * LANES:3 * c + 3 * LANES]

    lane = lax.broadcasted_iota(jnp.int32, wl.shape, 1)
    twl = jnp.tanh(wl)
    e, et = _seg_mats(c, HEAD_DIM)

    kk = k * kk_w_ref[...]
    kk = kk * lax.rsqrt(_seg_sum(kk * kk, e, et) + 1e-12)
    kk_ref[...] = kk.astype(kk_ref.dtype)
    r_ref[...] = r.astype(r_ref.dtype)
    v_ref[...] = v.astype(v_ref.dtype)
    g_ref[...] = _dot(_sigmoid(gl).astype(BF16), g2_ref[...]).astype(g_ref.dtype)

    ksum = jnp.zeros_like(k)
    for d, (kd_ref, ka_ref, lw_ref) in enumerate(((kd0_ref, ka0_ref, lw0_ref), (kd1_ref, ka1_ref, lw1_ref))):
        sel = (lane < HEAD_DIM) if d == 0 else (lane >= HEAD_DIM)
        wpre = _dot(jnp.where(sel, twl, 0.0).astype(BF16), w2_ref[...]) + w0_ref[d:d + 1, :]
        lw_ref[...] = (-math.exp(-0.5)) * _sigmoid(wpre)
        a = _sigmoid(_dot(jnp.where(sel, al, 0.0).astype(BF16), a2_ref[...]) + a0_ref[d:d + 1, :])
        kd = k * (1.0 + (a - 1.0) * ka_w_ref[...])
        kd_ref[...] = kd.astype(kd_ref.dtype)
        ka_ref[...] = (kk * a).astype(ka_ref.dtype)
        ksum = ksum + kd
    bonus = _seg_sum(r * ksum * rk_ref[...], e, et)
    bv_ref[...] = (bonus * v).astype(bv_ref.dtype)


def _rwkv_prep(zr, mu, w0, w2, a0, a2, g2, kk_w, ka_w, rk, c, tm):
    b, s, cols = zr.shape
    nh = tm // HALO
    last = s // HALO - 1
    tok = lambda bi, i: (bi, i, 0)
    const = lambda bi, i: (0, 0)
    out = lambda dt: jax.ShapeDtypeStruct((b, s, c), dt)
    ospec = pl.BlockSpec((None, tm, c), tok)
    return pl.pallas_call(
        functools.partial(_rwkv_prep_kernel, c=c),
        grid=(b, s // tm),
        in_specs=[pl.BlockSpec((None, tm, cols), tok),
                  pl.BlockSpec((None, HALO, cols), lambda bi, i: (bi, jnp.maximum(i * nh - 1, 0), 0)),
                  pl.BlockSpec((None, HALO, cols), lambda bi, i: (bi, jnp.minimum((i + 1) * nh, last), 0)),
                  pl.BlockSpec((1, cols), const),
                  pl.BlockSpec((2, c), const),
                  pl.BlockSpec((LANES, c), const),
                  pl.BlockSpec((2, c), const),
                  pl.BlockSpec((LANES, c), const),
                  pl.BlockSpec((LANES, c), const),
                  pl.BlockSpec((1, c), const),
                  pl.BlockSpec((1, c), const),
                  pl.BlockSpec((1, c), const)],
        out_specs=[ospec] * 11,
        out_shape=[out(BF16)] * 7 + [out(F32)] * 2 + [out(BF16)] * 2,
        compiler_params=_params(("parallel", "parallel")),
        name="rwkv_prep",
    )(zr, zr, zr, mu, w0, w2, a0, a2, g2, kk_w, ka_w, rk)


def _wkv_scan_kernel(lwf_ref, rf_ref, vf_ref, kkf_ref, kaf_ref, kdf_ref,
                     lwb_ref, rb_ref, vb_ref, kkb_ref, kab_ref, kdb_ref,
                     yf_ref, yb_ref, st_ref):
    L = SCAN_CHUNK
    P2 = 2 * L
    ci = pl.program_id(1)

    @pl.when(ci == 0)
    def _():
        st_ref[...] = jnp.zeros(st_ref.shape, F32)

    row = lax.broadcasted_iota(jnp.int32, (P2, P2), 0)
    col = lax.broadcasted_iota(jnp.int32, (P2, P2), 1)
    same_head = (row // L) == (col // L)
    rt = row % L
    ct = col % L
    eye = row == col
    ti = lax.broadcasted_iota(jnp.int32, (L, L), 0)
    tj = lax.broadcasted_iota(jnp.int32, (L, L), 1)
    lane_a = lax.broadcasted_iota(jnp.int32, (L, LANES), 1) < HEAD_DIM

    def expand(x):
        return jnp.concatenate([jnp.where(lane_a, x, 0.0), jnp.where(lane_a, 0.0, x)], axis=0)

    units = []
    dirs = ((lwf_ref, rf_ref, vf_ref, kkf_ref, kaf_ref, kdf_ref, yf_ref),
            (lwb_ref, rb_ref, vb_ref, kkb_ref, kab_ref, kdb_ref, yb_ref))
    for d, (lw_ref, r_ref, v_ref, kk_ref, ka_ref, kd_ref, y_ref) in enumerate(dirs):
        if d == 0:
            m_strict = same_head & (rt > ct)
            m_incl = same_head & (rt >= ct)
            tri = jnp.where(tj <= ti, 1.0, 0.0).astype(BF16)
        else:
            m_strict = same_head & (rt < ct)
            m_incl = same_head & (rt <= ct)
            tri = jnp.where(tj >= ti, 1.0, 0.0).astype(BF16)
        m_level = []
        for lvl in range(L.bit_length() - 1):
            joined = same_head & ((rt >> (lvl + 1)) == (ct >> (lvl + 1)))
            r_bit = (rt >> lvl) & 1
            c_bit = (ct >> lvl) & 1
            later_row = (r_bit == 1) & (c_bit == 0) if d == 0 else (r_bit == 0) & (c_bit == 1)
            m_level.append(joined & later_row)
        lw = lw_ref[...]
        hi = lw.astype(BF16)
        r1 = lw - hi.astype(F32)
        mid = r1.astype(BF16)
        lo = (r1 - mid.astype(F32)).astype(BF16)
        cum = _dot(tri, hi) + _dot(tri, mid) + _dot(tri, lo)
        cum_last = cum[L - 1:L, :] if d == 0 else cum[0:1, :]
        ec = jnp.exp(cum)
        enc = jnp.exp(-cum)
        p_last = jnp.exp(cum_last)
        a_t = -kk_ref[...].astype(F32) * jnp.exp(cum - lw)
        r_t = r_ref[...].astype(F32) * ec
        b_t = ka_ref[...].astype(F32) * enc
        k_t = kd_ref[...].astype(F32) * enc
        b_h = b_t * p_last
        k_h = k_t * p_last
        vv = v_ref[...].astype(F32)

        for p in range(lw.shape[1] // LANES):
            sl = slice(p * LANES, (p + 1) * LANES)
            units.append(dict(
                d=d, p=p, sl=sl, y_ref=y_ref, m_strict=m_strict, m_incl=m_incl, m_level=m_level,
                ea=expand(a_t[:, sl]), er=expand(r_t[:, sl]), vexp=expand(vv[:, sl]),
                bk2=jnp.concatenate([b_t[:, sl]] * 2 + [k_t[:, sl]] * 2, axis=0).astype(BF16),
                bh=b_h[:, sl], kh=k_h[:, sl], p_last=p_last[:, sl]))

    for u in units:
        g = _dot_nt(jnp.concatenate([u["ea"], u["er"]], axis=0).astype(BF16), u["bk2"])
        u["xab"] = jnp.where(u["m_strict"], g[0:P2, 0:P2], 0.0)
        u["xak"] = jnp.where(u["m_strict"], g[0:P2, P2:2 * P2], 0.0).astype(BF16)
        u["xr"] = jnp.concatenate([jnp.where(u["m_incl"], g[P2:2 * P2, 0:P2], 0.0),
                                   jnp.where(u["m_incl"], g[P2:2 * P2, P2:2 * P2], 0.0)], axis=1).astype(BF16)
        u["tinv"] = jnp.where(eye, 1.0, jnp.where(u["m_level"][0], u["xab"], 0.0))
    for lvl in range(1, L.bit_length() - 1):
        for u in units:
            xoff = jnp.where(u["m_level"][lvl], u["xab"], 0.0).astype(BF16)
            u["tx"] = _dot(u["tinv"].astype(BF16), xoff).astype(BF16)
        for u in units:
            u["tinv"] = u["tinv"] + _dot(u["tx"], u["tinv"].astype(BF16))
    for u in units:
        u["xv"] = _dot(u["xak"], u["vexp"].astype(BF16))
    for u in units:
        u["wu0"] = _dot(u["tinv"].astype(BF16), jnp.concatenate([u["ea"], u["xv"]], axis=1).astype(BF16))
        u["bkt"] = jnp.concatenate([expand(u["bh"]).T, expand(u["kh"]).T], axis=1).astype(BF16)
        u["p_col"] = jnp.broadcast_to(u["p_last"], (P2, LANES)).T
    for u in units:
        u["st"] = st_ref[u["d"], u["p"]]
        u["ws"] = _dot(jnp.concatenate([u["wu0"][:, 0:LANES], u["er"]], axis=0).astype(BF16), u["st"].astype(BF16))
    for u in units:
        uu = u["ws"][0:P2] + u["wu0"][:, LANES:2 * LANES]
        u["uv"] = jnp.concatenate([uu, u["vexp"]], axis=0).astype(BF16)
    for u in units:
        yexp = u["ws"][P2:2 * P2] + _dot(u["xr"], u["uv"])
        u["y_ref"][:, u["sl"]] = (yexp[0:L] + yexp[L:P2]).astype(u["y_ref"].dtype)
    for u in units:
        st_ref[u["d"], u["p"]] = u["p_col"] * u["st"] + _dot(u["bkt"], u["uv"])


def _wkv_scan(lw0, lw1, r, v, kk, ka0, ka1, kd0, kd1):
    b, s, c = r.shape
    L = SCAN_CHUNK
    nc = s // L
    fwd = lambda bi, i: (bi, i, 0)
    bwd = lambda bi, i: (bi, nc - 1 - i, 0)
    fs = pl.BlockSpec((None, L, c), fwd)
    bs = pl.BlockSpec((None, L, c), bwd)
    return pl.pallas_call(
        _wkv_scan_kernel,
        grid=(b, nc),
        in_specs=[fs] * 6 + [bs] * 6,
        out_specs=[fs, bs],
        out_shape=[jax.ShapeDtypeStruct((b, s, c), F32)] * 2,
        scratch_shapes=[pltpu.VMEM((2, c // LANES, LANES, LANES), F32)],
        compiler_params=_params(("parallel", "arbitrary")),
        name="wkv_scan",
    )(lw0, r, v, kk, ka0, kd0, lw1, r, v, kk, ka1, kd1)


def _post_kernel(x_ref, oa_ref, yf_ref, yb_ref, g_ref, bv_ref, zg_ref, lnw_ref, lnb_ref,
                 woa_ref, wor_ref, wout_ref, n2w_ref, wq_ref, x1_ref, xn_ref, qp_ref):
    y = yf_ref[...] + yb_ref[...]
    c = y.shape[1]
    e, et = _seg_mats(c, HEAD_DIM)
    mu = _seg_sum(y, e, et) * (1.0 / HEAD_DIM)
    dlt = y - mu
    var = _seg_sum(dlt * dlt, e, et) * (1.0 / HEAD_DIM)
    yn = dlt * lax.rsqrt(var + RWKV_LN_EPS) * lnw_ref[...] + lnb_ref[...]
    out = ((yn + bv_ref[...].astype(F32)) * g_ref[...].astype(F32)).astype(BF16)
    h_rwkv = _dot(out, wor_ref[...])
    h_attn = _dot(oa_ref[...], woa_ref[...])
    zg = zg_ref[...].astype(F32)
    merged = _sigmoid(zg[:, 0:c]) * h_attn + _sigmoid(zg[:, c:2 * c]) * h_rwkv
    x1 = x_ref[...] + _dot(merged.astype(BF16), wout_ref[...])
    x1_ref[...] = x1
    ms = jnp.mean(x1 * x1, axis=-1, keepdims=True)
    xn = (x1 * lax.rsqrt(ms + NORM_EPS) * n2w_ref[...]).astype(BF16)
    xn_ref[...] = xn
    qp = _dot(xn, wq_ref[...]).astype(BF16)
    for j in range(qp_ref.shape[0]):
        qp_ref[j] = qp[:, j * LANES:(j + 1) * LANES]


def _post(x, oa, yf, yb, g, bv, zg, lnw, lnb, woa, wor, wout, n2w, wq, tm):
    t, d = x.shape
    nq = wq.shape[1] // LANES
    tok = lambda i: (i, 0)
    const = lambda i: (0, 0)
    tspec = lambda w: pl.BlockSpec((tm, w), tok)
    cspec = lambda a: pl.BlockSpec(a.shape, const)
    return pl.pallas_call(
        _post_kernel,
        grid=(t // tm,),
        in_specs=[tspec(d), tspec(d), tspec(d), tspec(d), tspec(d), tspec(d), tspec(2 * d),
                  cspec(lnw), cspec(lnb), cspec(woa), cspec(wor), cspec(wout), cspec(n2w), cspec(wq)],
        out_specs=[tspec(d), tspec(d), pl.BlockSpec((nq, tm, LANES), lambda i: (0, i, 0))],
        out_shape=[jax.ShapeDtypeStruct((t, d), F32), jax.ShapeDtypeStruct((t, d), BF16),
                   jax.ShapeDtypeStruct((nq, t, LANES), BF16)],
        compiler_params=_params(("parallel",)),
        name="post_merge",
    )(x, oa, yf, yb, g, bv, zg, lnw, lnb, woa, wor, wout, n2w, wq)


SUBLANES = 8


def _sort_pairs(n):
    def merge(lo, hi, r):
        step = r * 2
        if step < hi - lo:
            yield from merge(lo, hi, step)
            yield from merge(lo + r, hi, step)
            for i in range(lo + r, hi - r, step):
                yield (i, i + r)
        else:
            yield (lo, lo + r)

    def sort(lo, hi):
        if hi - lo >= 1:
            mid = lo + (hi - lo) // 2
            yield from sort(lo, mid)
            yield from sort(mid + 1, hi)
            yield from merge(lo, hi, 1)

    return list(sort(0, n - 1))


def _exchange(xs, i, j):
    a, b = xs[i], xs[j]
    if b is None:
        return
    if a is None:
        xs[i], xs[j] = b, None
        return
    xs[i], xs[j] = jnp.maximum(a, b), jnp.minimum(a, b)


def _top_sorted(blocks, k):
    xs = list(blocks)
    for i, j in _sort_pairs(k):
        _exchange(xs, i, j)
    shift = SUBLANES // 2
    while shift >= 1:
        ys = [None if x is None else pltpu.roll(x, shift, 0) for x in xs]
        zs = []
        for i in range(k):
            a, b = xs[i], ys[k - 1 - i]
            zs.append(b if a is None else (a if b is None else jnp.maximum(a, b)))
        step = k // 2
        while step >= 1:
            for i in range(k):
                if i & step == 0:
                    _exchange(zs, i, i + step)
            step //= 2
        xs = zs
        shift //= 2
    return [x[0:1, :] for x in xs]


def _row_blocks(x):
    return [x[r:r + SUBLANES, :] for r in range(0, x.shape[0], SUBLANES)]


def _peer_select(s0, s1, k):
    a = _top_sorted(_row_blocks(s0), k)
    b = _top_sorted(_row_blocks(s1), k)
    amat = jnp.concatenate(a, axis=0)
    bmat = jnp.concatenate(b, axis=0)
    h8 = k // 2
    cand = ([a[0] + bmat[0:h8], a[0] + bmat[h8:k], a[1] + bmat[0:h8], amat[h8:k] + b[0]]
            + [a[i] + bmat[0:h8] for i in range(2, h8)])
    best = _top_sorted(cand + [None] * (k - len(cand)), k)
    z = jnp.zeros_like(best[0])
    for val in best:
        z = z + jnp.exp(val - best[0])
    return bmat, a[0], best[k - 1], z


def _peer_kernel(xn_ref, qp_ref, sk_ref, u_ref, vt_ref, x1_ref, o_ref,
                 cnt_ref, e0_ref, rank_ref, e1_ref, acc_ref):
    jb = pl.program_id(1)
    tt = xn_ref.shape[0]
    K = PEER_TOPK
    rows_per_blk = u_ref.shape[0] // PEER_NKEYS

    @pl.when(jb == 0)
    def _():
        acc_ref[...] = jnp.zeros(acc_ref.shape, F32)

        def head_body(h, carry):
            s0 = _dot_nt(sk_ref[2 * h], qp_ref[2 * h])
            s1 = _dot_nt(sk_ref[2 * h + 1], qp_ref[2 * h + 1])

            bmat, a0, thr, z = _peer_select(s0, s1, K)
            cnt = jnp.zeros(s0.shape, F32)
            rank = jnp.full(s1.shape, float(PEER_NKEYS), F32)
            for kk in range(K):
                cnt = cnt + jnp.where(s0 + bmat[kk:kk + 1] >= thr, 1.0, 0.0)
            for kk in reversed(range(K)):
                rank = jnp.where(s1 == bmat[kk:kk + 1], float(kk + 1), rank)
            cnt_ref[h] = cnt
            e0_ref[h] = jnp.exp(s0 - a0) / z
            rank_ref[h] = rank.astype(BF16)
            e1_ref[h] = jnp.exp(s1 - bmat[0:1]).astype(BF16)
            return carry

        lax.fori_loop(0, PEER_HEADS, head_body, 0)

    sub = PEER_SUB_ROWS * PEER_NKEYS
    xn = xn_ref[...]
    n_sub = u_ref.shape[0] // sub
    hpres = [None] * n_sub
    hpres[0] = _dot_nt(u_ref[0:sub, :], xn)
    zero = jnp.zeros((), BF16)
    for c in range(n_sub):
        if c + 1 < n_sub:
            hpres[c + 1] = _dot_nt(u_ref[(c + 1) * sub:(c + 2) * sub, :], xn)
        g_rows = []
        for il in range(c * PEER_SUB_ROWS, (c + 1) * PEER_SUB_ROWS):
            i = jb * rows_per_blk + il
            acc = None
            for h in range(PEER_HEADS):
                crow = cnt_ref[h, pl.ds(i, 1), :].astype(BF16)
                e0row = e0_ref[h, pl.ds(i, 1), :].astype(BF16)
                contrib = jnp.where(rank_ref[h] <= crow, e0row * e1_ref[h], zero)
                acc = contrib if acc is None else acc + contrib
            g_rows.append(acc)
        gates = jnp.concatenate(g_rows, axis=0)
        hpre = hpres[c]
        gelu = 0.5 * hpre * (1.0 + lax.erf(hpre * (2.0 ** -0.5)))
        hg = gelu.astype(BF16) * gates
        acc_ref[...] += _dot(vt_ref[:, c * sub:(c + 1) * sub], hg)

    @pl.when(jb == pl.num_programs(1) - 1)
    def _():
        o_ref[...] = x1_ref[...] + acc_ref[...].T


def _peer(xn, qp, sk, u, vt, x1, tt, eb):
    t, d = xn.shape
    n_exp = u.shape[0]
    nq = qp.shape[0]
    return pl.pallas_call(
        _peer_kernel,
        grid=(t // tt, n_exp // eb),
        in_specs=[pl.BlockSpec((tt, d), lambda i, j: (i, 0)),
                  pl.BlockSpec((nq, tt, LANES), lambda i, j: (0, i, 0)),
                  pl.BlockSpec(sk.shape, lambda i, j: (0, 0, 0)),
                  pl.BlockSpec((eb, d), lambda i, j: (j, 0)),
                  pl.BlockSpec((d, eb), lambda i, j: (0, j)),
                  pl.BlockSpec((tt, d), lambda i, j: (i, 0))],
        out_specs=pl.BlockSpec((tt, d), lambda i, j: (i, 0)),
        out_shape=jax.ShapeDtypeStruct((t, d), F32),
        scratch_shapes=[pltpu.VMEM((PEER_HEADS, PEER_NKEYS, tt), F32),
                        pltpu.VMEM((PEER_HEADS, PEER_NKEYS, tt), F32),
                        pltpu.VMEM((PEER_HEADS, PEER_NKEYS, tt), BF16),
                        pltpu.VMEM((PEER_HEADS, PEER_NKEYS, tt), BF16),
                        pltpu.VMEM((d, tt), F32)],
        compiler_params=_params(("parallel", "arbitrary")),
        name="peer",
    )(xn, qp, sk, u, vt, x1)


def _rope_tables(s):
    inv = 1.0 / (ROPE_THETA ** (jnp.arange(0, HEAD_DIM, 2, dtype=F32) / HEAD_DIM))
    ang = jnp.arange(s, dtype=F32)[:, None] * inv[None, :]
    cos, sin = jnp.cos(ang), jnp.sin(ang)
    reps = LANES // HEAD_DIM
    cos_t = jnp.tile(jnp.concatenate([cos, cos], axis=1), (1, reps))
    sin_t = jnp.tile(jnp.concatenate([-sin, sin], axis=1), (1, reps))
    return cos_t, sin_t


def _tile(n, pref):
    return pref if n % pref == 0 else n


def _trunk(x, P):
    b, s, d = x.shape
    t = b * s
    c = d
    xf = x.reshape(t, d)
    zg = _norm_matmul(xf, P["norm1_w"], P["w_g"], _tile(t, 512))

    cos_t, sin_t = _rope_tables(s)
    q, kt, va = _inproj_attn(x, P["norm1_w"], P["w_a"], cos_t, sin_t, P["q_norm_w"], P["k_norm_w"],
                             HEAD_DIM ** -0.5 * math.log2(math.e), _tile(s, 512))
    oa = _attention(P["lam_rows"], q, kt, va, P["subln_w"], _tile(s, 1024), _tile(s, 1024))

    zr = _norm_matmul(xf, P["norm1_w"], P["w_r"], _tile(t, 512)).reshape(b, s, -1)
    (r, v, kk, kd0, kd1, ka0, ka1, lw0, lw1, g, bv) = _rwkv_prep(
        zr, P["shift_mu"], P["decay_w0"], P["decay_w2"], P["iclr_a0"], P["iclr_a2"],
        P["gate_g2"], P["k_k"], P["k_a"], P["r_k"], c, _tile(s, 256))
    yf, yb = _wkv_scan(lw0, lw1, r, v, kk, ka0, ka1, kd0, kd1)

    flat = lambda a: a.reshape(t, a.shape[-1])
    x1, xn, qp = _post(xf, flat(oa), flat(yf), flat(yb), flat(g), flat(bv), zg, P["lnx_w"], P["lnx_b"],
                       P["w_o_attn"], P["w_o_rwkv"], P["w_out"], P["norm2_w"], P["peer_wq"], _tile(t, 256))
    y = _peer(xn, qp, P["peer_subkeys"], P["peer_u"], P["peer_vt"], x1, _tile(t, 512), 1024)
    return y.reshape(b, s, d)


def kernel(x_prompt, x_sample, norm1_w, w_in, q_norm_w, k_norm_w, lambda_q1, lambda_k1, lambda_q2, lambda_k2, subln_w, w_o_attn, shift_mu, decay_w0, decay_w2, iclr_a0, iclr_a2, gate_g2, k_k, k_a, r_k, lnx_w, lnx_b, w_o_rwkv, w_out, norm2_w, peer_wq, peer_subkeys, peer_u, peer_v):
    d = x_prompt.shape[-1]
    c = d
    att_cols = 3 * d
    rwkv_cols = shift_mu.shape[-1]
    l = 0
    w = w_in[l].astype(BF16)
    nsub = d // HEAD_DIM
    P = {
        "norm1_w": norm1_w[l][None, :],
        "w_a": w[:, :att_cols],
        "w_r": w[:, att_cols:att_cols + rwkv_cols],
        "w_g": w[:, att_cols + rwkv_cols:],
        "q_norm_w": jnp.tile(q_norm_w[l], nsub)[None, :],
        "k_norm_w": jnp.tile(k_norm_w[l], nsub)[None, :],
        "lam_rows": jnp.stack([lambda_q1[l], lambda_k1[l], lambda_q2[l], lambda_k2[l]]),
        "subln_w": subln_w[l][None, :],
        "w_o_attn": w_o_attn[l].astype(BF16),
        "shift_mu": shift_mu[l][None, :],
        "decay_w0": decay_w0[l],
        "decay_w2": decay_w2[l].reshape(-1, c).astype(BF16),
        "iclr_a0": iclr_a0[l],
        "iclr_a2": iclr_a2[l].reshape(-1, c).astype(BF16),
        "gate_g2": gate_g2[l].astype(BF16),
        "k_k": k_k[l][None, :],
        "k_a": k_a[l][None, :],
        "r_k": r_k[l].reshape(1, c),
        "lnx_w": lnx_w[l][None, :],
        "lnx_b": lnx_b[l][None, :],
        "w_o_rwkv": w_o_rwkv[l].astype(BF16),
        "w_out": w_out[l].astype(BF16),
        "norm2_w": norm2_w[l][None, :],
        "peer_wq": peer_wq[l].astype(BF16),
        "peer_subkeys": peer_subkeys[l].reshape(-1, PEER_NKEYS, peer_subkeys.shape[-1]).astype(BF16),
        "peer_u": peer_u[l].astype(BF16),
        "peer_vt": peer_v[l].astype(BF16).T,
    }
    return (_trunk(x_prompt, P), _trunk(x_sample, P))
```

```python
import functools
import math

import jax
import jax.numpy as jnp
from jax import lax
from jax.experimental import pallas as pl
from jax.experimental.pallas import tpu as pltpu

F32 = jnp.float32
BF16 = jnp.bfloat16

LANES = 128
HEAD_DIM = 64
ATT_HEADS = 8
NORM_EPS = 1e-6
RWKV_LN_EPS = 64e-5
ROPE_THETA = 10000.0
LAMBDA_INIT = 0.8 - 0.6 * math.exp(-0.3 * 0)
SCAN_CHUNK = 64
PEER_NKEYS = 128
PEER_TOPK = 16
PEER_HEADS = 8
PEER_SUB_ROWS = 2
VMEM_LIMIT = 56 * 1024 * 1024


def _params(sem):
    return pltpu.CompilerParams(dimension_semantics=sem, vmem_limit_bytes=VMEM_LIMIT)


def _dot(a, b):
    return jnp.dot(a, b, preferred_element_type=F32)


def _dot_nt(a, b):
    return lax.dot_general(a, b, (((1,), (1,)), ((), ())), preferred_element_type=F32)


def _split_dot(x, m):
    hi = x.astype(BF16)
    lo = (x - hi.astype(F32)).astype(BF16)
    return _dot(hi, m) + _dot(lo, m)


def _seg_mats(width, seg):
    lane = lax.broadcasted_iota(jnp.int32, (width, LANES), 0)
    grp = lax.broadcasted_iota(jnp.int32, (width, LANES), 1)
    e = jnp.where(lane // seg == grp, 1.0, 0.0).astype(BF16)
    grp_t = lax.broadcasted_iota(jnp.int32, (LANES, width), 0)
    lane_t = lax.broadcasted_iota(jnp.int32, (LANES, width), 1)
    et = jnp.where(lane_t // seg == grp_t, 1.0, 0.0).astype(BF16)
    return e, et


def _seg_sum(x, e, et):
    return _split_dot(_split_dot(x, e), et)


def _sigmoid(x):
    return 1.0 / (1.0 + jnp.exp(-x))


def _norm_matmul_kernel(x_ref, nw_ref, w_ref, o_ref):
    x = x_ref[...]
    ms = jnp.mean(x * x, axis=-1, keepdims=True)
    h = (x * lax.rsqrt(ms + NORM_EPS) * nw_ref[...]).astype(BF16)
    o_ref[...] = _dot(h, w_ref[...]).astype(o_ref.dtype)


def _norm_matmul(x, nw, w, tm):
    t, d = x.shape
    n = w.shape[1]
    return pl.pallas_call(
        _norm_matmul_kernel,
        grid=(t // tm,),
        in_specs=[pl.BlockSpec((tm, d), lambda i: (i, 0)),
                  pl.BlockSpec((1, d), lambda i: (0, 0)),
                  pl.BlockSpec((d, n), lambda i: (0, 0))],
        out_specs=pl.BlockSpec((tm, n), lambda i: (i, 0)),
        out_shape=jax.ShapeDtypeStruct((t, n), BF16),
        compiler_params=_params(("parallel",)),
        name="norm_matmul",
    )(x, nw, w)


def _norm_rope(x, w, c, s, e, et, scale):
    width = x.shape[1]
    ms = _seg_sum(x * x, e, et) * (1.0 / HEAD_DIM)
    y = x * lax.rsqrt(ms + NORM_EPS) * w
    lane = lax.broadcasted_iota(jnp.int32, y.shape, 1)
    half = HEAD_DIM // 2
    partner = jnp.where(lane % HEAD_DIM < half, pltpu.roll(y, width - half, 1), pltpu.roll(y, half, 1))
    return (y * c + partner * s) * scale


def _inproj_attn_kernel(x_ref, nw_ref, w_ref, cos_ref, sin_ref, qw_ref, kw_ref, q_ref, kt_ref, v_ref, *, q_scale):
    x = x_ref[...]
    ms = jnp.mean(x * x, axis=-1, keepdims=True)
    h = (x * lax.rsqrt(ms + NORM_EPS) * nw_ref[...]).astype(BF16)
    z = _dot(h, w_ref[...])
    width = q_ref.shape[1]
    reps = width // LANES
    e, et = _seg_mats(width, HEAD_DIM)
    c = jnp.concatenate([cos_ref[...]] * reps, axis=1)
    s = jnp.concatenate([sin_ref[...]] * reps, axis=1)
    q_ref[...] = _norm_rope(z[:, 0:width], qw_ref[...], c, s, e, et, q_scale).astype(q_ref.dtype)
    k = _norm_rope(z[:, width:2 * width], kw_ref[...], c, s, e, et, 1.0)
    for j in range(reps):
        kt_ref[j * LANES:(j + 1) * LANES, :] = k[:, j * LANES:(j + 1) * LANES].T.astype(kt_ref.dtype)
    v_ref[...] = z[:, 2 * width:3 * width].astype(v_ref.dtype)


def _inproj_attn(x, nw, w_a, cos_t, sin_t, qw, kw, q_scale, tm):
    b, s, d = x.shape
    width = qw.shape[1]
    tok = lambda bi, i: (bi, i, 0)
    const = lambda bi, i: (0, 0)
    return pl.pallas_call(
        functools.partial(_inproj_attn_kernel, q_scale=q_scale),
        grid=(b, s // tm),
        in_specs=[pl.BlockSpec((None, tm, d), tok),
                  pl.BlockSpec((1, d), const),
                  pl.BlockSpec(w_a.shape, const),
                  pl.BlockSpec((tm, LANES), lambda bi, i: (i, 0)),
                  pl.BlockSpec((tm, LANES), lambda bi, i: (i, 0)),
                  pl.BlockSpec((1, width), const),
                  pl.BlockSpec((1, width), const)],
        out_specs=[pl.BlockSpec((None, tm, width), tok),
                   pl.BlockSpec((None, width, tm), lambda bi, i: (bi, 0, i)),
                   pl.BlockSpec((None, tm, width), tok)],
        out_shape=[jax.ShapeDtypeStruct((b, s, width), BF16),
                   jax.ShapeDtypeStruct((b, width, s), BF16),
                   jax.ShapeDtypeStruct((b, s, width), BF16)],
        compiler_params=_params(("parallel", "parallel")),
        name="inproj_attn",
    )(x, nw, w_a, cos_t, sin_t, qw, kw)


ATTN_ROW_CHUNK = 512


def _attn_kernel(lam_ref, q_ref, kt_ref, v_ref, sw_ref, o_ref, qs_ref, m_ref, acc_ref, s0_ref, *, tq, tk):
    q = q_ref[...]
    lane = lax.broadcasted_iota(jnp.int32, q.shape, 1)
    zero = jnp.zeros_like(q)
    qs_ref[0:tq, :] = jnp.where(lane < HEAD_DIM, q, zero)
    qs_ref[tq:2 * tq, :] = jnp.where(lane < HEAD_DIM, zero, q)
    m_ref[...] = jnp.full(m_ref.shape, -jnp.inf, F32)
    acc_ref[...] = jnp.zeros(acc_ref.shape, F32)

    rc = min(ATTN_ROW_CHUNK, 2 * tq)
    n_chunks = 2 * tq // rc
    n_tiles = kt_ref.shape[1] // tk
    s0_ref[...] = _dot(qs_ref[0:rc, :], kt_ref[:, 0:tk])

    def key_tile(j, carry):
        off = pl.multiple_of(j * tk, tk)
        v = v_ref[pl.ds(off, tk), :]
        v_ones = jnp.concatenate([v, jnp.ones_like(v)], axis=1)
        kt = kt_ref[:, pl.ds(off, tk)]
        scores = [None] * n_chunks
        scores[0] = s0_ref[...]
        for c in range(n_chunks):
            rows = slice(c * rc, (c + 1) * rc)
            if c + 1 < n_chunks:
                scores[c + 1] = _dot(qs_ref[(c + 1) * rc:(c + 2) * rc, :], kt)
            else:
                off_next = pl.multiple_of(jnp.minimum(j + 1, n_tiles - 1) * tk, tk)
                s0_ref[...] = _dot(qs_ref[0:rc, :], kt_ref[:, pl.ds(off_next, tk)])
            s = scores[c]
            m_prev = m_ref[rows, :]
            m_new = jnp.maximum(m_prev, jnp.max(s, axis=1, keepdims=True))
            alpha = jnp.exp2(m_prev - m_new)
            p = jnp.exp2((s - jnp.concatenate([m_new] * (tk // LANES), axis=1)).astype(BF16))
            acc_ref[rows, :] = (jnp.concatenate([alpha, alpha], axis=1) * acc_ref[rows, :]
                                + _dot(p, v_ones))
            m_ref[rows, :] = m_new
        return carry

    lax.fori_loop(0, n_tiles, key_tile, 0)

    lv = lam_ref[...]
    lam = (jnp.exp(jnp.sum(lv[0:1] * lv[1:2], axis=1, keepdims=True))
           - jnp.exp(jnp.sum(lv[2:3] * lv[3:4], axis=1, keepdims=True)) + LAMBDA_INIT)
    acc = acc_ref[...]
    o = (acc[0:tq, 0:LANES] / acc[0:tq, LANES:2 * LANES]
         - lam * (acc[tq:2 * tq, 0:LANES] / acc[tq:2 * tq, LANES:2 * LANES]))
    ms = jnp.mean(o * o, axis=-1, keepdims=True)
    o = o * lax.rsqrt(ms + NORM_EPS) * sw_ref[...] * (1.0 - LAMBDA_INIT)
    o_ref[...] = o.astype(o_ref.dtype)


def _attention(lam_rows, q, kt, v, subln_w, tq, tk):
    b, s, width = q.shape
    heads = width // LANES
    return pl.pallas_call(
        functools.partial(_attn_kernel, tq=tq, tk=tk),
        grid=(b, heads, s // tq),
        in_specs=[pl.BlockSpec((4, HEAD_DIM), lambda bi, h, qi: (0, 0)),
                  pl.BlockSpec((None, tq, LANES), lambda bi, h, qi: (bi, qi, h)),
                  pl.BlockSpec((None, LANES, s), lambda bi, h, qi: (bi, h, 0)),
                  pl.BlockSpec((None, s, LANES), lambda bi, h, qi: (bi, 0, h)),
                  pl.BlockSpec((1, LANES), lambda bi, h, qi: (0, 0))],
        out_specs=pl.BlockSpec((None, tq, LANES), lambda bi, h, qi: (bi, qi, h)),
        out_shape=jax.ShapeDtypeStruct((b, s, width), BF16),
        scratch_shapes=[pltpu.VMEM((2 * tq, LANES), BF16),
                        pltpu.VMEM((2 * tq, LANES), F32),
                        pltpu.VMEM((2 * tq, 2 * LANES), F32),
                        pltpu.VMEM((min(ATTN_ROW_CHUNK, 2 * tq), tk), F32)],
        compiler_params=_params(("parallel", "parallel", "arbitrary")),
        name="diff_attention",
    )(lam_rows, q, kt, v, subln_w)


HALO = 16


def _rwkv_prep_kernel(z_ref, zp_ref, zn_ref, mu_ref, w0_ref, w2_ref, a0_ref, a2_ref, g2_ref,
                      kk_w_ref, ka_w_ref, rk_ref,
                      r_ref, v_ref, kk_ref, kd0_ref, kd1_ref, ka0_ref, ka1_ref, lw0_ref, lw1_ref,
                      g_ref, bv_ref, *, c):
    i = pl.program_id(1)
    z = z_ref[...].astype(F32)
    tm = z.shape[0]
    row = lax.broadcasted_iota(jnp.int32, z.shape, 0)
    prev_row = jnp.where(i == 0, 0.0, zp_ref[HALO - 1:HALO, :].astype(F32))
    next_row = jnp.where(i == pl.num_programs(1) - 1, 0.0, zn_ref[0:1, :].astype(F32))
    z_up = jnp.where(row == 0, prev_row, pltpu.roll(z, 1, 0))
    z_dn = jnp.where(row == tm - 1, next_row, pltpu.roll(z, tm - 1, 0))
    zs = z + mu_ref[...] * (0.5 * (z_up + z_dn) - z)

    r = zs[:, 0:c]
    k = zs[:, c:2 * c]
    v = zs[:, 2 * c:3 * c]
    wl = zs[:, 3 * c:3 * c + LANES]
    al = zs[:, 3 * c + LANES:3 * c + 2 * LANES]
    gl = zs[:, 3 * c + 2 * LANES:3 * c + 3 * LANES]

    lane = lax.broadcasted_iota(jnp.int32, wl.shape, 1)
    twl = jnp.tanh(wl)
    e, et = _seg_mats(c, HEAD_DIM)

    kk = k * kk_w_ref[...]
    kk = kk * lax.rsqrt(_seg_sum(kk * kk, e, et) + 1e-12)
    kk_ref[...] = kk.astype(kk_ref.dtype)
    r_ref[...] = r.astype(r_ref.dtype)
    v_ref[...] = v.astype(v_ref.dtype)
    g_ref[...] = _dot(_sigmoid(gl).astype(BF16), g2_ref[...]).astype(g_ref.dtype)

    ksum = jnp.zeros_like(k)
    for d, (kd_ref, ka_ref, lw_ref) in enumerate(((kd0_ref, ka0_ref, lw0_ref), (kd1_ref, ka1_ref, lw1_ref))):
        sel = (lane < HEAD_DIM) if d == 0 else (lane >= HEAD_DIM)
        wpre = _dot(jnp.where(sel, twl, 0.0).astype(BF16), w2_ref[...]) + w0_ref[d:d + 1, :]
        lw_ref[...] = (-math.exp(-0.5)) * _sigmoid(wpre)
        a = _sigmoid(_dot(jnp.where(sel, al, 0.0).astype(BF16), a2_ref[...]) + a0_ref[d:d + 1, :])
        kd = k * (1.0 + (a - 1.0) * ka_w_ref[...])
        kd_ref[...] = kd.astype(kd_ref.dtype)
        ka_ref[...] = (kk * a).astype(ka_ref.dtype)
        ksum = ksum + kd
    bonus = _seg_sum(r * ksum * rk_ref[...], e, et)
    bv_ref[...] = (bonus * v).astype(bv_ref.dtype)


def _rwkv_prep(zr, mu, w0, w2, a0, a2, g2, kk_w, ka_w, rk, c, tm):
    b, s, cols = zr.shape
    nh = tm // HALO
    last = s // HALO - 1
    tok = lambda bi, i: (bi, i, 0)
    const = lambda bi, i: (0, 0)
    out = lambda dt: jax.ShapeDtypeStruct((b, s, c), dt)
    ospec = pl.BlockSpec((None, tm, c), tok)
    return pl.pallas_call(
        functools.partial(_rwkv_prep_kernel, c=c),
        grid=(b, s // tm),
        in_specs=[pl.BlockSpec((None, tm, cols), tok),
                  pl.BlockSpec((None, HALO, cols), lambda bi, i: (bi, jnp.maximum(i * nh - 1, 0), 0)),
                  pl.BlockSpec((None, HALO, cols), lambda bi, i: (bi, jnp.minimum((i + 1) * nh, last), 0)),
                  pl.BlockSpec((1, cols), const),
                  pl.BlockSpec((2, c), const),
                  pl.BlockSpec((LANES, c), const),
                  pl.BlockSpec((2, c), const),
                  pl.BlockSpec((LANES, c), const),
                  pl.BlockSpec((LANES, c), const),
                  pl.BlockSpec((1, c), const),
                  pl.BlockSpec((1, c), const),
                  pl.BlockSpec((1, c), const)],
        out_specs=[ospec] * 11,
        out_shape=[out(BF16)] * 7 + [out(F32)] * 2 + [out(BF16)] * 2,
        compiler_params=_params(("parallel", "parallel")),
        name="rwkv_prep",
    )(zr, zr, zr, mu, w0, w2, a0, a2, g2, kk_w, ka_w, rk)


def _wkv_scan_kernel(lwf_ref, rf_ref, vf_ref, kkf_ref, kaf_ref, kdf_ref,
                     lwb_ref, rb_ref, vb_ref, kkb_ref, kab_ref, kdb_ref,
                     yf_ref, yb_ref, st_ref):
    L = SCAN_CHUNK
    P2 = 2 * L
    ci = pl.program_id(1)

    @pl.when(ci == 0)
    def _():
        st_ref[...] = jnp.zeros(st_ref.shape, F32)

    row = lax.broadcasted_iota(jnp.int32, (P2, P2), 0)
    col = lax.broadcasted_iota(jnp.int32, (P2, P2), 1)
    same_head = (row // L) == (col // L)
    rt = row % L
    ct = col % L
    eye = row == col
    ti = lax.broadcasted_iota(jnp.int32, (L, L), 0)
    tj = lax.broadcasted_iota(jnp.int32, (L, L), 1)
    lane_a = lax.broadcasted_iota(jnp.int32, (L, LANES), 1) < HEAD_DIM

    def expand(x):
        return jnp.concatenate([jnp.where(lane_a, x, 0.0), jnp.where(lane_a, 0.0, x)], axis=0)

    units = []
    dirs = ((lwf_ref, rf_ref, vf_ref, kkf_ref, kaf_ref, kdf_ref, yf_ref),
            (lwb_ref, rb_ref, vb_ref, kkb_ref, kab_ref, kdb_ref, yb_ref))
    for d, (lw_ref, r_ref, v_ref, kk_ref, ka_ref, kd_ref, y_ref) in enumerate(dirs):
        if d == 0:
            m_strict = same_head & (rt > ct)
            m_incl = same_head & (rt >= ct)
            tri = jnp.where(tj <= ti, 1.0, 0.0).astype(BF16)
        else:
            m_strict = same_head & (rt < ct)
            m_incl = same_head & (rt <= ct)
            tri = jnp.where(tj >= ti, 1.0, 0.0).astype(BF16)
        m_level = []
        for lvl in range(L.bit_length() - 1):
            joined = same_head & ((rt >> (lvl + 1)) == (ct >> (lvl + 1)))
            r_bit = (rt >> lvl) & 1
            c_bit = (ct >> lvl) & 1
            later_row = (r_bit == 1) & (c_bit == 0) if d == 0 else (r_bit == 0) & (c_bit == 1)
            m_level.append(joined & later_row)
        lw = lw_ref[...]
        hi = lw.astype(BF16)
        r1 = lw - hi.astype(F32)
        mid = r1.astype(BF16)
        lo = (r1 - mid.astype(F32)).astype(BF16)
        cum = _dot(tri, hi) + _dot(tri, mid) + _dot(tri, lo)
        cum_last = cum[L - 1:L, :] if d == 0 else cum[0:1, :]
        ec = jnp.exp(cum)
        enc = jnp.exp(-cum)
        p_last = jnp.exp(cum_last)
        a_t = -kk_ref[...].astype(F32) * jnp.exp(cum - lw)
        r_t = r_ref[...].astype(F32) * ec
        b_t = ka_ref[...].astype(F32) * enc
        k_t = kd_ref[...].astype(F32) * enc
        b_h = b_t * p_last
        k_h = k_t * p_last
        vv = v_ref[...].astype(F32)

        for p in range(lw.shape[1] // LANES):
            sl = slice(p * LANES, (p + 1) * LANES)
            units.append(dict(
                d=d, p=p, sl=sl, y_ref=y_ref, m_strict=m_strict, m_incl=m_incl, m_level=m_level,
                ea=expand(a_t[:, sl]), er=expand(r_t[:, sl]), vexp=expand(vv[:, sl]),
                bk2=jnp.concatenate([b_t[:, sl]] * 2 + [k_t[:, sl]] * 2, axis=0).astype(BF16),
                bh=b_h[:, sl], kh=k_h[:, sl], p_last=p_last[:, sl]))

    for u in units:
        g = _dot_nt(jnp.concatenate([u["ea"], u["er"]], axis=0).astype(BF16), u["bk2"])
        u["xab"] = jnp.where(u["m_strict"], g[0:P2, 0:P2], 0.0)
        u["xak"] = jnp.where(u["m_strict"], g[0:P2, P2:2 * P2], 0.0).astype(BF16)
        u["xr"] = jnp.concatenate([jnp.where(u["m_incl"], g[P2:2 * P2, 0:P2], 0.0),
                                   jnp.where(u["m_incl"], g[P2:2 * P2, P2:2 * P2], 0.0)], axis=1).astype(BF16)
        u["tinv"] = jnp.where(eye, 1.0, jnp.where(u["m_level"][0], u["xab"], 0.0))
    for lvl in range(1, L.bit_length() - 1):
        for u in units:
            xoff = jnp.where(u["m_level"][lvl], u["xab"], 0.0).astype(BF16)
            u["tx"] = _dot(u["tinv"].astype(BF16), xoff).astype(BF16)
        for u in units:
            u["tinv"] = u["tinv"] + _dot(u["tx"], u["tinv"].astype(BF16))
    for u in units:
        u["xv"] = _dot(u["xak"], u["vexp"].astype(BF16))
    for u in units:
        u["wu0"] = _dot(u["tinv"].astype(BF16), jnp.concatenate([u["ea"], u["xv"]], axis=1).astype(BF16))
        u["bkt"] = jnp.concatenate([expand(u["bh"]).T, expand(u["kh"]).T], axis=1).astype(BF16)
        u["p_col"] = jnp.broadcast_to(u["p_last"], (P2, LANES)).T
    for u in units:
        u["st"] = st_ref[u["d"], u["p"]]
        u["ws"] = _dot(jnp.concatenate([u["wu0"][:, 0:LANES], u["er"]], axis=0).astype(BF16), u["st"].astype(BF16))
    for u in units:
        uu = u["ws"][0:P2] + u["wu0"][:, LANES:2 * LANES]
        u["uv"] = jnp.concatenate([uu, u["vexp"]], axis=0).astype(BF16)
    for u in units:
        yexp = u["ws"][P2:2 * P2] + _dot(u["xr"], u["uv"])
        u["y_ref"][:, u["sl"]] = (yexp[0:L] + yexp[L:P2]).astype(u["y_ref"].dtype)
    for u in units:
        st_ref[u["d"], u["p"]] = u["p_col"] * u["st"] + _dot(u["bkt"], u["uv"])


def _wkv_scan(lw0, lw1, r, v, kk, ka0, ka1, kd0, kd1):
    b, s, c = r.shape
    L = SCAN_CHUNK
    nc = s // L
    fwd = lambda bi, i: (bi, i, 0)
    bwd = lambda bi, i: (bi, nc - 1 - i, 0)
    fs = pl.BlockSpec((None, L, c), fwd)
    bs = pl.BlockSpec((None, L, c), bwd)
    return pl.pallas_call(
        _wkv_scan_kernel,
        grid=(b, nc),
        in_specs=[fs] * 6 + [bs] * 6,
        out_specs=[fs, bs],
        out_shape=[jax.ShapeDtypeStruct((b, s, c), F32)] * 2,
        scratch_shapes=[pltpu.VMEM((2, c // LANES, LANES, LANES), F32)],
        compiler_params=_params(("parallel", "arbitrary")),
        name="wkv_scan",
    )(lw0, r, v, kk, ka0, kd0, lw1, r, v, kk, ka1, kd1)


def _post_kernel(x_ref, oa_ref, yf_ref, yb_ref, g_ref, bv_ref, zg_ref, lnw_ref, lnb_ref,
                 woa_ref, wor_ref, wout_ref, n2w_ref, wq_ref, x1_ref, xn_ref, qp_ref):
    y = yf_ref[...] + yb_ref[...]
    c = y.shape[1]
    e, et = _seg_mats(c, HEAD_DIM)
    mu = _seg_sum(y, e, et) * (1.0 / HEAD_DIM)
    dlt = y - mu
    var = _seg_sum(dlt * dlt, e, et) * (1.0 / HEAD_DIM)
    yn = dlt * lax.rsqrt(var + RWKV_LN_EPS) * lnw_ref[...] + lnb_ref[...]
    out = ((yn + bv_ref[...].astype(F32)) * g_ref[...].astype(F32)).astype(BF16)
    h_rwkv = _dot(out, wor_ref[...])
    h_attn = _dot(oa_ref[...], woa_ref[...])
    zg = zg_ref[...].astype(F32)
    merged = _sigmoid(zg[:, 0:c]) * h_attn + _sigmoid(zg[:, c:2 * c]) * h_rwkv
    x1 = x_ref[...] + _dot(merged.astype(BF16), wout_ref[...])
    x1_ref[...] = x1
    ms = jnp.mean(x1 * x1, axis=-1, keepdims=True)
    xn = (x1 * lax.rsqrt(ms + NORM_EPS) * n2w_ref[...]).astype(BF16)
    xn_ref[...] = xn
    qp = _dot(xn, wq_ref[...]).astype(BF16)
    for j in range(qp_ref.shape[0]):
        qp_ref[j] = qp[:, j * LANES:(j + 1) * LANES]


def _post(x, oa, yf, yb, g, bv, zg, lnw, lnb, woa, wor, wout, n2w, wq, tm):
    t, d = x.shape
    nq = wq.shape[1] // LANES
    tok = lambda i: (i, 0)
    const = lambda i: (0, 0)
    tspec = lambda w: pl.BlockSpec((tm, w), tok)
    cspec = lambda a: pl.BlockSpec(a.shape, const)
    return pl.pallas_call(
        _post_kernel,
        grid=(t // tm,),
        in_specs=[tspec(d), tspec(d), tspec(d), tspec(d), tspec(d), tspec(d), tspec(2 * d),
                  cspec(lnw), cspec(lnb), cspec(woa), cspec(wor), cspec(wout), cspec(n2w), cspec(wq)],
        out_specs=[tspec(d), tspec(d), pl.BlockSpec((nq, tm, LANES), lambda i: (0, i, 0))],
        out_shape=[jax.ShapeDtypeStruct((t, d), F32), jax.ShapeDtypeStruct((t, d), BF16),
                   jax.ShapeDtypeStruct((nq, t, LANES), BF16)],
        compiler_params=_params(("parallel",)),
        name="post_merge",
    )(x, oa, yf, yb, g, bv, zg, lnw, lnb, woa, wor, wout, n2w, wq)


SUBLANES = 8


def _sort_pairs(n):
    def merge(lo, hi, r):
        step = r * 2
        if step < hi - lo:
            yield from merge(lo, hi, step)
            yield from merge(lo + r, hi, step)
            for i in range(lo + r, hi - r, step):
                yield (i, i + r)
        else:
            yield (lo, lo + r)

    def sort(lo, hi):
        if hi - lo >= 1:
            mid = lo + (hi - lo) // 2
            yield from sort(lo, mid)
            yield from sort(mid + 1, hi)
            yield from merge(lo, hi, 1)

    return list(sort(0, n - 1))


def _exchange(xs, i, j):
    a, b = xs[i], xs[j]
    if b is None:
        return
    if a is None:
        xs[i], xs[j] = b, None
        return
    xs[i], xs[j] = jnp.maximum(a, b), jnp.minimum(a, b)


def _top_sorted(blocks, k):
    xs = list(blocks)
    for i, j in _sort_pairs(k):
        _exchange(xs, i, j)
    shift = SUBLANES // 2
    while shift >= 1:
        ys = [None if x is None else pltpu.roll(x, shift, 0) for x in xs]
        zs = []
        for i in range(k):
            a, b = xs[i], ys[k - 1 - i]
            zs.append(b if a is None else (a if b is None else jnp.maximum(a, b)))
        step = k // 2
        while step >= 1:
            for i in range(k):
                if i & step == 0:
                    _exchange(zs, i, i + step)
            step //= 2
        xs = zs
        shift //= 2
    return [x[0:1, :] for x in xs]


def _row_blocks(x):
    return [x[r:r + SUBLANES, :] for r in range(0, x.shape[0], SUBLANES)]


def _peer_select(s0, s1, k):
    a = _top_sorted(_row_blocks(s0), k)
    b = _top_sorted(_row_blocks(s1), k)
    amat = jnp.concatenate(a, axis=0)
    bmat = jnp.concatenate(b, axis=0)
    h8 = k // 2
    cand = ([a[0] + bmat[0:h8], a[0] + bmat[h8:k], a[1] + bmat[0:h8], amat[h8:k] + b[0]]
            + [a[i] + bmat[0:h8] for i in range(2, h8)])
    best = _top_sorted(cand + [None] * (k - len(cand)), k)
    z = jnp.zeros_like(best[0])
    for val in best:
        z = z + jnp.exp(val - best[0])
    return amat, bmat, best[k - 1], z


def _peer_kernel(xn_ref, qp_ref, sk_ref, u_ref, vt_ref, x1_ref, o_ref,
                 cnt_ref, e0_ref, rank_ref, e1_ref, acc_ref):
    jb = pl.program_id(1)
    tt = xn_ref.shape[0]
    K = PEER_TOPK
    rows_per_blk = u_ref.shape[0] // PEER_NKEYS

    @pl.when(jb == 0)
    def _():
        acc_ref[...] = jnp.zeros(acc_ref.shape, F32)

        def head_body(h, carry):
            s0 = _dot_nt(sk_ref[2 * h], qp_ref[2 * h])
            s1 = _dot_nt(sk_ref[2 * h + 1], qp_ref[2 * h + 1])

            amat, bmat, thr, z = _peer_select(s0, s1, K)
            a0 = amat[0:1]
            top_cnt = jnp.zeros(amat.shape, F32)
            for kk in range(K):
                top_cnt = top_cnt + jnp.where(amat + bmat[kk:kk + 1] >= thr, 1.0, 0.0)
            cnt = jnp.zeros(s0.shape, F32)
            rank = jnp.full(s1.shape, float(PEER_NKEYS), F32)
            for kk in reversed(range(K)):
                cnt = jnp.where(s0 == amat[kk:kk + 1], top_cnt[kk:kk + 1], cnt)
                rank = jnp.where(s1 == bmat[kk:kk + 1], float(kk + 1), rank)
            cnt_ref[h] = cnt
            e0_ref[h] = jnp.exp(s0 - a0) / z
            rank_ref[h] = rank.astype(BF16)
            e1_ref[h] = jnp.exp(s1 - bmat[0:1]).astype(BF16)
            return carry

        lax.fori_loop(0, PEER_HEADS, head_body, 0)

    sub = PEER_SUB_ROWS * PEER_NKEYS
    xn = xn_ref[...]
    n_sub = u_ref.shape[0] // sub
    hpres = [None] * n_sub
    hpres[0] = _dot_nt(u_ref[0:sub, :], xn)
    zero = jnp.zeros((), BF16)
    for c in range(n_sub):
        if c + 1 < n_sub:
            hpres[c + 1] = _dot_nt(u_ref[(c + 1) * sub:(c + 2) * sub, :], xn)
        g_rows = []
        for il in range(c * PEER_SUB_ROWS, (c + 1) * PEER_SUB_ROWS):
            i = jb * rows_per_blk + il
            acc = None
            for h in range(PEER_HEADS):
                crow = cnt_ref[h, pl.ds(i, 1), :].astype(BF16)
                e0row = e0_ref[h, pl.ds(i, 1), :].astype(BF16)
                contrib = jnp.where(rank_ref[h] <= crow, e0row * e1_ref[h], zero)
                acc = contrib if acc is None else acc + contrib
            g_rows.append(acc)
        gates = jnp.concatenate(g_rows, axis=0)
        hpre = hpres[c]
        gelu = 0.5 * hpre * (1.0 + lax.erf(hpre * (2.0 ** -0.5)))
        hg = gelu.astype(BF16) * gates
        acc_ref[...] += _dot(vt_ref[:, c * sub:(c + 1) * sub], hg)

    @pl.when(jb == pl.num_programs(1) - 1)
    def _():
        o_ref[...] = x1_ref[...] + acc_ref[...].T


def _peer(xn, qp, sk, u, vt, x1, tt, eb):
    t, d = xn.shape
    n_exp = u.shape[0]
    nq = qp.shape[0]
    return pl.pallas_call(
        _peer_kernel,
        grid=(t // tt, n_exp // eb),
        in_specs=[pl.BlockSpec((tt, d), lambda i, j: (i, 0)),
                  pl.BlockSpec((nq, tt, LANES), lambda i, j: (0, i, 0)),
                  pl.BlockSpec(sk.shape, lambda i, j: (0, 0, 0)),
                  pl.BlockSpec((eb, d), lambda i, j: (j, 0)),
                  pl.BlockSpec((d, eb), lambda i, j: (0, j)),
                  pl.BlockSpec((tt, d), lambda i, j: (i, 0))],
        out_specs=pl.BlockSpec((tt, d), lambda i, j: (i, 0)),
        out_shape=jax.ShapeDtypeStruct((t, d), F32),
        scratch_shapes=[pltpu.VMEM((PEER_HEADS, PEER_NKEYS, tt), F32),
                        pltpu.VMEM((PEER_HEADS, PEER_NKEYS, tt), F32),
                        pltpu.VMEM((PEER_HEADS, PEER_NKEYS, tt), BF16),
                        pltpu.VMEM((PEER_HEADS, PEER_NKEYS, tt), BF16),
                        pltpu.VMEM((d, tt), F32)],
        compiler_params=_params(("parallel", "arbitrary")),
        name="peer",
    )(xn, qp, sk, u, vt, x1)


def _rope_tables(s):
    inv = 1.0 / (ROPE_THETA ** (jnp.arange(0, HEAD_DIM, 2, dtype=F32) / HEAD_DIM))
    ang = jnp.arange(s, dtype=F32)[:, None] * inv[None, :]
    cos, sin = jnp.cos(ang), jnp.sin(ang)
    reps = LANES // HEAD_DIM
    cos_t = jnp.tile(jnp.concatenate([cos, cos], axis=1), (1, reps))
    sin_t = jnp.tile(jnp.concatenate([-sin, sin], axis=1), (1, reps))
    return cos_t, sin_t


def _tile(n, pref):
    return pref if n % pref == 0 else n


def _trunk(x, P):
    b, s, d = x.shape
    t = b * s
    c = d
    xf = x.reshape(t, d)
    zg = _norm_matmul(xf, P["norm1_w"], P["w_g"], _tile(t, 512))

    cos_t, sin_t = _rope_tables(s)
    q, kt, va = _inproj_attn(x, P["norm1_w"], P["w_a"], cos_t, sin_t, P["q_norm_w"], P["k_norm_w"],
                             HEAD_DIM ** -0.5 * math.log2(math.e), _tile(s, 512))
    oa = _attention(P["lam_rows"], q, kt, va, P["subln_w"], _tile(s, 1024), _tile(s, 2048))

    zr = _norm_matmul(xf, P["norm1_w"], P["w_r"], _tile(t, 512)).reshape(b, s, -1)
    (r, v, kk, kd0, kd1, ka0, ka1, lw0, lw1, g, bv) = _rwkv_prep(
        zr, P["shift_mu"], P["decay_w0"], P["decay_w2"], P["iclr_a0"], P["iclr_a2"],
        P["gate_g2"], P["k_k"], P["k_a"], P["r_k"], c, _tile(s, 256))
    yf, yb = _wkv_scan(lw0, lw1, r, v, kk, ka0, ka1, kd0, kd1)

    flat = lambda a: a.reshape(t, a.shape[-1])
    x1, xn, qp = _post(xf, flat(oa), flat(yf), flat(yb), flat(g), flat(bv), zg, P["lnx_w"], P["lnx_b"],
                       P["w_o_attn"], P["w_o_rwkv"], P["w_out"], P["norm2_w"], P["peer_wq"], _tile(t, 256))
    y = _peer(xn, qp, P["peer_subkeys"], P["peer_u"], P["peer_vt"], x1, _tile(t, 512), 2048)
    return y.reshape(b, s, d)


def kernel(x_prompt, x_sample, norm1_w, w_in, q_norm_w, k_norm_w, lambda_q1, lambda_k1, lambda_q2, lambda_k2, subln_w, w_o_attn, shift_mu, decay_w0, decay_w2, iclr_a0, iclr_a2, gate_g2, k_k, k_a, r_k, lnx_w, lnx_b, w_o_rwkv, w_out, norm2_w, peer_wq, peer_subkeys, peer_u, peer_v):
    d = x_prompt.shape[-1]
    c = d
    att_cols = 3 * d
    rwkv_cols = shift_mu.shape[-1]
    l = 0
    w = w_in[l].astype(BF16)
    nsub = d // HEAD_DIM
    P = {
        "norm1_w": norm1_w[l][None, :],
        "w_a": w[:, :att_cols],
        "w_r": w[:, att_cols:att_cols + rwkv_cols],
        "w_g": w[:, att_cols + rwkv_cols:],
        "q_norm_w": jnp.tile(q_norm_w[l], nsub)[None, :],
        "k_norm_w": jnp.tile(k_norm_w[l], nsub)[None, :],
        "lam_rows": jnp.stack([lambda_q1[l], lambda_k1[l], lambda_q2[l], lambda_k2[l]]),
        "subln_w": subln_w[l][None, :],
        "w_o_attn": w_o_attn[l].astype(BF16),
        "shift_mu": shift_mu[l][None, :],
        "decay_w0": decay_w0[l],
        "decay_w2": decay_w2[l].reshape(-1, c).astype(BF16),
        "iclr_a0": iclr_a0[l],
        "iclr_a2": iclr_a2[l].reshape(-1, c).astype(BF16),
        "gate_g2": gate_g2[l].astype(BF16),
        "k_k": k_k[l][None, :],
        "k_a": k_a[l][None, :],
        "r_k": r_k[l].reshape(1, c),
        "lnx_w": lnx_w[l][None, :],
        "lnx_b": lnx_b[l][None, :],
        "w_o_rwkv": w_o_rwkv[l].astype(BF16),
        "w_out": w_out[l].astype(BF16),
        "norm2_w": norm2_w[l][None, :],
        "peer_wq": peer_wq[l].astype(BF16),
        "peer_subkeys": peer_subkeys[l].reshape(-1, PEER_NKEYS, peer_subkeys.shape[-1]).astype(BF16),
        "peer_u": peer_u[l].astype(BF16),
        "peer_vt": peer_v[l].astype(BF16).T,
    }
    return (_trunk(x_prompt, P), _trunk(x_sample, P))
```

```python
import functools
import math

import jax
import jax.numpy as jnp
from jax import lax
from jax.experimental import pallas as pl
from jax.experimental.pallas import tpu as pltpu

F32 = jnp.float32
BF16 = jnp.bfloat16

LANES = 128
HEAD_DIM = 64
ATT_HEADS = 8
NORM_EPS = 1e-6
RWKV_LN_EPS = 64e-5
ROPE_THETA = 10000.0
LAMBDA_INIT = 0.8 - 0.6 * math.exp(-0.3 * 0)
SCAN_CHUNK = 64
PEER_NKEYS = 128
PEER_TOPK = 16
PEER_HEADS = 8
PEER_SUB_ROWS = 2
PEER_OUT_GROUP = 2
VMEM_LIMIT = 56 * 1024 * 1024


def _params(sem):
    return pltpu.CompilerParams(dimension_semantics=sem, vmem_limit_bytes=VMEM_LIMIT)


def _dot(a, b):
    return jnp.dot(a, b, preferred_element_type=F32)


def _dot_nt(a, b):
    return lax.dot_general(a, b, (((1,), (1,)), ((), ())), preferred_element_type=F32)


def _split_dot(x, m):
    hi = x.astype(BF16)
    lo = (x - hi.astype(F32)).astype(BF16)
    return _dot(hi, m) + _dot(lo, m)


def _seg_mats(width, seg):
    lane = lax.broadcasted_iota(jnp.int32, (width, LANES), 0)
    grp = lax.broadcasted_iota(jnp.int32, (width, LANES), 1)
    e = jnp.where(lane // seg == grp, 1.0, 0.0).astype(BF16)
    grp_t = lax.broadcasted_iota(jnp.int32, (LANES, width), 0)
    lane_t = lax.broadcasted_iota(jnp.int32, (LANES, width), 1)
    et = jnp.where(lane_t // seg == grp_t, 1.0, 0.0).astype(BF16)
    return e, et


def _seg_sum(x, e, et):
    return _split_dot(_split_dot(x, e), et)


def _sigmoid(x):
    return 1.0 / (1.0 + jnp.exp(-x))


def _norm_matmul_kernel(x_ref, nw_ref, w_ref, o_ref):
    x = x_ref[...]
    ms = jnp.mean(x * x, axis=-1, keepdims=True)
    h = (x * lax.rsqrt(ms + NORM_EPS) * nw_ref[...]).astype(BF16)
    o_ref[...] = _dot(h, w_ref[...]).astype(o_ref.dtype)


def _norm_matmul(x, nw, w, tm):
    t, d = x.shape
    n = w.shape[1]
    return pl.pallas_call(
        _norm_matmul_kernel,
        grid=(t // tm,),
        in_specs=[pl.BlockSpec((tm, d), lambda i: (i, 0)),
                  pl.BlockSpec((1, d), lambda i: (0, 0)),
                  pl.BlockSpec((d, n), lambda i: (0, 0))],
        out_specs=pl.BlockSpec((tm, n), lambda i: (i, 0)),
        out_shape=jax.ShapeDtypeStruct((t, n), BF16),
        compiler_params=_params(("parallel",)),
        name="norm_matmul",
    )(x, nw, w)


def _norm_rope(x, w, c, s, e, et, scale):
    width = x.shape[1]
    ms = _seg_sum(x * x, e, et) * (1.0 / HEAD_DIM)
    y = x * lax.rsqrt(ms + NORM_EPS) * w
    lane = lax.broadcasted_iota(jnp.int32, y.shape, 1)
    half = HEAD_DIM // 2
    partner = jnp.where(lane % HEAD_DIM < half, pltpu.roll(y, width - half, 1), pltpu.roll(y, half, 1))
    return (y * c + partner * s) * scale


def _inproj_attn_kernel(x_ref, nw_ref, w_ref, cos_ref, sin_ref, qw_ref, kw_ref, q_ref, kt_ref, v_ref, *, q_scale):
    x = x_ref[...]
    ms = jnp.mean(x * x, axis=-1, keepdims=True)
    h = (x * lax.rsqrt(ms + NORM_EPS) * nw_ref[...]).astype(BF16)
    z = _dot(h, w_ref[...])
    width = q_ref.shape[1]
    reps = width // LANES
    e, et = _seg_mats(width, HEAD_DIM)
    c = jnp.concatenate([cos_ref[...]] * reps, axis=1)
    s = jnp.concatenate([sin_ref[...]] * reps, axis=1)
    q_ref[...] = _norm_rope(z[:, 0:width], qw_ref[...], c, s, e, et, q_scale).astype(q_ref.dtype)
    k = _norm_rope(z[:, width:2 * width], kw_ref[...], c, s, e, et, 1.0)
    for j in range(reps):
        kt_ref[j * LANES:(j + 1) * LANES, :] = k[:, j * LANES:(j + 1) * LANES].T.astype(kt_ref.dtype)
    v_ref[...] = z[:, 2 * width:3 * width].astype(v_ref.dtype)


def _inproj_attn(x, nw, w_a, cos_t, sin_t, qw, kw, q_scale, tm):
    b, s, d = x.shape
    width = qw.shape[1]
    tok = lambda bi, i: (bi, i, 0)
    const = lambda bi, i: (0, 0)
    return pl.pallas_call(
        functools.partial(_inproj_attn_kernel, q_scale=q_scale),
        grid=(b, s // tm),
        in_specs=[pl.BlockSpec((None, tm, d), tok),
                  pl.BlockSpec((1, d), const),
                  pl.BlockSpec(w_a.shape, const),
                  pl.BlockSpec((tm, LANES), lambda bi, i: (i, 0)),
                  pl.BlockSpec((tm, LANES), lambda bi, i: (i, 0)),
                  pl.BlockSpec((1, width), const),
                  pl.BlockSpec((1, width), const)],
        out_specs=[pl.BlockSpec((None, tm, width), tok),
                   pl.BlockSpec((None, width, tm), lambda bi, i: (bi, 0, i)),
                   pl.BlockSpec((None, tm, width), tok)],
        out_shape=[jax.ShapeDtypeStruct((b, s, width), BF16),
                   jax.ShapeDtypeStruct((b, width, s), BF16),
                   jax.ShapeDtypeStruct((b, s, width), BF16)],
        compiler_params=_params(("parallel", "parallel")),
        name="inproj_attn",
    )(x, nw, w_a, cos_t, sin_t, qw, kw)


ATTN_ROW_CHUNK = 512


def _attn_kernel(lam_ref, q_ref, kt_ref, v_ref, sw_ref, o_ref, qs_ref, m_ref, acc_ref, s0_ref, *, tq, tk):
    q = q_ref[...]
    lane = lax.broadcasted_iota(jnp.int32, q.shape, 1)
    zero = jnp.zeros_like(q)
    qs_ref[0:tq, :] = jnp.where(lane < HEAD_DIM, q, zero)
    qs_ref[tq:2 * tq, :] = jnp.where(lane < HEAD_DIM, zero, q)
    m_ref[...] = jnp.full(m_ref.shape, -jnp.inf, F32)
    acc_ref[...] = jnp.zeros(acc_ref.shape, F32)

    rc = min(ATTN_ROW_CHUNK, 2 * tq)
    n_chunks = 2 * tq // rc
    n_tiles = kt_ref.shape[1] // tk
    s0_ref[...] = _dot(qs_ref[0:rc, :], kt_ref[:, 0:tk])

    def key_tile(j, carry):
        off = pl.multiple_of(j * tk, tk)
        v = v_ref[pl.ds(off, tk), :]
        v_ones = jnp.concatenate([v, jnp.ones_like(v)], axis=1)
        kt = kt_ref[:, pl.ds(off, tk)]
        scores = [None] * n_chunks
        scores[0] = s0_ref[...]
        for c in range(n_chunks):
            rows = slice(c * rc, (c + 1) * rc)
            if c + 1 < n_chunks:
                scores[c + 1] = _dot(qs_ref[(c + 1) * rc:(c + 2) * rc, :], kt)
            else:
                off_next = pl.multiple_of(jnp.minimum(j + 1, n_tiles - 1) * tk, tk)
                s0_ref[...] = _dot(qs_ref[0:rc, :], kt_ref[:, pl.ds(off_next, tk)])
            s = scores[c]
            m_prev = m_ref[rows, :]
            m_new = jnp.maximum(m_prev, jnp.max(s, axis=1, keepdims=True))
            alpha = jnp.exp2(m_prev - m_new)
            p = jnp.exp2((s - jnp.concatenate([m_new] * (tk // LANES), axis=1)).astype(BF16))
            acc_ref[rows, :] = (jnp.concatenate([alpha, alpha], axis=1) * acc_ref[rows, :]
                                + _dot(p, v_ones))
            m_ref[rows, :] = m_new
        return carry

    lax.fori_loop(0, n_tiles, key_tile, 0)

    lv = lam_ref[...]
    lam = (jnp.exp(jnp.sum(lv[0:1] * lv[1:2], axis=1, keepdims=True))
           - jnp.exp(jnp.sum(lv[2:3] * lv[3:4], axis=1, keepdims=True)) + LAMBDA_INIT)
    acc = acc_ref[...]
    o = (acc[0:tq, 0:LANES] / acc[0:tq, LANES:2 * LANES]
         - lam * (acc[tq:2 * tq, 0:LANES] / acc[tq:2 * tq, LANES:2 * LANES]))
    ms = jnp.mean(o * o, axis=-1, keepdims=True)
    o = o * lax.rsqrt(ms + NORM_EPS) * sw_ref[...] * (1.0 - LAMBDA_INIT)
    o_ref[...] = o.astype(o_ref.dtype)


def _attention(lam_rows, q, kt, v, subln_w, tq, tk):
    b, s, width = q.shape
    heads = width // LANES
    return pl.pallas_call(
        functools.partial(_attn_kernel, tq=tq, tk=tk),
        grid=(b, heads, s // tq),
        in_specs=[pl.BlockSpec((4, HEAD_DIM), lambda bi, h, qi: (0, 0)),
                  pl.BlockSpec((None, tq, LANES), lambda bi, h, qi: (bi, qi, h)),
                  pl.BlockSpec((None, LANES, s), lambda bi, h, qi: (bi, h, 0)),
                  pl.BlockSpec((None, s, LANES), lambda bi, h, qi: (bi, 0, h)),
                  pl.BlockSpec((1, LANES), lambda bi, h, qi: (0, 0))],
        out_specs=pl.BlockSpec((None, tq, LANES), lambda bi, h, qi: (bi, qi, h)),
        out_shape=jax.ShapeDtypeStruct((b, s, width), BF16),
        scratch_shapes=[pltpu.VMEM((2 * tq, LANES), BF16),
                        pltpu.VMEM((2 * tq, LANES), F32),
                        pltpu.VMEM((2 * tq, 2 * LANES), F32),
                        pltpu.VMEM((min(ATTN_ROW_CHUNK, 2 * tq), tk), F32)],
        compiler_params=_params(("parallel", "parallel", "arbitrary")),
        name="diff_attention",
    )(lam_rows, q, kt, v, subln_w)


HALO = 16


def _rwkv_prep_kernel(z_ref, zp_ref, zn_ref, mu_ref, w0_ref, w2_ref, a0_ref, a2_ref, g2_ref,
                      kk_w_ref, ka_w_ref, rk_ref,
                      r_ref, v_ref, kk_ref, kd0_ref, kd1_ref, ka0_ref, ka1_ref, lw0_ref, lw1_ref,
                      g_ref, bv_ref, *, c):
    i = pl.program_id(1)
    z = z_ref[...].astype(F32)
    tm = z.shape[0]
    row = lax.broadcasted_iota(jnp.int32, z.shape, 0)
    prev_row = jnp.where(i == 0, 0.0, zp_ref[HALO - 1:HALO, :].astype(F32))
    next_row = jnp.where(i == pl.num_programs(1) - 1, 0.0, zn_ref[0:1, :].astype(F32))
    z_up = jnp.where(row == 0, prev_row, pltpu.roll(z, 1, 0))
    z_dn = jnp.where(row == tm - 1, next_row, pltpu.roll(z, tm - 1, 0))
    zs = z + mu_ref[...] * (0.5 * (z_up + z_dn) - z)

    r = zs[:, 0:c]
    k = zs[:, c:2 * c]
    v = zs[:, 2 * c:3 * c]
    wl = zs[:, 3 * c:3 * c + LANES]
    al = zs[:, 3 * c + LANES:3 * c + 2 * LANES]
    gl = zs[:, 3 * c + 2 * LANES:3 * c + 3 * LANES]

    lane = lax.broadcasted_iota(jnp.int32, wl.shape, 1)
    twl = jnp.tanh(wl)
    e, et = _seg_mats(c, HEAD_DIM)

    kk = k * kk_w_ref[...]
    kk = kk * lax.rsqrt(_seg_sum(kk * kk, e, et) + 1e-12)
    kk_ref[...] = kk.astype(kk_ref.dtype)
    r_ref[...] = r.astype(r_ref.dtype)
    v_ref[...] = v.astype(v_ref.dtype)
    g_ref[...] = _dot(_sigmoid(gl).astype(BF16), g2_ref[...]).astype(g_ref.dtype)

    ksum = jnp.zeros_like(k)
    for d, (kd_ref, ka_ref, lw_ref) in enumerate(((kd0_ref, ka0_ref, lw0_ref), (kd1_ref, ka1_ref, lw1_ref))):
        sel = (lane < HEAD_DIM) if d == 0 else (lane >= HEAD_DIM)
        wpre = _dot(jnp.where(sel, twl, 0.0).astype(BF16), w2_ref[...]) + w0_ref[d:d + 1, :]
        lw_ref[...] = (-math.exp(-0.5)) * _sigmoid(wpre)
        a = _sigmoid(_dot(jnp.where(sel, al, 0.0).astype(BF16), a2_ref[...]) + a0_ref[d:d + 1, :])
        kd = k * (1.0 + (a - 1.0) * ka_w_ref[...])
        kd_ref[...] = kd.astype(kd_ref.dtype)
        ka_ref[...] = (kk * a).astype(ka_ref.dtype)
        ksum = ksum + kd
    bonus = _seg_sum(r * ksum * rk_ref[...], e, et)
    bv_ref[...] = (bonus * v).astype(bv_ref.dtype)


def _rwkv_prep(zr, mu, w0, w2, a0, a2, g2, kk_w, ka_w, rk, c, tm):
    b, s, cols = zr.shape
    nh = tm // HALO
    last = s // HALO - 1
    tok = lambda bi, i: (bi, i, 0)
    const = lambda bi, i: (0, 0)
    out = lambda dt: jax.ShapeDtypeStruct((b, s, c), dt)
    ospec = pl.BlockSpec((None, tm, c), tok)
    return pl.pallas_call(
        functools.partial(_rwkv_prep_kernel, c=c),
        grid=(b, s // tm),
        in_specs=[pl.BlockSpec((None, tm, cols), tok),
                  pl.BlockSpec((None, HALO, cols), lambda bi, i: (bi, jnp.maximum(i * nh - 1, 0), 0)),
                  pl.BlockSpec((None, HALO, cols), lambda bi, i: (bi, jnp.minimum((i + 1) * nh, last), 0)),
                  pl.BlockSpec((1, cols), const),
                  pl.BlockSpec((2, c), const),
                  pl.BlockSpec((LANES, c), const),
                  pl.BlockSpec((2, c), const),
                  pl.BlockSpec((LANES, c), const),
                  pl.BlockSpec((LANES, c), const),
                  pl.BlockSpec((1, c), const),
                  pl.BlockSpec((1, c), const),
                  pl.BlockSpec((1, c), const)],
        out_specs=[ospec] * 11,
        out_shape=[out(BF16)] * 7 + [out(F32)] * 2 + [out(BF16)] * 2,
        compiler_params=_params(("parallel", "parallel")),
        name="rwkv_prep",
    )(zr, zr, zr, mu, w0, w2, a0, a2, g2, kk_w, ka_w, rk)


def _wkv_scan_kernel(lwf_ref, rf_ref, vf_ref, kkf_ref, kaf_ref, kdf_ref,
                     lwb_ref, rb_ref, vb_ref, kkb_ref, kab_ref, kdb_ref,
                     yf_ref, yb_ref, st_ref):
    L = SCAN_CHUNK
    P2 = 2 * L
    ci = pl.program_id(1)

    @pl.when(ci == 0)
    def _():
        st_ref[...] = jnp.zeros(st_ref.shape, F32)

    row = lax.broadcasted_iota(jnp.int32, (P2, P2), 0)
    col = lax.broadcasted_iota(jnp.int32, (P2, P2), 1)
    same_head = (row // L) == (col // L)
    rt = row % L
    ct = col % L
    eye = row == col
    ti = lax.broadcasted_iota(jnp.int32, (L, L), 0)
    tj = lax.broadcasted_iota(jnp.int32, (L, L), 1)
    lane_a = lax.broadcasted_iota(jnp.int32, (L, LANES), 1) < HEAD_DIM

    def expand(x):
        return jnp.concatenate([jnp.where(lane_a, x, 0.0), jnp.where(lane_a, 0.0, x)], axis=0)

    units = []
    dirs = ((lwf_ref, rf_ref, vf_ref, kkf_ref, kaf_ref, kdf_ref, yf_ref),
            (lwb_ref, rb_ref, vb_ref, kkb_ref, kab_ref, kdb_ref, yb_ref))
    for d, (lw_ref, r_ref, v_ref, kk_ref, ka_ref, kd_ref, y_ref) in enumerate(dirs):
        if d == 0:
            m_strict = same_head & (rt > ct)
            m_incl = same_head & (rt >= ct)
            tri = jnp.where(tj <= ti, 1.0, 0.0).astype(BF16)
        else:
            m_strict = same_head & (rt < ct)
            m_incl = same_head & (rt <= ct)
            tri = jnp.where(tj >= ti, 1.0, 0.0).astype(BF16)
        m_level = []
        for lvl in range(L.bit_length() - 1):
            joined = same_head & ((rt >> (lvl + 1)) == (ct >> (lvl + 1)))
            r_bit = (rt >> lvl) & 1
            c_bit = (ct >> lvl) & 1
            later_row = (r_bit == 1) & (c_bit == 0) if d == 0 else (r_bit == 0) & (c_bit == 1)
            m_level.append(joined & later_row)
        lw = lw_ref[...]
        hi = lw.astype(BF16)
        r1 = lw - hi.astype(F32)
        mid = r1.astype(BF16)
        lo = (r1 - mid.astype(F32)).astype(BF16)
        cum = _dot(tri, hi) + _dot(tri, mid) + _dot(tri, lo)
        cum_last = cum[L - 1:L, :] if d == 0 else cum[0:1, :]
        ec = jnp.exp(cum)
        enc = jnp.exp(-cum)
        p_last = jnp.exp(cum_last)
        a_t = -kk_ref[...].astype(F32) * jnp.exp(cum - lw)
        r_t = r_ref[...].astype(F32) * ec
        b_t = ka_ref[...].astype(F32) * enc
        k_t = kd_ref[...].astype(F32) * enc
        b_h = b_t * p_last
        k_h = k_t * p_last
        vv = v_ref[...].astype(F32)

        for p in range(lw.shape[1] // LANES):
            sl = slice(p * LANES, (p + 1) * LANES)
            units.append(dict(
                d=d, p=p, sl=sl, y_ref=y_ref, m_strict=m_strict, m_incl=m_incl, m_level=m_level,
                ea=expand(a_t[:, sl]), er=expand(r_t[:, sl]), vexp=expand(vv[:, sl]),
                bk2=jnp.concatenate([b_t[:, sl]] * 2 + [k_t[:, sl]] * 2, axis=0).astype(BF16),
                bh=b_h[:, sl], kh=k_h[:, sl], p_last=p_last[:, sl]))

    for u in units:
        g = _dot_nt(jnp.concatenate([u["ea"], u["er"]], axis=0).astype(BF16), u["bk2"])
        u["xab"] = jnp.where(u["m_strict"], g[0:P2, 0:P2], 0.0)
        u["xak"] = jnp.where(u["m_strict"], g[0:P2, P2:2 * P2], 0.0).astype(BF16)
        u["xr"] = jnp.concatenate([jnp.where(u["m_incl"], g[P2:2 * P2, 0:P2], 0.0),
                                   jnp.where(u["m_incl"], g[P2:2 * P2, P2:2 * P2], 0.0)], axis=1).astype(BF16)
        u["tinv"] = jnp.where(eye, 1.0, jnp.where(u["m_level"][0], u["xab"], 0.0))
    for lvl in range(1, L.bit_length() - 1):
        for u in units:
            xoff = jnp.where(u["m_level"][lvl], u["xab"], 0.0).astype(BF16)
            u["tx"] = _dot(u["tinv"].astype(BF16), xoff).astype(BF16)
        for u in units:
            u["tinv"] = u["tinv"] + _dot(u["tx"], u["tinv"].astype(BF16))
    for u in units:
        u["xv"] = _dot(u["xak"], u["vexp"].astype(BF16))
    for u in units:
        u["wu0"] = _dot(u["tinv"].astype(BF16), jnp.concatenate([u["ea"], u["xv"]], axis=1).astype(BF16))
        u["bkt"] = jnp.concatenate([expand(u["bh"]).T, expand(u["kh"]).T], axis=1).astype(BF16)
        u["p_col"] = jnp.broadcast_to(u["p_last"], (P2, LANES)).T
    for u in units:
        u["st"] = st_ref[u["d"], u["p"]]
        u["ws"] = _dot(jnp.concatenate([u["wu0"][:, 0:LANES], u["er"]], axis=0).astype(BF16), u["st"].astype(BF16))
    for u in units:
        uu = u["ws"][0:P2] + u["wu0"][:, LANES:2 * LANES]
        u["uv"] = jnp.concatenate([uu, u["vexp"]], axis=0).astype(BF16)
    for u in units:
        yexp = u["ws"][P2:2 * P2] + _dot(u["xr"], u["uv"])
        u["y_ref"][:, u["sl"]] = (yexp[0:L] + yexp[L:P2]).astype(u["y_ref"].dtype)
    for u in units:
        st_ref[u["d"], u["p"]] = u["p_col"] * u["st"] + _dot(u["bkt"], u["uv"])


def _wkv_scan(lw0, lw1, r, v, kk, ka0, ka1, kd0, kd1):
    b, s, c = r.shape
    L = SCAN_CHUNK
    nc = s // L
    fwd = lambda bi, i: (bi, i, 0)
    bwd = lambda bi, i: (bi, nc - 1 - i, 0)
    fs = pl.BlockSpec((None, L, c), fwd)
    bs = pl.BlockSpec((None, L, c), bwd)
    return pl.pallas_call(
        _wkv_scan_kernel,
        grid=(b, nc),
        in_specs=[fs] * 6 + [bs] * 6,
        out_specs=[fs, bs],
        out_shape=[jax.ShapeDtypeStruct((b, s, c), F32)] * 2,
        scratch_shapes=[pltpu.VMEM((2, c // LANES, LANES, LANES), F32)],
        compiler_params=_params(("parallel", "arbitrary")),
        name="wkv_scan",
    )(lw0, r, v, kk, ka0, kd0, lw1, r, v, kk, ka1, kd1)


def _post_kernel(x_ref, oa_ref, yf_ref, yb_ref, g_ref, bv_ref, zg_ref, lnw_ref, lnb_ref,
                 woa_ref, wor_ref, wout_ref, n2w_ref, wq_ref, x1_ref, xn_ref, qp_ref):
    y = yf_ref[...] + yb_ref[...]
    c = y.shape[1]
    e, et = _seg_mats(c, HEAD_DIM)
    mu = _seg_sum(y, e, et) * (1.0 / HEAD_DIM)
    dlt = y - mu
    var = _seg_sum(dlt * dlt, e, et) * (1.0 / HEAD_DIM)
    yn = dlt * lax.rsqrt(var + RWKV_LN_EPS) * lnw_ref[...] + lnb_ref[...]
    out = ((yn + bv_ref[...].astype(F32)) * g_ref[...].astype(F32)).astype(BF16)
    h_rwkv = _dot(out, wor_ref[...])
    h_attn = _dot(oa_ref[...], woa_ref[...])
    zg = zg_ref[...].astype(F32)
    merged = _sigmoid(zg[:, 0:c]) * h_attn + _sigmoid(zg[:, c:2 * c]) * h_rwkv
    x1 = x_ref[...] + _dot(merged.astype(BF16), wout_ref[...])
    x1_ref[...] = x1
    ms = jnp.mean(x1 * x1, axis=-1, keepdims=True)
    xn = (x1 * lax.rsqrt(ms + NORM_EPS) * n2w_ref[...]).astype(BF16)
    xn_ref[...] = xn
    qp = _dot(xn, wq_ref[...]).astype(BF16)
    for j in range(qp_ref.shape[0]):
        qp_ref[j] = qp[:, j * LANES:(j + 1) * LANES]


def _post(x, oa, yf, yb, g, bv, zg, lnw, lnb, woa, wor, wout, n2w, wq, tm):
    t, d = x.shape
    nq = wq.shape[1] // LANES
    tok = lambda i: (i, 0)
    const = lambda i: (0, 0)
    tspec = lambda w: pl.BlockSpec((tm, w), tok)
    cspec = lambda a: pl.BlockSpec(a.shape, const)
    return pl.pallas_call(
        _post_kernel,
        grid=(t // tm,),
        in_specs=[tspec(d), tspec(d), tspec(d), tspec(d), tspec(d), tspec(d), tspec(2 * d),
                  cspec(lnw), cspec(lnb), cspec(woa), cspec(wor), cspec(wout), cspec(n2w), cspec(wq)],
        out_specs=[tspec(d), tspec(d), pl.BlockSpec((nq, tm, LANES), lambda i: (0, i, 0))],
        out_shape=[jax.ShapeDtypeStruct((t, d), F32), jax.ShapeDtypeStruct((t, d), BF16),
                   jax.ShapeDtypeStruct((nq, t, LANES), BF16)],
        compiler_params=_params(("parallel",)),
        name="post_merge",
    )(x, oa, yf, yb, g, bv, zg, lnw, lnb, woa, wor, wout, n2w, wq)


SUBLANES = 8


def _sort_pairs(n):
    def merge(lo, hi, r):
        step = r * 2
        if step < hi - lo:
            yield from merge(lo, hi, step)
            yield from merge(lo + r, hi, step)
            for i in range(lo + r, hi - r, step):
                yield (i, i + r)
        else:
            yield (lo, lo + r)

    def sort(lo, hi):
        if hi - lo >= 1:
            mid = lo + (hi - lo) // 2
            yield from sort(lo, mid)
            yield from sort(mid + 1, hi)
            yield from merge(lo, hi, 1)

    return list(sort(0, n - 1))


def _exchange(xs, i, j):
    a, b = xs[i], xs[j]
    if b is None:
        return
    if a is None:
        xs[i], xs[j] = b, None
        return
    xs[i], xs[j] = jnp.maximum(a, b), jnp.minimum(a, b)


def _top_sorted(blocks, k):
    xs = list(blocks)
    for i, j in _sort_pairs(k):
        _exchange(xs, i, j)
    shift = SUBLANES // 2
    while shift >= 1:
        ys = [None if x is None else pltpu.roll(x, shift, 0) for x in xs]
        zs = []
        for i in range(k):
            a, b = xs[i], ys[k - 1 - i]
            zs.append(b if a is None else (a if b is None else jnp.maximum(a, b)))
        step = k // 2
        while step >= 1:
            for i in range(k):
                if i & step == 0:
                    _exchange(zs, i, i + step)
            step //= 2
        xs = zs
        shift //= 2
    return [x[0:1, :] for x in xs]


def _row_blocks(x):
    return [x[r:r + SUBLANES, :] for r in range(0, x.shape[0], SUBLANES)]


def _peer_select(s0, s1, k):
    a = _top_sorted(_row_blocks(s0), k)
    b = _top_sorted(_row_blocks(s1), k)
    amat = jnp.concatenate(a, axis=0)
    bmat = jnp.concatenate(b, axis=0)
    h8 = k // 2
    cand = ([a[0] + bmat[0:h8], a[0] + bmat[h8:k], a[1] + bmat[0:h8], amat[h8:k] + b[0]]
            + [a[i] + bmat[0:h8] for i in range(2, h8)])
    best = _top_sorted(cand + [None] * (k - len(cand)), k)
    z = jnp.zeros_like(best[0])
    for val in best:
        z = z + jnp.exp(val - best[0])
    return amat, bmat, best[k - 1], z


def _peer_kernel(xn_ref, qp_ref, sk_ref, u_ref, vt_ref, x1_ref, o_ref,
                 cnt_ref, e0_ref, rank_ref, e1_ref, acc_ref):
    jb = pl.program_id(1)
    tt = xn_ref.shape[0]
    K = PEER_TOPK
    rows_per_blk = u_ref.shape[0] // PEER_NKEYS

    @pl.when(jb == 0)
    def _():
        acc_ref[...] = jnp.zeros(acc_ref.shape, F32)

        def head_body(h, carry):
            s0 = _dot_nt(sk_ref[2 * h], qp_ref[2 * h])
            s1 = _dot_nt(sk_ref[2 * h + 1], qp_ref[2 * h + 1])

            amat, bmat, thr, z = _peer_select(s0, s1, K)
            a0 = amat[0:1]
            top_cnt = jnp.zeros(amat.shape, F32)
            for kk in range(K):
                top_cnt = top_cnt + jnp.where(amat + bmat[kk:kk + 1] >= thr, 1.0, 0.0)
            cnt = jnp.zeros(s0.shape, F32)
            rank = jnp.full(s1.shape, float(PEER_NKEYS), F32)
            for kk in reversed(range(K)):
                cnt = jnp.where(s0 == amat[kk:kk + 1], top_cnt[kk:kk + 1], cnt)
                rank = jnp.where(s1 == bmat[kk:kk + 1], float(kk + 1), rank)
            cnt_ref[h] = cnt
            e0_ref[h] = jnp.exp(s0 - a0) / z
            rank_ref[h] = rank.astype(BF16)
            e1_ref[h] = jnp.exp(s1 - bmat[0:1]).astype(BF16)
            return carry

        lax.fori_loop(0, PEER_HEADS, head_body, 0)

    sub = PEER_SUB_ROWS * PEER_NKEYS
    xn = xn_ref[...]
    n_sub = u_ref.shape[0] // sub
    hpres = [None] * n_sub
    hpres[0] = _dot_nt(u_ref[0:sub, :], xn)
    zero = jnp.zeros((), BF16)
    hgs = []
    for c in range(n_sub):
        if c + 1 < n_sub:
            hpres[c + 1] = _dot_nt(u_ref[(c + 1) * sub:(c + 2) * sub, :], xn)
        g_rows = []
        for il in range(c * PEER_SUB_ROWS, (c + 1) * PEER_SUB_ROWS):
            i = jb * rows_per_blk + il
            acc = None
            for h in range(PEER_HEADS):
                crow = cnt_ref[h, pl.ds(i, 1), :].astype(BF16)
                e0row = e0_ref[h, pl.ds(i, 1), :].astype(BF16)
                contrib = jnp.where(rank_ref[h] <= crow, e0row * e1_ref[h], zero)
                acc = contrib if acc is None else acc + contrib
            g_rows.append(acc)
        gates = jnp.concatenate(g_rows, axis=0)
        hpre = hpres[c]
        gelu = 0.5 * hpre * (1.0 + lax.erf(hpre * (2.0 ** -0.5)))
        hgs.append(gelu.astype(BF16) * gates)
        if len(hgs) == PEER_OUT_GROUP:
            lo = (c + 1 - PEER_OUT_GROUP) * sub
            acc_ref[...] += _dot(vt_ref[:, lo:(c + 1) * sub], jnp.concatenate(hgs, axis=0))
            hgs = []

    @pl.when(jb == pl.num_programs(1) - 1)
    def _():
        o_ref[...] = x1_ref[...] + acc_ref[...].T


def _peer(xn, qp, sk, u, vt, x1, tt, eb):
    t, d = xn.shape
    n_exp = u.shape[0]
    nq = qp.shape[0]
    return pl.pallas_call(
        _peer_kernel,
        grid=(t // tt, n_exp // eb),
        in_specs=[pl.BlockSpec((tt, d), lambda i, j: (i, 0)),
                  pl.BlockSpec((nq, tt, LANES), lambda i, j: (0, i, 0)),
                  pl.BlockSpec(sk.shape, lambda i, j: (0, 0, 0)),
                  pl.BlockSpec((eb, d), lambda i, j: (j, 0)),
                  pl.BlockSpec((d, eb), lambda i, j: (0, j)),
                  pl.BlockSpec((tt, d), lambda i, j: (i, 0))],
        out_specs=pl.BlockSpec((tt, d), lambda i, j: (i, 0)),
        out_shape=jax.ShapeDtypeStruct((t, d), F32),
        scratch_shapes=[pltpu.VMEM((PEER_HEADS, PEER_NKEYS, tt), F32),
                        pltpu.VMEM((PEER_HEADS, PEER_NKEYS, tt), F32),
                        pltpu.VMEM((PEER_HEADS, PEER_NKEYS, tt), BF16),
                        pltpu.VMEM((PEER_HEADS, PEER_NKEYS, tt), BF16),
                        pltpu.VMEM((d, tt), F32)],
        compiler_params=_params(("parallel", "arbitrary")),
        name="peer",
    )(xn, qp, sk, u, vt, x1)


def _rope_tables(s):
    inv = 1.0 / (ROPE_THETA ** (jnp.arange(0, HEAD_DIM, 2, dtype=F32) / HEAD_DIM))
    ang = jnp.arange(s, dtype=F32)[:, None] * inv[None, :]
    cos, sin = jnp.cos(ang), jnp.sin(ang)
    reps = LANES // HEAD_DIM
    cos_t = jnp.tile(jnp.concatenate([cos, cos], axis=1), (1, reps))
    sin_t = jnp.tile(jnp.concatenate([-sin, sin], axis=1), (1, reps))
    return cos_t, sin_t


def _tile(n, pref):
    return pref if n % pref == 0 else n


def _trunk(x, P):
    b, s, d = x.shape
    t = b * s
    c = d
    xf = x.reshape(t, d)
    zg = _norm_matmul(xf, P["norm1_w"], P["w_g"], _tile(t, 512))

    cos_t, sin_t = _rope_tables(s)
    q, kt, va = _inproj_attn(x, P["norm1_w"], P["w_a"], cos_t, sin_t, P["q_norm_w"], P["k_norm_w"],
                             HEAD_DIM ** -0.5 * math.log2(math.e), _tile(s, 512))
    oa = _attention(P["lam_rows"], q, kt, va, P["subln_w"], _tile(s, 2048), _tile(s, 2048))

    zr = _norm_matmul(xf, P["norm1_w"], P["w_r"], _tile(t, 512)).reshape(b, s, -1)
    (r, v, kk, kd0, kd1, ka0, ka1, lw0, lw1, g, bv) = _rwkv_prep(
        zr, P["shift_mu"], P["decay_w0"], P["decay_w2"], P["iclr_a0"], P["iclr_a2"],
        P["gate_g2"], P["k_k"], P["k_a"], P["r_k"], c, _tile(s, 256))
    yf, yb = _wkv_scan(lw0, lw1, r, v, kk, ka0, ka1, kd0, kd1)

    flat = lambda a: a.reshape(t, a.shape[-1])
    x1, xn, qp = _post(xf, flat(oa), flat(yf), flat(yb), flat(g), flat(bv), zg, P["lnx_w"], P["lnx_b"],
                       P["w_o_attn"], P["w_o_rwkv"], P["w_out"], P["norm2_w"], P["peer_wq"], _tile(t, 256))
    y = _peer(xn, qp, P["peer_subkeys"], P["peer_u"], P["peer_vt"], x1, _tile(t, 512), 2048)
    return y.reshape(b, s, d)


def kernel(x_prompt, x_sample, norm1_w, w_in, q_norm_w, k_norm_w, lambda_q1, lambda_k1, lambda_q2, lambda_k2, subln_w, w_o_attn, shift_mu, decay_w0, decay_w2, iclr_a0, iclr_a2, gate_g2, k_k, k_a, r_k, lnx_w, lnx_b, w_o_rwkv, w_out, norm2_w, peer_wq, peer_subkeys, peer_u, peer_v):
    d = x_prompt.shape[-1]
    c = d
    att_cols = 3 * d
    rwkv_cols = shift_mu.shape[-1]
    l = 0
    w = w_in[l].astype(BF16)
    nsub = d // HEAD_DIM
    P = {
        "norm1_w": norm1_w[l][None, :],
        "w_a": w[:, :att_cols],
        "w_r": w[:, att_cols:att_cols + rwkv_cols],
        "w_g": w[:, att_cols + rwkv_cols:],
        "q_norm_w": jnp.tile(q_norm_w[l], nsub)[None, :],
        "k_norm_w": jnp.tile(k_norm_w[l], nsub)[None, :],
        "lam_rows": jnp.stack([lambda_q1[l], lambda_k1[l], lambda_q2[l], lambda_k2[l]]),
        "subln_w": subln_w[l][None, :],
        "w_o_attn": w_o_attn[l].astype(BF16),
        "shift_mu": shift_mu[l][None, :],
        "decay_w0": decay_w0[l],
        "decay_w2": decay_w2[l].reshape(-1, c).astype(BF16),
        "iclr_a0": iclr_a0[l],
        "iclr_a2": iclr_a2[l].reshape(-1, c).astype(BF16),
        "gate_g2": gate_g2[l].astype(BF16),
        "k_k": k_k[l][None, :],
        "k_a": k_a[l][None, :],
        "r_k": r_k[l].reshape(1, c),
        "lnx_w": lnx_w[l][None, :],
        "lnx_b": lnx_b[l][None, :],
        "w_o_rwkv": w_o_rwkv[l].astype(BF16),
        "w_out": w_out[l].astype(BF16),
        "norm2_w": norm2_w[l][None, :],
        "peer_wq": peer_wq[l].astype(BF16),
        "peer_subkeys": peer_subkeys[l].reshape(-1, PEER_NKEYS, peer_subkeys.shape[-1]).astype(BF16),
        "peer_u": peer_u[l].astype(BF16),
        "peer_vt": peer_v[l].astype(BF16).T,
    }
    return (_trunk(x_prompt, P), _trunk(x_sample, P))
```

```python
import functools
import math

import jax
import jax.numpy as jnp
from jax import lax
from jax.experimental import pallas as pl
from jax.experimental.pallas import tpu as pltpu

F32 = jnp.float32
BF16 = jnp.bfloat16

LANES = 128
HEAD_DIM = 64
ATT_HEADS = 8
NORM_EPS = 1e-6
RWKV_LN_EPS = 64e-5
ROPE_THETA = 10000.0
LAMBDA_INIT = 0.8 - 0.6 * math.exp(-0.3 * 0)
SCAN_CHUNK = 64
PEER_NKEYS = 128
PEER_TOPK = 16
PEER_HEADS = 8
PEER_SUB_ROWS = 2
PEER_OUT_GROUP = 8
VMEM_LIMIT = 56 * 1024 * 1024


def _params(sem):
    return pltpu.CompilerParams(dimension_semantics=sem, vmem_limit_bytes=VMEM_LIMIT)


def _dot(a, b):
    return jnp.dot(a, b, preferred_element_type=F32)


def _dot_nt(a, b):
    return lax.dot_general(a, b, (((1,), (1,)), ((), ())), preferred_element_type=F32)


def _split_dot(x, m):
    hi = x.astype(BF16)
    lo = (x - hi.astype(F32)).astype(BF16)
    return _dot(hi, m) + _dot(lo, m)


def _seg_mats(width, seg):
    lane = lax.broadcasted_iota(jnp.int32, (width, LANES), 0)
    grp = lax.broadcasted_iota(jnp.int32, (width, LANES), 1)
    e = jnp.where(lane // seg == grp, 1.0, 0.0).astype(BF16)
    grp_t = lax.broadcasted_iota(jnp.int32, (LANES, width), 0)
    lane_t = lax.broadcasted_iota(jnp.int32, (LANES, width), 1)
    et = jnp.where(lane_t // seg == grp_t, 1.0, 0.0).astype(BF16)
    return e, et


def _seg_sum(x, e, et, signed=False):
    totals = _split_dot(x, e) if signed else _dot(x.astype(BF16), e)
    return _split_dot(totals, et)


def _sigmoid(x):
    return 1.0 / (1.0 + jnp.exp(-x))


def _norm_matmul_kernel(x_ref, nw_ref, w_ref, o_ref):
    x = x_ref[...]
    ms = jnp.mean(x * x, axis=-1, keepdims=True)
    h = (x * lax.rsqrt(ms + NORM_EPS) * nw_ref[...]).astype(BF16)
    o_ref[...] = _dot(h, w_ref[...]).astype(o_ref.dtype)


def _norm_matmul(x, nw, w, tm):
    t, d = x.shape
    n = w.shape[1]
    return pl.pallas_call(
        _norm_matmul_kernel,
        grid=(t // tm,),
        in_specs=[pl.BlockSpec((tm, d), lambda i: (i, 0)),
                  pl.BlockSpec((1, d), lambda i: (0, 0)),
                  pl.BlockSpec((d, n), lambda i: (0, 0))],
        out_specs=pl.BlockSpec((tm, n), lambda i: (i, 0)),
        out_shape=jax.ShapeDtypeStruct((t, n), BF16),
        compiler_params=_params(("parallel",)),
        name="norm_matmul",
    )(x, nw, w)


def _norm_rope(x, w, c, s, e, et, scale):
    width = x.shape[1]
    ms = _seg_sum(x * x, e, et) * (1.0 / HEAD_DIM)
    y = x * lax.rsqrt(ms + NORM_EPS) * w
    lane = lax.broadcasted_iota(jnp.int32, y.shape, 1)
    half = HEAD_DIM // 2
    partner = jnp.where(lane % HEAD_DIM < half, pltpu.roll(y, width - half, 1), pltpu.roll(y, half, 1))
    return (y * c + partner * s) * scale


def _inproj_attn_kernel(x_ref, nw_ref, w_ref, cos_ref, sin_ref, qw_ref, kw_ref, q_ref, kt_ref, v_ref, *, q_scale):
    x = x_ref[...]
    ms = jnp.mean(x * x, axis=-1, keepdims=True)
    h = (x * lax.rsqrt(ms + NORM_EPS) * nw_ref[...]).astype(BF16)
    z = _dot(h, w_ref[...])
    width = q_ref.shape[1]
    reps = width // LANES
    e, et = _seg_mats(width, HEAD_DIM)
    c = jnp.concatenate([cos_ref[...]] * reps, axis=1)
    s = jnp.concatenate([sin_ref[...]] * reps, axis=1)
    q_ref[...] = _norm_rope(z[:, 0:width], qw_ref[...], c, s, e, et, q_scale).astype(q_ref.dtype)
    k = _norm_rope(z[:, width:2 * width], kw_ref[...], c, s, e, et, 1.0)
    for j in range(reps):
        kt_ref[j * LANES:(j + 1) * LANES, :] = k[:, j * LANES:(j + 1) * LANES].T.astype(kt_ref.dtype)
    v_ref[...] = z[:, 2 * width:3 * width].astype(v_ref.dtype)


def _inproj_attn(x, nw, w_a, cos_t, sin_t, qw, kw, q_scale, tm):
    b, s, d = x.shape
    width = qw.shape[1]
    tok = lambda bi, i: (bi, i, 0)
    const = lambda bi, i: (0, 0)
    return pl.pallas_call(
        functools.partial(_inproj_attn_kernel, q_scale=q_scale),
        grid=(b, s // tm),
        in_specs=[pl.BlockSpec((None, tm, d), tok),
                  pl.BlockSpec((1, d), const),
                  pl.BlockSpec(w_a.shape, const),
                  pl.BlockSpec((tm, LANES), lambda bi, i: (i, 0)),
                  pl.BlockSpec((tm, LANES), lambda bi, i: (i, 0)),
                  pl.BlockSpec((1, width), const),
                  pl.BlockSpec((1, width), const)],
        out_specs=[pl.BlockSpec((None, tm, width), tok),
                   pl.BlockSpec((None, width, tm), lambda bi, i: (bi, 0, i)),
                   pl.BlockSpec((None, tm, width), tok)],
        out_shape=[jax.ShapeDtypeStruct((b, s, width), BF16),
                   jax.ShapeDtypeStruct((b, width, s), BF16),
                   jax.ShapeDtypeStruct((b, s, width), BF16)],
        compiler_params=_params(("parallel", "parallel")),
        name="inproj_attn",
    )(x, nw, w_a, cos_t, sin_t, qw, kw)


ATTN_ROW_CHUNK = 512


def _attn_kernel(lam_ref, q_ref, kt_ref, v_ref, sw_ref, o_ref, qs_ref, m_ref, acc_ref, s0_ref, *, tq, tk):
    q = q_ref[...]
    lane = lax.broadcasted_iota(jnp.int32, q.shape, 1)
    zero = jnp.zeros_like(q)
    qs_ref[0:tq, :] = jnp.where(lane < HEAD_DIM, q, zero)
    qs_ref[tq:2 * tq, :] = jnp.where(lane < HEAD_DIM, zero, q)
    m_ref[...] = jnp.full(m_ref.shape, -jnp.inf, F32)
    acc_ref[...] = jnp.zeros(acc_ref.shape, F32)

    rc = min(ATTN_ROW_CHUNK, 2 * tq)
    n_chunks = 2 * tq // rc
    n_tiles = kt_ref.shape[1] // tk
    s0_ref[...] = _dot(qs_ref[0:rc, :], kt_ref[:, 0:tk])

    def key_tile(j, carry):
        off = pl.multiple_of(j * tk, tk)
        v = v_ref[pl.ds(off, tk), :]
        v_ones = jnp.concatenate([v, jnp.ones_like(v)], axis=1)
        kt = kt_ref[:, pl.ds(off, tk)]
        scores = [None] * n_chunks
        scores[0] = s0_ref[...]
        for c in range(n_chunks):
            rows = slice(c * rc, (c + 1) * rc)
            if c + 1 < n_chunks:
                scores[c + 1] = _dot(qs_ref[(c + 1) * rc:(c + 2) * rc, :], kt)
            else:
                off_next = pl.multiple_of(jnp.minimum(j + 1, n_tiles - 1) * tk, tk)
                s0_ref[...] = _dot(qs_ref[0:rc, :], kt_ref[:, pl.ds(off_next, tk)])
            s = scores[c]
            m_prev = m_ref[rows, :]
            m_new = jnp.maximum(m_prev, jnp.max(s, axis=1, keepdims=True))
            alpha = jnp.exp2(m_prev - m_new)
            p = jnp.exp2((s - jnp.concatenate([m_new] * (tk // LANES), axis=1)).astype(BF16))
            acc_ref[rows, :] = (jnp.concatenate([alpha, alpha], axis=1) * acc_ref[rows, :]
                                + _dot(p, v_ones))
            m_ref[rows, :] = m_new
        return carry

    lax.fori_loop(0, n_tiles, key_tile, 0)

    lv = lam_ref[...]
    lam = (jnp.exp(jnp.sum(lv[0:1] * lv[1:2], axis=1, keepdims=True))
           - jnp.exp(jnp.sum(lv[2:3] * lv[3:4], axis=1, keepdims=True)) + LAMBDA_INIT)
    acc = acc_ref[...]
    o = (acc[0:tq, 0:LANES] / acc[0:tq, LANES:2 * LANES]
         - lam * (acc[tq:2 * tq, 0:LANES] / acc[tq:2 * tq, LANES:2 * LANES]))
    ms = jnp.mean(o * o, axis=-1, keepdims=True)
    o = o * lax.rsqrt(ms + NORM_EPS) * sw_ref[...] * (1.0 - LAMBDA_INIT)
    o_ref[...] = o.astype(o_ref.dtype)


def _attention(lam_rows, q, kt, v, subln_w, tq, tk):
    b, s, width = q.shape
    heads = width // LANES
    return pl.pallas_call(
        functools.partial(_attn_kernel, tq=tq, tk=tk),
        grid=(b, heads, s // tq),
        in_specs=[pl.BlockSpec((4, HEAD_DIM), lambda bi, h, qi: (0, 0)),
                  pl.BlockSpec((None, tq, LANES), lambda bi, h, qi: (bi, qi, h)),
                  pl.BlockSpec((None, LANES, s), lambda bi, h, qi: (bi, h, 0)),
                  pl.BlockSpec((None, s, LANES), lambda bi, h, qi: (bi, 0, h)),
                  pl.BlockSpec((1, LANES), lambda bi, h, qi: (0, 0))],
        out_specs=pl.BlockSpec((None, tq, LANES), lambda bi, h, qi: (bi, qi, h)),
        out_shape=jax.ShapeDtypeStruct((b, s, width), BF16),
        scratch_shapes=[pltpu.VMEM((2 * tq, LANES), BF16),
                        pltpu.VMEM((2 * tq, LANES), F32),
                        pltpu.VMEM((2 * tq, 2 * LANES), F32),
                        pltpu.VMEM((min(ATTN_ROW_CHUNK, 2 * tq), tk), F32)],
        compiler_params=_params(("parallel", "parallel", "arbitrary")),
        name="diff_attention",
    )(lam_rows, q, kt, v, subln_w)


HALO = 16


def _rwkv_prep_kernel(z_ref, zp_ref, zn_ref, mu_ref, w0_ref, w2_ref, a0_ref, a2_ref, g2_ref,
                      kk_w_ref, ka_w_ref, rk_ref,
                      r_ref, v_ref, kk_ref, kd0_ref, kd1_ref, ka0_ref, ka1_ref, lw0_ref, lw1_ref,
                      g_ref, bv_ref, *, c):
    i = pl.program_id(1)
    z = z_ref[...].astype(F32)
    tm = z.shape[0]
    row = lax.broadcasted_iota(jnp.int32, z.shape, 0)
    prev_row = jnp.where(i == 0, 0.0, zp_ref[HALO - 1:HALO, :].astype(F32))
    next_row = jnp.where(i == pl.num_programs(1) - 1, 0.0, zn_ref[0:1, :].astype(F32))
    z_up = jnp.where(row == 0, prev_row, pltpu.roll(z, 1, 0))
    z_dn = jnp.where(row == tm - 1, next_row, pltpu.roll(z, tm - 1, 0))
    zs = z + mu_ref[...] * (0.5 * (z_up + z_dn) - z)

    r = zs[:, 0:c]
    k = zs[:, c:2 * c]
    v = zs[:, 2 * c:3 * c]
    wl = zs[:, 3 * c:3 * c + LANES]
    al = zs[:, 3 * c + LANES:3 * c + 2 * LANES]
    gl = zs[:, 3 * c + 2 * LANES:3 * c + 3 * LANES]

    lane = lax.broadcasted_iota(jnp.int32, wl.shape, 1)
    twl = jnp.tanh(wl)
    e, et = _seg_mats(c, HEAD_DIM)

    kk = k * kk_w_ref[...]
    kk = kk * lax.rsqrt(_seg_sum(kk * kk, e, et) + 1e-12)
    kk_ref[...] = kk.astype(kk_ref.dtype)
    r_ref[...] = r.astype(r_ref.dtype)
    v_ref[...] = v.astype(v_ref.dtype)
    g_ref[...] = _dot(_sigmoid(gl).astype(BF16), g2_ref[...]).astype(g_ref.dtype)

    ksum = jnp.zeros_like(k)
    for d, (kd_ref, ka_ref, lw_ref) in enumerate(((kd0_ref, ka0_ref, lw0_ref), (kd1_ref, ka1_ref, lw1_ref))):
        sel = (lane < HEAD_DIM) if d == 0 else (lane >= HEAD_DIM)
        wpre = _dot(jnp.where(sel, twl, 0.0).astype(BF16), w2_ref[...]) + w0_ref[d:d + 1, :]
        lw_ref[...] = (-math.exp(-0.5)) * _sigmoid(wpre)
        a = _sigmoid(_dot(jnp.where(sel, al, 0.0).astype(BF16), a2_ref[...]) + a0_ref[d:d + 1, :])
        kd = k * (1.0 + (a - 1.0) * ka_w_ref[...])
        kd_ref[...] = kd.astype(kd_ref.dtype)
        ka_ref[...] = (kk * a).astype(ka_ref.dtype)
        ksum = ksum + kd
    bonus = _seg_sum(r * ksum * rk_ref[...], e, et, signed=True)
    bv_ref[...] = (bonus * v).astype(bv_ref.dtype)


def _rwkv_prep(zr, mu, w0, w2, a0, a2, g2, kk_w, ka_w, rk, c, tm):
    b, s, cols = zr.shape
    nh = tm // HALO
    last = s // HALO - 1
    tok = lambda bi, i: (bi, i, 0)
    const = lambda bi, i: (0, 0)
    out = lambda dt: jax.ShapeDtypeStruct((b, s, c), dt)
    ospec = pl.BlockSpec((None, tm, c), tok)
    return pl.pallas_call(
        functools.partial(_rwkv_prep_kernel, c=c),
        grid=(b, s // tm),
        in_specs=[pl.BlockSpec((None, tm, cols), tok),
                  pl.BlockSpec((None, HALO, cols), lambda bi, i: (bi, jnp.maximum(i * nh - 1, 0), 0)),
                  pl.BlockSpec((None, HALO, cols), lambda bi, i: (bi, jnp.minimum((i + 1) * nh, last), 0)),
                  pl.BlockSpec((1, cols), const),
                  pl.BlockSpec((2, c), const),
                  pl.BlockSpec((LANES, c), const),
                  pl.BlockSpec((2, c), const),
                  pl.BlockSpec((LANES, c), const),
                  pl.BlockSpec((LANES, c), const),
                  pl.BlockSpec((1, c), const),
                  pl.BlockSpec((1, c), const),
                  pl.BlockSpec((1, c), const)],
        out_specs=[ospec] * 11,
        out_shape=[out(BF16)] * 7 + [out(F32)] * 2 + [out(BF16)] * 2,
        compiler_params=_params(("parallel", "parallel")),
        name="rwkv_prep",
    )(zr, zr, zr, mu, w0, w2, a0, a2, g2, kk_w, ka_w, rk)


def _wkv_scan_kernel(lwf_ref, rf_ref, vf_ref, kkf_ref, kaf_ref, kdf_ref,
                     lwb_ref, rb_ref, vb_ref, kkb_ref, kab_ref, kdb_ref,
                     yf_ref, yb_ref, st_ref):
    L = SCAN_CHUNK
    P2 = 2 * L
    ci = pl.program_id(1)

    @pl.when(ci == 0)
    def _():
        st_ref[...] = jnp.zeros(st_ref.shape, F32)

    row = lax.broadcasted_iota(jnp.int32, (P2, P2), 0)
    col = lax.broadcasted_iota(jnp.int32, (P2, P2), 1)
    same_head = (row // L) == (col // L)
    rt = row % L
    ct = col % L
    eye = row == col
    ti = lax.broadcasted_iota(jnp.int32, (L, L), 0)
    tj = lax.broadcasted_iota(jnp.int32, (L, L), 1)
    lane_a = lax.broadcasted_iota(jnp.int32, (L, LANES), 1) < HEAD_DIM

    def expand(x):
        return jnp.concatenate([jnp.where(lane_a, x, 0.0), jnp.where(lane_a, 0.0, x)], axis=0)

    units = []
    dirs = ((lwf_ref, rf_ref, vf_ref, kkf_ref, kaf_ref, kdf_ref, yf_ref),
            (lwb_ref, rb_ref, vb_ref, kkb_ref, kab_ref, kdb_ref, yb_ref))
    for d, (lw_ref, r_ref, v_ref, kk_ref, ka_ref, kd_ref, y_ref) in enumerate(dirs):
        if d == 0:
            m_strict = same_head & (rt > ct)
            m_incl = same_head & (rt >= ct)
            tri = jnp.where(tj <= ti, 1.0, 0.0).astype(BF16)
        else:
            m_strict = same_head & (rt < ct)
            m_incl = same_head & (rt <= ct)
            tri = jnp.where(tj >= ti, 1.0, 0.0).astype(BF16)
        m_level = []
        for lvl in range(L.bit_length() - 1):
            joined = same_head & ((rt >> (lvl + 1)) == (ct >> (lvl + 1)))
            r_bit = (rt >> lvl) & 1
            c_bit = (ct >> lvl) & 1
            later_row = (r_bit == 1) & (c_bit == 0) if d == 0 else (r_bit == 0) & (c_bit == 1)
            m_level.append(joined & later_row)
        lw = lw_ref[...]
        hi = lw.astype(BF16)
        r1 = lw - hi.astype(F32)
        mid = r1.astype(BF16)
        lo = (r1 - mid.astype(F32)).astype(BF16)
        cum = _dot(tri, hi) + _dot(tri, mid) + _dot(tri, lo)
        cum_last = cum[L - 1:L, :] if d == 0 else cum[0:1, :]
        ec = jnp.exp(cum)
        enc = jnp.exp(-cum)
        p_last = jnp.exp(cum_last)
        a_t = -kk_ref[...].astype(F32) * jnp.exp(cum - lw)
        r_t = r_ref[...].astype(F32) * ec
        b_t = ka_ref[...].astype(F32) * enc
        k_t = kd_ref[...].astype(F32) * enc
        b_h = b_t * p_last
        k_h = k_t * p_last
        vv = v_ref[...].astype(F32)

        for p in range(lw.shape[1] // LANES):
            sl = slice(p * LANES, (p + 1) * LANES)
            units.append(dict(
                d=d, p=p, sl=sl, y_ref=y_ref, m_strict=m_strict, m_incl=m_incl, m_level=m_level,
                ea=expand(a_t[:, sl]), er=expand(r_t[:, sl]), vexp=expand(vv[:, sl]),
                bk2=jnp.concatenate([b_t[:, sl]] * 2 + [k_t[:, sl]] * 2, axis=0).astype(BF16),
                bh=b_h[:, sl], kh=k_h[:, sl], p_last=p_last[:, sl]))

    for u in units:
        g = _dot_nt(jnp.concatenate([u["ea"], u["er"]], axis=0).astype(BF16), u["bk2"])
        u["xab"] = jnp.where(u["m_strict"], g[0:P2, 0:P2], 0.0)
        u["xak"] = jnp.where(u["m_strict"], g[0:P2, P2:2 * P2], 0.0).astype(BF16)
        u["xr"] = jnp.concatenate([jnp.where(u["m_incl"], g[P2:2 * P2, 0:P2], 0.0),
                                   jnp.where(u["m_incl"], g[P2:2 * P2, P2:2 * P2], 0.0)], axis=1).astype(BF16)
        u["tinv"] = jnp.where(eye, 1.0, jnp.where(u["m_level"][0], u["xab"], 0.0))
    for lvl in range(1, L.bit_length() - 1):
        for u in units:
            xoff = jnp.where(u["m_level"][lvl], u["xab"], 0.0).astype(BF16)
            u["tx"] = _dot(u["tinv"].astype(BF16), xoff).astype(BF16)
        for u in units:
            u["tinv"] = u["tinv"] + _dot(u["tx"], u["tinv"].astype(BF16))
    for u in units:
        u["xv"] = _dot(u["xak"], u["vexp"].astype(BF16))
    for u in units:
        u["wu0"] = _dot(u["tinv"].astype(BF16), jnp.concatenate([u["ea"], u["xv"]], axis=1).astype(BF16))
        u["bkt"] = jnp.concatenate([expand(u["bh"]).T, expand(u["kh"]).T], axis=1).astype(BF16)
        u["p_col"] = jnp.broadcast_to(u["p_last"], (P2, LANES)).T
    for u in units:
        u["st"] = st_ref[u["d"], u["p"]]
        u["ws"] = _dot(jnp.concatenate([u["wu0"][:, 0:LANES], u["er"]], axis=0).astype(BF16), u["st"].astype(BF16))
    for u in units:
        uu = u["ws"][0:P2] + u["wu0"][:, LANES:2 * LANES]
        u["uv"] = jnp.concatenate([uu, u["vexp"]], axis=0).astype(BF16)
    for u in units:
        yexp = u["ws"][P2:2 * P2] + _dot(u["xr"], u["uv"])
        u["y_ref"][:, u["sl"]] = (yexp[0:L] + yexp[L:P2]).astype(u["y_ref"].dtype)
    for u in units:
        st_ref[u["d"], u["p"]] = u["p_col"] * u["st"] + _dot(u["bkt"], u["uv"])


def _wkv_scan(lw0, lw1, r, v, kk, ka0, ka1, kd0, kd1):
    b, s, c = r.shape
    L = SCAN_CHUNK
    nc = s // L
    fwd = lambda bi, i: (bi, i, 0)
    bwd = lambda bi, i: (bi, nc - 1 - i, 0)
    fs = pl.BlockSpec((None, L, c), fwd)
    bs = pl.BlockSpec((None, L, c), bwd)
    return pl.pallas_call(
        _wkv_scan_kernel,
        grid=(b, nc),
        in_specs=[fs] * 6 + [bs] * 6,
        out_specs=[fs, bs],
        out_shape=[jax.ShapeDtypeStruct((b, s, c), F32)] * 2,
        scratch_shapes=[pltpu.VMEM((2, c // LANES, LANES, LANES), F32)],
        compiler_params=_params(("parallel", "arbitrary")),
        name="wkv_scan",
    )(lw0, r, v, kk, ka0, kd0, lw1, r, v, kk, ka1, kd1)


def _post_kernel(x_ref, oa_ref, yf_ref, yb_ref, g_ref, bv_ref, zg_ref, lnw_ref, lnb_ref,
                 woa_ref, wor_ref, wout_ref, n2w_ref, wq_ref, x1_ref, xn_ref, qp_ref):
    y = yf_ref[...] + yb_ref[...]
    c = y.shape[1]
    e, et = _seg_mats(c, HEAD_DIM)
    mu = _seg_sum(y, e, et, signed=True) * (1.0 / HEAD_DIM)
    dlt = y - mu
    var = _seg_sum(dlt * dlt, e, et) * (1.0 / HEAD_DIM)
    yn = dlt * lax.rsqrt(var + RWKV_LN_EPS) * lnw_ref[...] + lnb_ref[...]
    out = ((yn + bv_ref[...].astype(F32)) * g_ref[...].astype(F32)).astype(BF16)
    h_rwkv = _dot(out, wor_ref[...])
    h_attn = _dot(oa_ref[...], woa_ref[...])
    zg = zg_ref[...].astype(F32)
    merged = _sigmoid(zg[:, 0:c]) * h_attn + _sigmoid(zg[:, c:2 * c]) * h_rwkv
    x1 = x_ref[...] + _dot(merged.astype(BF16), wout_ref[...])
    x1_ref[...] = x1
    ms = jnp.mean(x1 * x1, axis=-1, keepdims=True)
    xn = (x1 * lax.rsqrt(ms + NORM_EPS) * n2w_ref[...]).astype(BF16)
    xn_ref[...] = xn
    qp = _dot(xn, wq_ref[...]).astype(BF16)
    for j in range(qp_ref.shape[0]):
        qp_ref[j] = qp[:, j * LANES:(j + 1) * LANES]


def _post(x, oa, yf, yb, g, bv, zg, lnw, lnb, woa, wor, wout, n2w, wq, tm):
    t, d = x.shape
    nq = wq.shape[1] // LANES
    tok = lambda i: (i, 0)
    const = lambda i: (0, 0)
    tspec = lambda w: pl.BlockSpec((tm, w), tok)
    cspec = lambda a: pl.BlockSpec(a.shape, const)
    return pl.pallas_call(
        _post_kernel,
        grid=(t // tm,),
        in_specs=[tspec(d), tspec(d), tspec(d), tspec(d), tspec(d), tspec(d), tspec(2 * d),
                  cspec(lnw), cspec(lnb), cspec(woa), cspec(wor), cspec(wout), cspec(n2w), cspec(wq)],
        out_specs=[tspec(d), tspec(d), pl.BlockSpec((nq, tm, LANES), lambda i: (0, i, 0))],
        out_shape=[jax.ShapeDtypeStruct((t, d), F32), jax.ShapeDtypeStruct((t, d), BF16),
                   jax.ShapeDtypeStruct((nq, t, LANES), BF16)],
        compiler_params=_params(("parallel",)),
        name="post_merge",
    )(x, oa, yf, yb, g, bv, zg, lnw, lnb, woa, wor, wout, n2w, wq)


SUBLANES = 8


def _sort_pairs(n):
    def merge(lo, hi, r):
        step = r * 2
        if step < hi - lo:
            yield from merge(lo, hi, step)
            yield from merge(lo + r, hi, step)
            for i in range(lo + r, hi - r, step):
                yield (i, i + r)
        else:
            yield (lo, lo + r)

    def sort(lo, hi):
        if hi - lo >= 1:
            mid = lo + (hi - lo) // 2
            yield from sort(lo, mid)
            yield from sort(mid + 1, hi)
            yield from merge(lo, hi, 1)

    return list(sort(0, n - 1))


def _exchange(xs, i, j):
    a, b = xs[i], xs[j]
    if b is None:
        return
    if a is None:
        xs[i], xs[j] = b, None
        return
    xs[i], xs[j] = jnp.maximum(a, b), jnp.minimum(a, b)


def _top_sorted(blocks, k):
    xs = list(blocks)
    for i, j in _sort_pairs(k):
        _exchange(xs, i, j)
    shift = SUBLANES // 2
    while shift >= 1:
        ys = [None if x is None else pltpu.roll(x, shift, 0) for x in xs]
        zs = []
        for i in range(k):
            a, b = xs[i], ys[k - 1 - i]
            zs.append(b if a is None else (a if b is None else jnp.maximum(a, b)))
        step = k // 2
        while step >= 1:
            for i in range(k):
                if i & step == 0:
                    _exchange(zs, i, i + step)
            step //= 2
        xs = zs
        shift //= 2
    return [x[0:1, :] for x in xs]


def _row_blocks(x):
    return [x[r:r + SUBLANES, :] for r in range(0, x.shape[0], SUBLANES)]


def _peer_select(s0, s1, k):
    a = _top_sorted(_row_blocks(s0), k)
    b = _top_sorted(_row_blocks(s1), k)
    amat = jnp.concatenate(a, axis=0)
    bmat = jnp.concatenate(b, axis=0)
    h8 = k // 2
    cand = ([a[0] + bmat[0:h8], a[0] + bmat[h8:k], a[1] + bmat[0:h8], amat[h8:k] + b[0]]
            + [a[i] + bmat[0:h8] for i in range(2, h8)])
    best = _top_sorted(cand + [None] * (k - len(cand)), k)
    z = jnp.zeros_like(best[0])
    for val in best:
        z = z + jnp.exp(val - best[0])
    return amat, bmat, best[k - 1], z


def _peer_kernel(xn_ref, qp_ref, sk_ref, u_ref, vt_ref, x1_ref, o_ref,
                 cnt_ref, e0_ref, rank_ref, e1_ref, acc_ref):
    jb = pl.program_id(1)
    tt = xn_ref.shape[0]
    K = PEER_TOPK
    rows_per_blk = u_ref.shape[0] // PEER_NKEYS

    @pl.when(jb == 0)
    def _():
        acc_ref[...] = jnp.zeros(acc_ref.shape, F32)

        def head_body(h, carry):
            s0 = _dot_nt(sk_ref[2 * h], qp_ref[2 * h])
            s1 = _dot_nt(sk_ref[2 * h + 1], qp_ref[2 * h + 1])

            amat, bmat, thr, z = _peer_select(s0, s1, K)
            a0 = amat[0:1]
            top_cnt = jnp.zeros(amat.shape, F32)
            for kk in range(K):
                top_cnt = top_cnt + jnp.where(amat + bmat[kk:kk + 1] >= thr, 1.0, 0.0)
            cnt = jnp.zeros(s0.shape, F32)
            rank = jnp.full(s1.shape, float(PEER_NKEYS), F32)
            for kk in reversed(range(K)):
                cnt = jnp.where(s0 == amat[kk:kk + 1], top_cnt[kk:kk + 1], cnt)
                rank = jnp.where(s1 == bmat[kk:kk + 1], float(kk + 1), rank)
            cnt_ref[h] = cnt
            e0_ref[h] = jnp.exp(s0 - a0) / z
            rank_ref[h] = rank.astype(BF16)
            e1_ref[h] = jnp.exp(s1 - bmat[0:1]).astype(BF16)
            return carry

        lax.fori_loop(0, PEER_HEADS, head_body, 0)

    sub = PEER_SUB_ROWS * PEER_NKEYS
    xn = xn_ref[...]
    n_sub = u_ref.shape[0] // sub
    hpres = [None] * n_sub
    hpres[0] = _dot_nt(u_ref[0:sub, :], xn)
    zero = jnp.zeros((), BF16)
    hgs = []
    for c in range(n_sub):
        if c + 1 < n_sub:
            hpres[c + 1] = _dot_nt(u_ref[(c + 1) * sub:(c + 2) * sub, :], xn)
        g_rows = []
        for il in range(c * PEER_SUB_ROWS, (c + 1) * PEER_SUB_ROWS):
            i = jb * rows_per_blk + il
            acc = None
            for h in range(PEER_HEADS):
                crow = cnt_ref[h, pl.ds(i, 1), :].astype(BF16)
                e0row = e0_ref[h, pl.ds(i, 1), :].astype(BF16)
                contrib = jnp.where(rank_ref[h] <= crow, e0row * e1_ref[h], zero)
                acc = contrib if acc is None else acc + contrib
            g_rows.append(acc)
        gates = jnp.concatenate(g_rows, axis=0)
        hpre = hpres[c]
        gelu = 0.5 * hpre * (1.0 + lax.erf(hpre * (2.0 ** -0.5)))
        hgs.append(gelu.astype(BF16) * gates)
        if len(hgs) == PEER_OUT_GROUP:
            lo = (c + 1 - PEER_OUT_GROUP) * sub
            acc_ref[...] += _dot(vt_ref[:, lo:(c + 1) * sub], jnp.concatenate(hgs, axis=0))
            hgs = []

    @pl.when(jb == pl.num_programs(1) - 1)
    def _():
        o_ref[...] = x1_ref[...] + acc_ref[...].T


def _peer(xn, qp, sk, u, vt, x1, tt, eb):
    t, d = xn.shape
    n_exp = u.shape[0]
    nq = qp.shape[0]
    return pl.pallas_call(
        _peer_kernel,
        grid=(t // tt, n_exp // eb),
        in_specs=[pl.BlockSpec((tt, d), lambda i, j: (i, 0)),
                  pl.BlockSpec((nq, tt, LANES), lambda i, j: (0, i, 0)),
                  pl.BlockSpec(sk.shape, lambda i, j: (0, 0, 0)),
                  pl.BlockSpec((eb, d), lambda i, j: (j, 0)),
                  pl.BlockSpec((d, eb), lambda i, j: (0, j)),
                  pl.BlockSpec((tt, d), lambda i, j: (i, 0))],
        out_specs=pl.BlockSpec((tt, d), lambda i, j: (i, 0)),
        out_shape=jax.ShapeDtypeStruct((t, d), F32),
        scratch_shapes=[pltpu.VMEM((PEER_HEADS, PEER_NKEYS, tt), F32),
                        pltpu.VMEM((PEER_HEADS, PEER_NKEYS, tt), F32),
                        pltpu.VMEM((PEER_HEADS, PEER_NKEYS, tt), BF16),
                        pltpu.VMEM((PEER_HEADS, PEER_NKEYS, tt), BF16),
                        pltpu.VMEM((d, tt), F32)],
        compiler_params=_params(("parallel", "arbitrary")),
        name="peer",
    )(xn, qp, sk, u, vt, x1)


def _rope_tables(s):
    inv = 1.0 / (ROPE_THETA ** (jnp.arange(0, HEAD_DIM, 2, dtype=F32) / HEAD_DIM))
    ang = jnp.arange(s, dtype=F32)[:, None] * inv[None, :]
    cos, sin = jnp.cos(ang), jnp.sin(ang)
    reps = LANES // HEAD_DIM
    cos_t = jnp.tile(jnp.concatenate([cos, cos], axis=1), (1, reps))
    sin_t = jnp.tile(jnp.concatenate([-sin, sin], axis=1), (1, reps))
    return cos_t, sin_t


def _tile(n, pref):
    return pref if n % pref == 0 else n


def _tiling(t, s):
    return dict(
        proj=_tile(t, 512),
        inproj_attn=_tile(s, 512),
        attn_q=_tile(s, 2048),
        attn_k=_tile(s, 2048),
        rwkv_prep=_tile(s, 256),
        post=_tile(t, 256),
        peer_tokens=_tile(t, 512),
        peer_experts=2048,
    )


def _trunk(x, P):
    b, s, d = x.shape
    t = b * s
    c = d
    tiles = _tiling(t, s)
    xf = x.reshape(t, d)
    zg = _norm_matmul(xf, P["norm1_w"], P["w_g"], tiles["proj"])

    cos_t, sin_t = _rope_tables(s)
    q, kt, va = _inproj_attn(x, P["norm1_w"], P["w_a"], cos_t, sin_t, P["q_norm_w"], P["k_norm_w"],
                             HEAD_DIM ** -0.5 * math.log2(math.e), tiles["inproj_attn"])
    oa = _attention(P["lam_rows"], q, kt, va, P["subln_w"], tiles["attn_q"], tiles["attn_k"])

    zr = _norm_matmul(xf, P["norm1_w"], P["w_r"], tiles["proj"]).reshape(b, s, -1)
    (r, v, kk, kd0, kd1, ka0, ka1, lw0, lw1, g, bv) = _rwkv_prep(
        zr, P["shift_mu"], P["decay_w0"], P["decay_w2"], P["iclr_a0"], P["iclr_a2"],
        P["gate_g2"], P["k_k"], P["k_a"], P["r_k"], c, tiles["rwkv_prep"])
    yf, yb = _wkv_scan(lw0, lw1, r, v, kk, ka0, ka1, kd0, kd1)

    flat = lambda a: a.reshape(t, a.shape[-1])
    x1, xn, qp = _post(xf, flat(oa), flat(yf), flat(yb), flat(g), flat(bv), zg, P["lnx_w"], P["lnx_b"],
                       P["w_o_attn"], P["w_o_rwkv"], P["w_out"], P["norm2_w"], P["peer_wq"], tiles["post"])
    y = _peer(xn, qp, P["peer_subkeys"], P["peer_u"], P["peer_vt"], x1, tiles["peer_tokens"],
              tiles["peer_experts"])
    return y.reshape(b, s, d)


def kernel(x_prompt, x_sample, norm1_w, w_in, q_norm_w, k_norm_w, lambda_q1, lambda_k1, lambda_q2, lambda_k2, subln_w, w_o_attn, shift_mu, decay_w0, decay_w2, iclr_a0, iclr_a2, gate_g2, k_k, k_a, r_k, lnx_w, lnx_b, w_o_rwkv, w_out, norm2_w, peer_wq, peer_subkeys, peer_u, peer_v):
    d = x_prompt.shape[-1]
    c = d
    att_cols = 3 * d
    rwkv_cols = shift_mu.shape[-1]
    l = 0
    w = w_in[l].astype(BF16)
    nsub = d // HEAD_DIM
    P = {
        "norm1_w": norm1_w[l][None, :],
        "w_a": w[:, :att_cols],
        "w_r": w[:, att_cols:att_cols + rwkv_cols],
        "w_g": w[:, att_cols + rwkv_cols:],
        "q_norm_w": jnp.tile(q_norm_w[l], nsub)[None, :],
        "k_norm_w": jnp.tile(k_norm_w[l], nsub)[None, :],
        "lam_rows": jnp.stack([lambda_q1[l], lambda_k1[l], lambda_q2[l], lambda_k2[l]]),
        "subln_w": subln_w[l][None, :],
        "w_o_attn": w_o_attn[l].astype(BF16),
        "shift_mu": shift_mu[l][None, :],
        "decay_w0": decay_w0[l],
        "decay_w2": decay_w2[l].reshape(-1, c).astype(BF16),
        "iclr_a0": iclr_a0[l],
        "iclr_a2": iclr_a2[l].reshape(-1, c).astype(BF16),
        "gate_g2": gate_g2[l].astype(BF16),
        "k_k": k_k[l][None, :],
        "k_a": k_a[l][None, :],
        "r_k": r_k[l].reshape(1, c),
        "lnx_w": lnx_w[l][None, :],
        "lnx_b": lnx_b[l][None, :],
        "w_o_rwkv": w_o_rwkv[l].astype(BF16),
        "w_out": w_out[l].astype(BF16),
        "norm2_w": norm2_w[l][None, :],
        "peer_wq": peer_wq[l].astype(BF16),
        "peer_subkeys": peer_subkeys[l].reshape(-1, PEER_NKEYS, peer_subkeys.shape[-1]).astype(BF16),
        "peer_u": peer_u[l].astype(BF16),
        "peer_vt": peer_v[l].astype(BF16).T,
    }
    return (_trunk(x_prompt, P), _trunk(x_sample, P))
```

```python
import functools
import math

import jax
import jax.numpy as jnp
from jax import lax
from jax.experimental import pallas as pl
from jax.experimental.pallas import tpu as pltpu

F32 = jnp.float32
BF16 = jnp.bfloat16

LANES = 128
HEAD_DIM = 64
ATT_HEADS = 8
NORM_EPS = 1e-6
RWKV_LN_EPS = 64e-5
ROPE_THETA = 10000.0
LAMBDA_INIT = 0.8 - 0.6 * math.exp(-0.3 * 0)
SCAN_CHUNK = 64
PEER_NKEYS = 128
PEER_TOPK = 16
PEER_HEADS = 8
PEER_SUB_ROWS = 2
PEER_OUT_GROUP = 8
VMEM_LIMIT = 56 * 1024 * 1024


def _params(sem):
    return pltpu.CompilerParams(dimension_semantics=sem, vmem_limit_bytes=VMEM_LIMIT)


def _dot(a, b):
    return jnp.dot(a, b, preferred_element_type=F32)


def _dot_nt(a, b):
    return lax.dot_general(a, b, (((1,), (1,)), ((), ())), preferred_element_type=F32)


def _split_dot(x, m):
    hi = x.astype(BF16)
    lo = (x - hi.astype(F32)).astype(BF16)
    return _dot(hi, m) + _dot(lo, m)


def _seg_mats(width, seg):
    lane = lax.broadcasted_iota(jnp.int32, (width, LANES), 0)
    grp = lax.broadcasted_iota(jnp.int32, (width, LANES), 1)
    e = jnp.where(lane // seg == grp, 1.0, 0.0).astype(BF16)
    grp_t = lax.broadcasted_iota(jnp.int32, (LANES, width), 0)
    lane_t = lax.broadcasted_iota(jnp.int32, (LANES, width), 1)
    et = jnp.where(lane_t // seg == grp_t, 1.0, 0.0).astype(BF16)
    return e, et


def _seg_sum(x, e, et, signed=False):
    totals = _split_dot(x, e) if signed else _dot(x.astype(BF16), e)
    return _split_dot(totals, et)


def _sigmoid(x):
    return 1.0 / (1.0 + jnp.exp(-x))


def _norm_matmul_kernel(x_ref, nw_ref, w_ref, o_ref):
    x = x_ref[...]
    ms = jnp.mean(x * x, axis=-1, keepdims=True)
    h = (x * lax.rsqrt(ms + NORM_EPS) * nw_ref[...]).astype(BF16)
    o_ref[...] = _dot(h, w_ref[...]).astype(o_ref.dtype)


def _norm_matmul(x, nw, w, tm):
    t, d = x.shape
    n = w.shape[1]
    return pl.pallas_call(
        _norm_matmul_kernel,
        grid=(t // tm,),
        in_specs=[pl.BlockSpec((tm, d), lambda i: (i, 0)),
                  pl.BlockSpec((1, d), lambda i: (0, 0)),
                  pl.BlockSpec((d, n), lambda i: (0, 0))],
        out_specs=pl.BlockSpec((tm, n), lambda i: (i, 0)),
        out_shape=jax.ShapeDtypeStruct((t, n), BF16),
        compiler_params=_params(("parallel",)),
        name="norm_matmul",
    )(x, nw, w)


def _norm_rope(x, w, c, s, e, et, scale):
    width = x.shape[1]
    ms = _seg_sum(x * x, e, et) * (1.0 / HEAD_DIM)
    y = x * lax.rsqrt(ms + NORM_EPS) * w
    lane = lax.broadcasted_iota(jnp.int32, y.shape, 1)
    half = HEAD_DIM // 2
    partner = jnp.where(lane % HEAD_DIM < half, pltpu.roll(y, width - half, 1), pltpu.roll(y, half, 1))
    return (y * c + partner * s) * scale


def _inproj_attn_kernel(x_ref, nw_ref, w_ref, cos_ref, sin_ref, qw_ref, kw_ref, q_ref, kt_ref, v_ref, *, q_scale):
    x = x_ref[...]
    ms = jnp.mean(x * x, axis=-1, keepdims=True)
    h = (x * lax.rsqrt(ms + NORM_EPS) * nw_ref[...]).astype(BF16)
    z = _dot(h, w_ref[...])
    width = q_ref.shape[1]
    reps = width // LANES
    e, et = _seg_mats(width, HEAD_DIM)
    c = jnp.concatenate([cos_ref[...]] * reps, axis=1)
    s = jnp.concatenate([sin_ref[...]] * reps, axis=1)
    q_ref[...] = _norm_rope(z[:, 0:width], qw_ref[...], c, s, e, et, q_scale).astype(q_ref.dtype)
    k = _norm_rope(z[:, width:2 * width], kw_ref[...], c, s, e, et, 1.0)
    for j in range(reps):
        kt_ref[j * LANES:(j + 1) * LANES, :] = k[:, j * LANES:(j + 1) * LANES].T.astype(kt_ref.dtype)
    v_ref[...] = z[:, 2 * width:3 * width].astype(v_ref.dtype)


def _inproj_attn(x, nw, w_a, cos_t, sin_t, qw, kw, q_scale, tm):
    b, s, d = x.shape
    width = qw.shape[1]
    tok = lambda bi, i: (bi, i, 0)
    const = lambda bi, i: (0, 0)
    return pl.pallas_call(
        functools.partial(_inproj_attn_kernel, q_scale=q_scale),
        grid=(b, s // tm),
        in_specs=[pl.BlockSpec((None, tm, d), tok),
                  pl.BlockSpec((1, d), const),
                  pl.BlockSpec(w_a.shape, const),
                  pl.BlockSpec((tm, LANES), lambda bi, i: (i, 0)),
                  pl.BlockSpec((tm, LANES), lambda bi, i: (i, 0)),
                  pl.BlockSpec((1, width), const),
                  pl.BlockSpec((1, width), const)],
        out_specs=[pl.BlockSpec((None, tm, width), tok),
                   pl.BlockSpec((None, width, tm), lambda bi, i: (bi, 0, i)),
                   pl.BlockSpec((None, tm, width), tok)],
        out_shape=[jax.ShapeDtypeStruct((b, s, width), BF16),
                   jax.ShapeDtypeStruct((b, width, s), BF16),
                   jax.ShapeDtypeStruct((b, s, width), BF16)],
        compiler_params=_params(("parallel", "parallel")),
        name="inproj_attn",
    )(x, nw, w_a, cos_t, sin_t, qw, kw)


ATTN_ROW_CHUNK = 512


def _attn_kernel(lam_ref, q_ref, kt_ref, v_ref, sw_ref, o_ref, qs_ref, m_ref, acc_ref, s0_ref, *, tq, tk):
    q = q_ref[...]
    lane = lax.broadcasted_iota(jnp.int32, q.shape, 1)
    zero = jnp.zeros_like(q)
    qs_ref[0:tq, :] = jnp.where(lane < HEAD_DIM, q, zero)
    qs_ref[tq:2 * tq, :] = jnp.where(lane < HEAD_DIM, zero, q)
    m_ref[...] = jnp.full(m_ref.shape, -jnp.inf, F32)
    acc_ref[...] = jnp.zeros(acc_ref.shape, F32)

    rc = min(ATTN_ROW_CHUNK, 2 * tq)
    n_chunks = 2 * tq // rc
    n_tiles = kt_ref.shape[1] // tk
    s0_ref[...] = _dot(qs_ref[0:rc, :], kt_ref[:, 0:tk])

    def key_tile(j, carry):
        off = pl.multiple_of(j * tk, tk)
        v = v_ref[pl.ds(off, tk), :]
        v_ones = jnp.concatenate([v, jnp.ones_like(v)], axis=1)
        kt = kt_ref[:, pl.ds(off, tk)]
        scores = [None] * n_chunks
        scores[0] = s0_ref[...]
        for c in range(n_chunks):
            rows = slice(c * rc, (c + 1) * rc)
            if c + 1 < n_chunks:
                scores[c + 1] = _dot(qs_ref[(c + 1) * rc:(c + 2) * rc, :], kt)
            else:
                off_next = pl.multiple_of(jnp.minimum(j + 1, n_tiles - 1) * tk, tk)
                s0_ref[...] = _dot(qs_ref[0:rc, :], kt_ref[:, pl.ds(off_next, tk)])
            s = scores[c]
            m_prev = m_ref[rows, :]
            m_new = jnp.maximum(m_prev, jnp.max(s, axis=1, keepdims=True))
            alpha = jnp.exp2(m_prev - m_new)
            p = jnp.exp2((s - jnp.concatenate([m_new] * (tk // LANES), axis=1)).astype(BF16))
            acc_ref[rows, :] = (jnp.concatenate([alpha, alpha], axis=1) * acc_ref[rows, :]
                                + _dot(p, v_ones))
            m_ref[rows, :] = m_new
        return carry

    lax.fori_loop(0, n_tiles, key_tile, 0)

    lv = lam_ref[...]
    lam = (jnp.exp(jnp.sum(lv[0:1] * lv[1:2], axis=1, keepdims=True))
           - jnp.exp(jnp.sum(lv[2:3] * lv[3:4], axis=1, keepdims=True)) + LAMBDA_INIT)
    acc = acc_ref[...]
    o = (acc[0:tq, 0:LANES] / acc[0:tq, LANES:2 * LANES]
         - lam * (acc[tq:2 * tq, 0:LANES] / acc[tq:2 * tq, LANES:2 * LANES]))
    ms = jnp.mean(o * o, axis=-1, keepdims=True)
    o = o * lax.rsqrt(ms + NORM_EPS) * sw_ref[...] * (1.0 - LAMBDA_INIT)
    o_ref[...] = o.astype(o_ref.dtype)


def _attention(lam_rows, q, kt, v, subln_w, tq, tk):
    b, s, width = q.shape
    heads = width // LANES
    return pl.pallas_call(
        functools.partial(_attn_kernel, tq=tq, tk=tk),
        grid=(b, heads, s // tq),
        in_specs=[pl.BlockSpec((4, HEAD_DIM), lambda bi, h, qi: (0, 0)),
                  pl.BlockSpec((None, tq, LANES), lambda bi, h, qi: (bi, qi, h)),
                  pl.BlockSpec((None, LANES, s), lambda bi, h, qi: (bi, h, 0)),
                  pl.BlockSpec((None, s, LANES), lambda bi, h, qi: (bi, 0, h)),
                  pl.BlockSpec((1, LANES), lambda bi, h, qi: (0, 0))],
        out_specs=pl.BlockSpec((None, tq, LANES), lambda bi, h, qi: (bi, qi, h)),
        out_shape=jax.ShapeDtypeStruct((b, s, width), BF16),
        scratch_shapes=[pltpu.VMEM((2 * tq, LANES), BF16),
                        pltpu.VMEM((2 * tq, LANES), F32),
                        pltpu.VMEM((2 * tq, 2 * LANES), F32),
                        pltpu.VMEM((min(ATTN_ROW_CHUNK, 2 * tq), tk), F32)],
        compiler_params=_params(("parallel", "parallel", "arbitrary")),
        name="diff_attention",
    )(lam_rows, q, kt, v, subln_w)


HALO = 16


def _rwkv_prep_kernel(z_ref, zp_ref, zn_ref, mu_ref, w0_ref, w2_ref, a0_ref, a2_ref, g2_ref,
                      kk_w_ref, ka_w_ref, rk_ref,
                      r_ref, v_ref, kk_ref, kd0_ref, kd1_ref, ka0_ref, ka1_ref, lw0_ref, lw1_ref,
                      g_ref, bv_ref, *, c):
    i = pl.program_id(1)
    z = z_ref[...].astype(F32)
    tm = z.shape[0]
    row = lax.broadcasted_iota(jnp.int32, z.shape, 0)
    prev_row = jnp.where(i == 0, 0.0, zp_ref[HALO - 1:HALO, :].astype(F32))
    next_row = jnp.where(i == pl.num_programs(1) - 1, 0.0, zn_ref[0:1, :].astype(F32))
    z_up = jnp.where(row == 0, prev_row, pltpu.roll(z, 1, 0))
    z_dn = jnp.where(row == tm - 1, next_row, pltpu.roll(z, tm - 1, 0))
    zs = z + mu_ref[...] * (0.5 * (z_up + z_dn) - z)

    r = zs[:, 0:c]
    k = zs[:, c:2 * c]
    v = zs[:, 2 * c:3 * c]
    wl = zs[:, 3 * c:3 * c + LANES]
    al = zs[:, 3 * c + LANES:3 * c + 2 * LANES]
    gl = zs[:, 3 * c + 2 * LANES:3 * c + 3 * LANES]

    lane = lax.broadcasted_iota(jnp.int32, wl.shape, 1)
    twl = jnp.tanh(wl)
    e, et = _seg_mats(c, HEAD_DIM)

    kk = k * kk_w_ref[...]
    kk = kk * lax.rsqrt(_seg_sum(kk * kk, e, et) + 1e-12)
    kk_ref[...] = kk.astype(kk_ref.dtype)
    r_ref[...] = r.astype(r_ref.dtype)
    v_ref[...] = v.astype(v_ref.dtype)
    g_ref[...] = _dot(_sigmoid(gl).astype(BF16), g2_ref[...]).astype(g_ref.dtype)

    ksum = jnp.zeros_like(k)
    for d, (kd_ref, ka_ref, lw_ref) in enumerate(((kd0_ref, ka0_ref, lw0_ref), (kd1_ref, ka1_ref, lw1_ref))):
        sel = (lane < HEAD_DIM) if d == 0 else (lane >= HEAD_DIM)
        wpre = _dot(jnp.where(sel, twl, 0.0).astype(BF16), w2_ref[...]) + w0_ref[d:d + 1, :]
        lw_ref[...] = (-math.exp(-0.5)) * _sigmoid(wpre)
        a = _sigmoid(_dot(jnp.where(sel, al, 0.0).astype(BF16), a2_ref[...]) + a0_ref[d:d + 1, :])
        kd = k * (1.0 + (a - 1.0) * ka_w_ref[...])
        kd_ref[...] = kd.astype(kd_ref.dtype)
        ka_ref[...] = (kk * a).astype(ka_ref.dtype)
        ksum = ksum + kd
    bonus = _seg_sum(r * ksum * rk_ref[...], e, et, signed=True)
    bv_ref[...] = (bonus * v).astype(bv_ref.dtype)


def _rwkv_prep(zr, mu, w0, w2, a0, a2, g2, kk_w, ka_w, rk, c, tm):
    b, s, cols = zr.shape
    nh = tm // HALO
    last = s // HALO - 1
    tok = lambda bi, i: (bi, i, 0)
    const = lambda bi, i: (0, 0)
    out = lambda dt: jax.ShapeDtypeStruct((b, s, c), dt)
    ospec = pl.BlockSpec((None, tm, c), tok)
    return pl.pallas_call(
        functools.partial(_rwkv_prep_kernel, c=c),
        grid=(b, s // tm),
        in_specs=[pl.BlockSpec((None, tm, cols), tok),
                  pl.BlockSpec((None, HALO, cols), lambda bi, i: (bi, jnp.maximum(i * nh - 1, 0), 0)),
                  pl.BlockSpec((None, HALO, cols), lambda bi, i: (bi, jnp.minimum((i + 1) * nh, last), 0)),
                  pl.BlockSpec((1, cols), const),
                  pl.BlockSpec((2, c), const),
                  pl.BlockSpec((LANES, c), const),
                  pl.BlockSpec((2, c), const),
                  pl.BlockSpec((LANES, c), const),
                  pl.BlockSpec((LANES, c), const),
                  pl.BlockSpec((1, c), const),
                  pl.BlockSpec((1, c), const),
                  pl.BlockSpec((1, c), const)],
        out_specs=[ospec] * 11,
        out_shape=[out(BF16)] * 7 + [out(F32)] * 2 + [out(BF16)] * 2,
        compiler_params=_params(("parallel", "parallel")),
        name="rwkv_prep",
    )(zr, zr, zr, mu, w0, w2, a0, a2, g2, kk_w, ka_w, rk)


def _wkv_scan_kernel(lwf_ref, rf_ref, vf_ref, kkf_ref, kaf_ref, kdf_ref,
                     lwb_ref, rb_ref, vb_ref, kkb_ref, kab_ref, kdb_ref,
                     yf_ref, yb_ref, st_ref):
    L = SCAN_CHUNK
    P2 = 2 * L
    ci = pl.program_id(1)

    @pl.when(ci == 0)
    def _():
        st_ref[...] = jnp.zeros(st_ref.shape, F32)

    row = lax.broadcasted_iota(jnp.int32, (P2, P2), 0)
    col = lax.broadcasted_iota(jnp.int32, (P2, P2), 1)
    same_head = (row // L) == (col // L)
    rt = row % L
    ct = col % L
    eye = row == col
    ti = lax.broadcasted_iota(jnp.int32, (L, L), 0)
    tj = lax.broadcasted_iota(jnp.int32, (L, L), 1)
    lane_a = lax.broadcasted_iota(jnp.int32, (L, LANES), 1) < HEAD_DIM

    def expand(x):
        return jnp.concatenate([jnp.where(lane_a, x, 0.0), jnp.where(lane_a, 0.0, x)], axis=0)

    units = []
    dirs = ((lwf_ref, rf_ref, vf_ref, kkf_ref, kaf_ref, kdf_ref, yf_ref),
            (lwb_ref, rb_ref, vb_ref, kkb_ref, kab_ref, kdb_ref, yb_ref))
    for d, (lw_ref, r_ref, v_ref, kk_ref, ka_ref, kd_ref, y_ref) in enumerate(dirs):
        if d == 0:
            m_strict = same_head & (rt > ct)
            m_incl = same_head & (rt >= ct)
            tri = jnp.where(tj <= ti, 1.0, 0.0).astype(BF16)
        else:
            m_strict = same_head & (rt < ct)
            m_incl = same_head & (rt <= ct)
            tri = jnp.where(tj >= ti, 1.0, 0.0).astype(BF16)
        m_level = []
        for lvl in range(L.bit_length() - 1):
            joined = same_head & ((rt >> (lvl + 1)) == (ct >> (lvl + 1)))
            r_bit = (rt >> lvl) & 1
            c_bit = (ct >> lvl) & 1
            later_row = (r_bit == 1) & (c_bit == 0) if d == 0 else (r_bit == 0) & (c_bit == 1)
            m_level.append(joined & later_row)
        lw = lw_ref[...]
        hi = lw.astype(BF16)
        r1 = lw - hi.astype(F32)
        mid = r1.astype(BF16)
        lo = (r1 - mid.astype(F32)).astype(BF16)
        cum = _dot(tri, hi) + _dot(tri, mid) + _dot(tri, lo)
        cum_last = cum[L - 1:L, :] if d == 0 else cum[0:1, :]
        ec = jnp.exp(cum)
        enc = jnp.exp(-cum)
        p_last = jnp.exp(cum_last)
        a_t = -kk_ref[...].astype(F32) * jnp.exp(cum - lw)
        r_t = r_ref[...].astype(F32) * ec
        b_t = ka_ref[...].astype(F32) * enc
        k_t = kd_ref[...].astype(F32) * enc
        b_h = b_t * p_last
        k_h = k_t * p_last
        vv = v_ref[...].astype(F32)

        for p in range(lw.shape[1] // LANES):
            sl = slice(p * LANES, (p + 1) * LANES)
            units.append(dict(
                d=d, p=p, sl=sl, y_ref=y_ref, m_strict=m_strict, m_incl=m_incl, m_level=m_level,
                ea=expand(a_t[:, sl]), er=expand(r_t[:, sl]), vexp=expand(vv[:, sl]),
                bk2=jnp.concatenate([b_t[:, sl]] * 2 + [k_t[:, sl]] * 2, axis=0).astype(BF16),
                bh=b_h[:, sl], kh=k_h[:, sl], p_last=p_last[:, sl]))

    for u in units:
        g = _dot_nt(jnp.concatenate([u["ea"], u["er"]], axis=0).astype(BF16), u["bk2"])
        u["xab"] = jnp.where(u["m_strict"], g[0:P2, 0:P2], 0.0)
        u["xak"] = jnp.where(u["m_strict"], g[0:P2, P2:2 * P2], 0.0).astype(BF16)
        u["xr"] = jnp.concatenate([jnp.where(u["m_incl"], g[P2:2 * P2, 0:P2], 0.0),
                                   jnp.where(u["m_incl"], g[P2:2 * P2, P2:2 * P2], 0.0)], axis=1).astype(BF16)
        u["tinv"] = jnp.where(eye, 1.0, jnp.where(u["m_level"][0], u["xab"], 0.0))
    for lvl in range(1, L.bit_length() - 1):
        for u in units:
            xoff = jnp.where(u["m_level"][lvl], u["xab"], 0.0).astype(BF16)
            u["tx"] = _dot(u["tinv"].astype(BF16), xoff).astype(BF16)
        for u in units:
            u["tinv"] = u["tinv"] + _dot(u["tx"], u["tinv"].astype(BF16))
    for u in units:
        u["xv"] = _dot(u["xak"], u["vexp"].astype(BF16))
    for u in units:
        u["wu0"] = _dot(u["tinv"].astype(BF16), jnp.concatenate([u["ea"], u["xv"]], axis=1).astype(BF16))
        u["bkt"] = jnp.concatenate([expand(u["bh"]).T, expand(u["kh"]).T], axis=1).astype(BF16)
        u["p_col"] = jnp.broadcast_to(u["p_last"], (P2, LANES)).T
    for u in units:
        u["st"] = st_ref[u["d"], u["p"]]
        u["ws"] = _dot(jnp.concatenate([u["wu0"][:, 0:LANES], u["er"]], axis=0).astype(BF16), u["st"].astype(BF16))
    for u in units:
        uu = u["ws"][0:P2] + u["wu0"][:, LANES:2 * LANES]
        u["uv"] = jnp.concatenate([uu, u["vexp"]], axis=0).astype(BF16)
    for u in units:
        yexp = u["ws"][P2:2 * P2] + _dot(u["xr"], u["uv"])
        u["y_ref"][:, u["sl"]] = (yexp[0:L] + yexp[L:P2]).astype(u["y_ref"].dtype)
    for u in units:
        st_ref[u["d"], u["p"]] = u["p_col"] * u["st"] + _dot(u["bkt"], u["uv"])


def _wkv_scan(lw0, lw1, r, v, kk, ka0, ka1, kd0, kd1):
    b, s, c = r.shape
    L = SCAN_CHUNK
    nc = s // L
    fwd = lambda bi, i: (bi, i, 0)
    bwd = lambda bi, i: (bi, nc - 1 - i, 0)
    fs = pl.BlockSpec((None, L, c), fwd)
    bs = pl.BlockSpec((None, L, c), bwd)
    return pl.pallas_call(
        _wkv_scan_kernel,
        grid=(b, nc),
        in_specs=[fs] * 6 + [bs] * 6,
        out_specs=[fs, bs],
        out_shape=[jax.ShapeDtypeStruct((b, s, c), F32)] * 2,
        scratch_shapes=[pltpu.VMEM((2, c // LANES, LANES, LANES), F32)],
        compiler_params=_params(("parallel", "arbitrary")),
        name="wkv_scan",
    )(lw0, r, v, kk, ka0, kd0, lw1, r, v, kk, ka1, kd1)


def _post_kernel(x_ref, oa_ref, yf_ref, yb_ref, g_ref, bv_ref, zg_ref, lnw_ref, lnb_ref,
                 woa_ref, wor_ref, wout_ref, n2w_ref, wq_ref, x1_ref, xn_ref, qp_ref):
    y = yf_ref[...] + yb_ref[...]
    c = y.shape[1]
    e, et = _seg_mats(c, HEAD_DIM)
    mu = _seg_sum(y, e, et, signed=True) * (1.0 / HEAD_DIM)
    dlt = y - mu
    var = _seg_sum(dlt * dlt, e, et) * (1.0 / HEAD_DIM)
    yn = dlt * lax.rsqrt(var + RWKV_LN_EPS) * lnw_ref[...] + lnb_ref[...]
    out = ((yn + bv_ref[...].astype(F32)) * g_ref[...].astype(F32)).astype(BF16)
    h_rwkv = _dot(out, wor_ref[...])
    h_attn = _dot(oa_ref[...], woa_ref[...])
    zg = zg_ref[...].astype(F32)
    merged = _sigmoid(zg[:, 0:c]) * h_attn + _sigmoid(zg[:, c:2 * c]) * h_rwkv
    x1 = x_ref[...] + _dot(merged.astype(BF16), wout_ref[...])
    x1_ref[...] = x1
    ms = jnp.mean(x1 * x1, axis=-1, keepdims=True)
    xn = (x1 * lax.rsqrt(ms + NORM_EPS) * n2w_ref[...]).astype(BF16)
    xn_ref[...] = xn
    qp = _dot(xn, wq_ref[...]).astype(BF16)
    for j in range(qp_ref.shape[0]):
        qp_ref[j] = qp[:, j * LANES:(j + 1) * LANES]


def _post(x, oa, yf, yb, g, bv, zg, lnw, lnb, woa, wor, wout, n2w, wq, tm):
    t, d = x.shape
    nq = wq.shape[1] // LANES
    tok = lambda i: (i, 0)
    const = lambda i: (0, 0)
    tspec = lambda w: pl.BlockSpec((tm, w), tok)
    cspec = lambda a: pl.BlockSpec(a.shape, const)
    return pl.pallas_call(
        _post_kernel,
        grid=(t // tm,),
        in_specs=[tspec(d), tspec(d), tspec(d), tspec(d), tspec(d), tspec(d), tspec(2 * d),
                  cspec(lnw), cspec(lnb), cspec(woa), cspec(wor), cspec(wout), cspec(n2w), cspec(wq)],
        out_specs=[tspec(d), tspec(d), pl.BlockSpec((nq, tm, LANES), lambda i: (0, i, 0))],
        out_shape=[jax.ShapeDtypeStruct((t, d), F32), jax.ShapeDtypeStruct((t, d), BF16),
                   jax.ShapeDtypeStruct((nq, t, LANES), BF16)],
        compiler_params=_params(("parallel",)),
        name="post_merge",
    )(x, oa, yf, yb, g, bv, zg, lnw, lnb, woa, wor, wout, n2w, wq)


SUBLANES = 8


def _sort_pairs(n):
    def merge(lo, hi, r):
        step = r * 2
        if step < hi - lo:
            yield from merge(lo, hi, step)
            yield from merge(lo + r, hi, step)
            for i in range(lo + r, hi - r, step):
                yield (i, i + r)
        else:
            yield (lo, lo + r)

    def sort(lo, hi):
        if hi - lo >= 1:
            mid = lo + (hi - lo) // 2
            yield from sort(lo, mid)
            yield from sort(mid + 1, hi)
            yield from merge(lo, hi, 1)

    return list(sort(0, n - 1))


def _exchange(xs, i, j):
    a, b = xs[i], xs[j]
    if b is None:
        return
    if a is None:
        xs[i], xs[j] = b, None
        return
    xs[i], xs[j] = jnp.maximum(a, b), jnp.minimum(a, b)


def _top_sorted(blocks, k):
    xs = list(blocks)
    for i, j in _sort_pairs(k):
        _exchange(xs, i, j)
    shift = SUBLANES // 2
    while shift >= 1:
        ys = [None if x is None else pltpu.roll(x, shift, 0) for x in xs]
        zs = []
        for i in range(k):
            a, b = xs[i], ys[k - 1 - i]
            zs.append(b if a is None else (a if b is None else jnp.maximum(a, b)))
        step = k // 2
        while step >= 1:
            for i in range(k):
                if i & step == 0:
                    _exchange(zs, i, i + step)
            step //= 2
        xs = zs
        shift //= 2
    return [x[0:1, :] for x in xs]


def _row_blocks(x):
    return [x[r:r + SUBLANES, :] for r in range(0, x.shape[0], SUBLANES)]


def _peer_select(s0, s1, k):
    a = _top_sorted(_row_blocks(s0), k)
    b = _top_sorted(_row_blocks(s1), k)
    amat = jnp.concatenate(a, axis=0)
    bmat = jnp.concatenate(b, axis=0)
    h8 = k // 2
    cand = ([a[0] + bmat[0:h8], a[0] + bmat[h8:k], a[1] + bmat[0:h8], amat[h8:k] + b[0]]
            + [a[i] + bmat[0:h8] for i in range(2, h8)])
    best = _top_sorted(cand + [None] * (k - len(cand)), k)
    z = jnp.zeros_like(best[0])
    for val in best:
        z = z + jnp.exp(val - best[0])
    return amat, bmat, best[k - 1], z


def _peer_kernel(xn_ref, qp_ref, sk_ref, u_ref, vt_ref, x1_ref, o_ref,
                 cnt_ref, e0_ref, rank_ref, e1_ref, acc_ref):
    jb = pl.program_id(1)
    tt = xn_ref.shape[0]
    K = PEER_TOPK
    rows_per_blk = u_ref.shape[0] // PEER_NKEYS

    @pl.when(jb == 0)
    def _():
        acc_ref[...] = jnp.zeros(acc_ref.shape, F32)

        def head_body(h, carry):
            s0 = _dot_nt(sk_ref[2 * h], qp_ref[2 * h])
            s1 = _dot_nt(sk_ref[2 * h + 1], qp_ref[2 * h + 1])

            amat, bmat, thr, z = _peer_select(s0, s1, K)
            a0 = amat[0:1]
            top_cnt = jnp.zeros(amat.shape, F32)
            for kk in range(K):
                top_cnt = top_cnt + jnp.where(amat + bmat[kk:kk + 1] >= thr, 1.0, 0.0)
            cnt = jnp.zeros(s0.shape, F32)
            rank = jnp.full(s1.shape, float(PEER_NKEYS), F32)
            for kk in reversed(range(K)):
                cnt = jnp.where(s0 == amat[kk:kk + 1], top_cnt[kk:kk + 1], cnt)
                rank = jnp.where(s1 == bmat[kk:kk + 1], float(kk + 1), rank)
            cnt_ref[h] = cnt
            e0_ref[h] = jnp.exp(s0 - a0) / z
            rank_ref[h] = rank.astype(BF16)
            e1_ref[h] = jnp.exp(s1 - bmat[0:1]).astype(BF16)
            return carry

        lax.fori_loop(0, PEER_HEADS, head_body, 0)

    sub = PEER_SUB_ROWS * PEER_NKEYS
    xn = xn_ref[...]
    n_sub = u_ref.shape[0] // sub
    hpres = [None] * n_sub
    hpres[0] = _dot_nt(u_ref[0:sub, :], xn)
    zero = jnp.zeros((), BF16)
    hgs = []
    for c in range(n_sub):
        if c + 1 < n_sub:
            hpres[c + 1] = _dot_nt(u_ref[(c + 1) * sub:(c + 2) * sub, :], xn)
        g_rows = []
        for il in range(c * PEER_SUB_ROWS, (c + 1) * PEER_SUB_ROWS):
            i = jb * rows_per_blk + il
            acc = None
            for h in range(PEER_HEADS):
                crow = cnt_ref[h, pl.ds(i, 1), :].astype(BF16)
                e0row = e0_ref[h, pl.ds(i, 1), :].astype(BF16)
                contrib = jnp.where(rank_ref[h] <= crow, e0row * e1_ref[h], zero)
                acc = contrib if acc is None else acc + contrib
            g_rows.append(acc)
        gates = jnp.concatenate(g_rows, axis=0)
        hpre = hpres[c]
        gelu = 0.5 * hpre * (1.0 + lax.erf(hpre * (2.0 ** -0.5)))
        hgs.append(gelu.astype(BF16) * gates)
        if len(hgs) == PEER_OUT_GROUP:
            lo = (c + 1 - PEER_OUT_GROUP) * sub
            acc_ref[...] += _dot(vt_ref[:, lo:(c + 1) * sub], jnp.concatenate(hgs, axis=0))
            hgs = []

    @pl.when(jb == pl.num_programs(1) - 1)
    def _():
        o_ref[...] = x1_ref[...] + acc_ref[...].T


def _peer(xn, qp, sk, u, vt, x1, tt, eb):
    t, d = xn.shape
    n_exp = u.shape[0]
    nq = qp.shape[0]
    return pl.pallas_call(
        _peer_kernel,
        grid=(t // tt, n_exp // eb),
        in_specs=[pl.BlockSpec((tt, d), lambda i, j: (i, 0)),
                  pl.BlockSpec((nq, tt, LANES), lambda i, j: (0, i, 0)),
                  pl.BlockSpec(sk.shape, lambda i, j: (0, 0, 0)),
                  pl.BlockSpec((eb, d), lambda i, j: (j, 0)),
                  pl.BlockSpec((d, eb), lambda i, j: (0, j)),
                  pl.BlockSpec((tt, d), lambda i, j: (i, 0))],
        out_specs=pl.BlockSpec((tt, d), lambda i, j: (i, 0)),
        out_shape=jax.ShapeDtypeStruct((t, d), F32),
        scratch_shapes=[pltpu.VMEM((PEER_HEADS, PEER_NKEYS, tt), F32),
                        pltpu.VMEM((PEER_HEADS, PEER_NKEYS, tt), F32),
                        pltpu.VMEM((PEER_HEADS, PEER_NKEYS, tt), BF16),
                        pltpu.VMEM((PEER_HEADS, PEER_NKEYS, tt), BF16),
                        pltpu.VMEM((d, tt), F32)],
        compiler_params=_params(("parallel", "arbitrary")),
        name="peer",
    )(xn, qp, sk, u, vt, x1)


def _rope_tables(s):
    inv = 1.0 / (ROPE_THETA ** (jnp.arange(0, HEAD_DIM, 2, dtype=F32) / HEAD_DIM))
    ang = jnp.arange(s, dtype=F32)[:, None] * inv[None, :]
    cos, sin = jnp.cos(ang), jnp.sin(ang)
    reps = LANES // HEAD_DIM
    cos_t = jnp.tile(jnp.concatenate([cos, cos], axis=1), (1, reps))
    sin_t = jnp.tile(jnp.concatenate([-sin, sin], axis=1), (1, reps))
    return cos_t, sin_t


def _tile(n, pref):
    return pref if n % pref == 0 else n


def _tiling(t, s):
    return dict(
        proj=_tile(t, 512),
        inproj_attn=_tile(s, 512),
        attn_q=_tile(s, 4096),
        attn_k=_tile(s, 2048),
        rwkv_prep=_tile(s, 256),
        post=_tile(t, 256),
        peer_tokens=_tile(t, 512),
        peer_experts=2048,
    )


def _trunk(x, P):
    b, s, d = x.shape
    t = b * s
    c = d
    tiles = _tiling(t, s)
    xf = x.reshape(t, d)
    zg = _norm_matmul(xf, P["norm1_w"], P["w_g"], tiles["proj"])

    cos_t, sin_t = _rope_tables(s)
    q, kt, va = _inproj_attn(x, P["norm1_w"], P["w_a"], cos_t, sin_t, P["q_norm_w"], P["k_norm_w"],
                             HEAD_DIM ** -0.5 * math.log2(math.e), tiles["inproj_attn"])
    oa = _attention(P["lam_rows"], q, kt, va, P["subln_w"], tiles["attn_q"], tiles["attn_k"])

    zr = _norm_matmul(xf, P["norm1_w"], P["w_r"], tiles["proj"]).reshape(b, s, -1)
    (r, v, kk, kd0, kd1, ka0, ka1, lw0, lw1, g, bv) = _rwkv_prep(
        zr, P["shift_mu"], P["decay_w0"], P["decay_w2"], P["iclr_a0"], P["iclr_a2"],
        P["gate_g2"], P["k_k"], P["k_a"], P["r_k"], c, tiles["rwkv_prep"])
    yf, yb = _wkv_scan(lw0, lw1, r, v, kk, ka0, ka1, kd0, kd1)

    flat = lambda a: a.reshape(t, a.shape[-1])
    x1, xn, qp = _post(xf, flat(oa), flat(yf), flat(yb), flat(g), flat(bv), zg, P["lnx_w"], P["lnx_b"],
                       P["w_o_attn"], P["w_o_rwkv"], P["w_out"], P["norm2_w"], P["peer_wq"], tiles["post"])
    y = _peer(xn, qp, P["peer_subkeys"], P["peer_u"], P["peer_vt"], x1, tiles["peer_tokens"],
              tiles["peer_experts"])
    return y.reshape(b, s, d)


def kernel(x_prompt, x_sample, norm1_w, w_in, q_norm_w, k_norm_w, lambda_q1, lambda_k1, lambda_q2, lambda_k2, subln_w, w_o_attn, shift_mu, decay_w0, decay_w2, iclr_a0, iclr_a2, gate_g2, k_k, k_a, r_k, lnx_w, lnx_b, w_o_rwkv, w_out, norm2_w, peer_wq, peer_subkeys, peer_u, peer_v):
    d = x_prompt.shape[-1]
    c = d
    att_cols = 3 * d
    rwkv_cols = shift_mu.shape[-1]
    l = 0
    w = w_in[l].astype(BF16)
    nsub = d // HEAD_DIM
    P = {
        "norm1_w": norm1_w[l][None, :],
        "w_a": w[:, :att_cols],
        "w_r": w[:, att_cols:att_cols + rwkv_cols],
        "w_g": w[:, att_cols + rwkv_cols:],
        "q_norm_w": jnp.tile(q_norm_w[l], nsub)[None, :],
        "k_norm_w": jnp.tile(k_norm_w[l], nsub)[None, :],
        "lam_rows": jnp.stack([lambda_q1[l], lambda_k1[l], lambda_q2[l], lambda_k2[l]]),
        "subln_w": subln_w[l][None, :],
        "w_o_attn": w_o_attn[l].astype(BF16),
        "shift_mu": shift_mu[l][None, :],
        "decay_w0": decay_w0[l],
        "decay_w2": decay_w2[l].reshape(-1, c).astype(BF16),
        "iclr_a0": iclr_a0[l],
        "iclr_a2": iclr_a2[l].reshape(-1, c).astype(BF16),
        "gate_g2": gate_g2[l].astype(BF16),
        "k_k": k_k[l][None, :],
        "k_a": k_a[l][None, :],
        "r_k": r_k[l].reshape(1, c),
        "lnx_w": lnx_w[l][None, :],
        "lnx_b": lnx_b[l][None, :],
        "w_o_rwkv": w_o_rwkv[l].astype(BF16),
        "w_out": w_out[l].astype(BF16),
        "norm2_w": norm2_w[l][None, :],
        "peer_wq": peer_wq[l].astype(BF16),
        "peer_subkeys": peer_subkeys[l].reshape(-1, PEER_NKEYS, peer_subkeys.shape[-1]).astype(BF16),
        "peer_u": peer_u[l].astype(BF16),
        "peer_vt": peer_v[l].astype(BF16).T,
    }
    return (_trunk(x_prompt, P), _trunk(x_sample, P))
```

```python
import functools
import math

import jax
import jax.numpy as jnp
from jax import lax
from jax.experimental import pallas as pl
from jax.experimental.pallas import tpu as pltpu

F32 = jnp.float32
BF16 = jnp.bfloat16

LANES = 128
HEAD_DIM = 64
ATT_HEADS = 8
NORM_EPS = 1e-6
RWKV_LN_EPS = 64e-5
ROPE_THETA = 10000.0
LAMBDA_INIT = 0.8 - 0.6 * math.exp(-0.3 * 0)
SCAN_CHUNK = 64
PEER_NKEYS = 128
PEER_TOPK = 16
PEER_HEADS = 8
PEER_SUB_ROWS = 2
PEER_OUT_GROUP = 8
VMEM_LIMIT = 56 * 1024 * 1024


def _params(sem):
    return pltpu.CompilerParams(dimension_semantics=sem, vmem_limit_bytes=VMEM_LIMIT)


def _dot(a, b):
    return jnp.dot(a, b, preferred_element_type=F32)


def _dot_nt(a, b):
    return lax.dot_general(a, b, (((1,), (1,)), ((), ())), preferred_element_type=F32)


def _split_dot(x, m):
    hi = x.astype(BF16)
    lo = (x - hi.astype(F32)).astype(BF16)
    return _dot(hi, m) + _dot(lo, m)


def _seg_mats(width, seg):
    lane = lax.broadcasted_iota(jnp.int32, (width, LANES), 0)
    grp = lax.broadcasted_iota(jnp.int32, (width, LANES), 1)
    e = jnp.where(lane // seg == grp, 1.0, 0.0).astype(BF16)
    grp_t = lax.broadcasted_iota(jnp.int32, (LANES, width), 0)
    lane_t = lax.broadcasted_iota(jnp.int32, (LANES, width), 1)
    et = jnp.where(lane_t // seg == grp_t, 1.0, 0.0).astype(BF16)
    return e, et


def _seg_sum(x, e, et, signed=False):
    totals = _split_dot(x, e) if signed else _dot(x.astype(BF16), e)
    return _split_dot(totals, et)


def _sigmoid(x):
    return 1.0 / (1.0 + jnp.exp(-x))


def _norm_matmul_kernel(x_ref, nw_ref, w_ref, o_ref):
    x = x_ref[...]
    ms = jnp.mean(x * x, axis=-1, keepdims=True)
    h = (x * lax.rsqrt(ms + NORM_EPS) * nw_ref[...]).astype(BF16)
    o_ref[...] = _dot(h, w_ref[...]).astype(o_ref.dtype)


def _norm_matmul(x, nw, w, tm):
    t, d = x.shape
    n = w.shape[1]
    return pl.pallas_call(
        _norm_matmul_kernel,
        grid=(t // tm,),
        in_specs=[pl.BlockSpec((tm, d), lambda i: (i, 0)),
                  pl.BlockSpec((1, d), lambda i: (0, 0)),
                  pl.BlockSpec((d, n), lambda i: (0, 0))],
        out_specs=pl.BlockSpec((tm, n), lambda i: (i, 0)),
        out_shape=jax.ShapeDtypeStruct((t, n), BF16),
        compiler_params=_params(("parallel",)),
        name="norm_matmul",
    )(x, nw, w)


def _norm_rope(x, w, c, s, e, et, scale):
    width = x.shape[1]
    ms = _seg_sum(x * x, e, et) * (1.0 / HEAD_DIM)
    y = x * lax.rsqrt(ms + NORM_EPS) * w
    lane = lax.broadcasted_iota(jnp.int32, y.shape, 1)
    half = HEAD_DIM // 2
    partner = jnp.where(lane % HEAD_DIM < half, pltpu.roll(y, width - half, 1), pltpu.roll(y, half, 1))
    return (y * c + partner * s) * scale


def _inproj_attn_kernel(x_ref, nw_ref, w_ref, cos_ref, sin_ref, qw_ref, kw_ref, q_ref, kt_ref, v_ref, *, q_scale):
    x = x_ref[...]
    ms = jnp.mean(x * x, axis=-1, keepdims=True)
    h = (x * lax.rsqrt(ms + NORM_EPS) * nw_ref[...]).astype(BF16)
    z = _dot(h, w_ref[...])
    width = q_ref.shape[1]
    reps = width // LANES
    e, et = _seg_mats(width, HEAD_DIM)
    c = jnp.concatenate([cos_ref[...]] * reps, axis=1)
    s = jnp.concatenate([sin_ref[...]] * reps, axis=1)
    q_ref[...] = _norm_rope(z[:, 0:width], qw_ref[...], c, s, e, et, q_scale).astype(q_ref.dtype)
    k = _norm_rope(z[:, width:2 * width], kw_ref[...], c, s, e, et, 1.0)
    for j in range(reps):
        kt_ref[j * LANES:(j + 1) * LANES, :] = k[:, j * LANES:(j + 1) * LANES].T.astype(kt_ref.dtype)
    v_ref[...] = z[:, 2 * width:3 * width].astype(v_ref.dtype)


def _inproj_attn(x, nw, w_a, cos_t, sin_t, qw, kw, q_scale, tm):
    b, s, d = x.shape
    width = qw.shape[1]
    tok = lambda bi, i: (bi, i, 0)
    const = lambda bi, i: (0, 0)
    return pl.pallas_call(
        functools.partial(_inproj_attn_kernel, q_scale=q_scale),
        grid=(b, s // tm),
        in_specs=[pl.BlockSpec((None, tm, d), tok),
                  pl.BlockSpec((1, d), const),
                  pl.BlockSpec(w_a.shape, const),
                  pl.BlockSpec((tm, LANES), lambda bi, i: (i, 0)),
                  pl.BlockSpec((tm, LANES), lambda bi, i: (i, 0)),
                  pl.BlockSpec((1, width), const),
                  pl.BlockSpec((1, width), const)],
        out_specs=[pl.BlockSpec((None, tm, width), tok),
                   pl.BlockSpec((None, width, tm), lambda bi, i: (bi, 0, i)),
                   pl.BlockSpec((None, tm, width), tok)],
        out_shape=[jax.ShapeDtypeStruct((b, s, width), BF16),
                   jax.ShapeDtypeStruct((b, width, s), BF16),
                   jax.ShapeDtypeStruct((b, s, width), BF16)],
        compiler_params=_params(("parallel", "parallel")),
        name="inproj_attn",
    )(x, nw, w_a, cos_t, sin_t, qw, kw)


ATTN_ROW_CHUNK = 512


def _attn_kernel(lam_ref, q_ref, kt_ref, v_ref, sw_ref, o_ref, qs_ref, m_ref, acc_ref, s0_ref, *, tq, tk):
    q = q_ref[...]
    lane = lax.broadcasted_iota(jnp.int32, q.shape, 1)
    zero = jnp.zeros_like(q)
    qs_ref[0:tq, :] = jnp.where(lane < HEAD_DIM, q, zero)
    qs_ref[tq:2 * tq, :] = jnp.where(lane < HEAD_DIM, zero, q)
    m_ref[...] = jnp.full(m_ref.shape, -jnp.inf, F32)
    acc_ref[...] = jnp.zeros(acc_ref.shape, F32)

    rc = min(ATTN_ROW_CHUNK, 2 * tq)
    n_chunks = 2 * tq // rc
    n_tiles = kt_ref.shape[1] // tk
    s0_ref[...] = _dot(qs_ref[0:rc, :], kt_ref[:, 0:tk])

    def key_tile(j, carry):
        off = pl.multiple_of(j * tk, tk)
        v = v_ref[pl.ds(off, tk), :]
        v_ones = jnp.concatenate([v, jnp.ones_like(v)], axis=1)
        kt = kt_ref[:, pl.ds(off, tk)]
        scores = [None] * n_chunks
        scores[0] = s0_ref[...]
        for c in range(n_chunks):
            rows = slice(c * rc, (c + 1) * rc)
            if c + 1 < n_chunks:
                scores[c + 1] = _dot(qs_ref[(c + 1) * rc:(c + 2) * rc, :], kt)
            else:
                off_next = pl.multiple_of(jnp.minimum(j + 1, n_tiles - 1) * tk, tk)
                s0_ref[...] = _dot(qs_ref[0:rc, :], kt_ref[:, pl.ds(off_next, tk)])
            s = scores[c]
            m_prev = m_ref[rows, :]
            m_new = jnp.maximum(m_prev, jnp.max(s, axis=1, keepdims=True))
            alpha = jnp.exp2(m_prev - m_new)
            p = jnp.exp2((s - jnp.concatenate([m_new] * (tk // LANES), axis=1)).astype(BF16))
            acc_ref[rows, :] = (jnp.concatenate([alpha, alpha], axis=1) * acc_ref[rows, :]
                                + _dot(p, v_ones))
            m_ref[rows, :] = m_new
        return carry

    lax.fori_loop(0, n_tiles, key_tile, 0)

    lv = lam_ref[...]
    lam = (jnp.exp(jnp.sum(lv[0:1] * lv[1:2], axis=1, keepdims=True))
           - jnp.exp(jnp.sum(lv[2:3] * lv[3:4], axis=1, keepdims=True)) + LAMBDA_INIT)
    acc = acc_ref[...]
    o = (acc[0:tq, 0:LANES] / acc[0:tq, LANES:2 * LANES]
         - lam * (acc[tq:2 * tq, 0:LANES] / acc[tq:2 * tq, LANES:2 * LANES]))
    ms = jnp.mean(o * o, axis=-1, keepdims=True)
    o = o * lax.rsqrt(ms + NORM_EPS) * sw_ref[...] * (1.0 - LAMBDA_INIT)
    o_ref[...] = o.astype(o_ref.dtype)


def _attention(lam_rows, q, kt, v, subln_w, tq, tk):
    b, s, width = q.shape
    heads = width // LANES
    return pl.pallas_call(
        functools.partial(_attn_kernel, tq=tq, tk=tk),
        grid=(b, heads, s // tq),
        in_specs=[pl.BlockSpec((4, HEAD_DIM), lambda bi, h, qi: (0, 0)),
                  pl.BlockSpec((None, tq, LANES), lambda bi, h, qi: (bi, qi, h)),
                  pl.BlockSpec((None, LANES, s), lambda bi, h, qi: (bi, h, 0)),
                  pl.BlockSpec((None, s, LANES), lambda bi, h, qi: (bi, 0, h)),
                  pl.BlockSpec((1, LANES), lambda bi, h, qi: (0, 0))],
        out_specs=pl.BlockSpec((None, tq, LANES), lambda bi, h, qi: (bi, qi, h)),
        out_shape=jax.ShapeDtypeStruct((b, s, width), BF16),
        scratch_shapes=[pltpu.VMEM((2 * tq, LANES), BF16),
                        pltpu.VMEM((2 * tq, LANES), F32),
                        pltpu.VMEM((2 * tq, 2 * LANES), F32),
                        pltpu.VMEM((min(ATTN_ROW_CHUNK, 2 * tq), tk), F32)],
        compiler_params=_params(("parallel", "parallel", "arbitrary")),
        name="diff_attention",
    )(lam_rows, q, kt, v, subln_w)


HALO = 16


def _rwkv_prep_kernel(z_ref, zp_ref, zn_ref, mu_ref, w0_ref, w2_ref, a0_ref, a2_ref, g2_ref,
                      kk_w_ref, ka_w_ref, rk_ref,
                      r_ref, v_ref, kk_ref, kd0_ref, kd1_ref, ka0_ref, ka1_ref, lw0_ref, lw1_ref,
                      g_ref, bv_ref, *, c):
    i = pl.program_id(1)
    z = z_ref[...].astype(F32)
    tm = z.shape[0]
    row = lax.broadcasted_iota(jnp.int32, z.shape, 0)
    prev_row = jnp.where(i == 0, 0.0, zp_ref[HALO - 1:HALO, :].astype(F32))
    next_row = jnp.where(i == pl.num_programs(1) - 1, 0.0, zn_ref[0:1, :].astype(F32))
    z_up = jnp.where(row == 0, prev_row, pltpu.roll(z, 1, 0))
    z_dn = jnp.where(row == tm - 1, next_row, pltpu.roll(z, tm - 1, 0))
    zs = z + mu_ref[...] * (0.5 * (z_up + z_dn) - z)

    r = zs[:, 0:c]
    k = zs[:, c:2 * c]
    v = zs[:, 2 * c:3 * c]
    wl = zs[:, 3 * c:3 * c + LANES]
    al = zs[:, 3 * c + LANES:3 * c + 2 * LANES]
    gl = zs[:, 3 * c + 2 * LANES:3 * c + 3 * LANES]

    lane = lax.broadcasted_iota(jnp.int32, wl.shape, 1)
    twl = jnp.tanh(wl)
    e, et = _seg_mats(c, HEAD_DIM)

    kk = k * kk_w_ref[...]
    kk = kk * lax.rsqrt(_seg_sum(kk * kk, e, et) + 1e-12)
    kk_ref[...] = kk.astype(kk_ref.dtype)
    r_ref[...] = r.astype(r_ref.dtype)
    v_ref[...] = v.astype(v_ref.dtype)
    g_ref[...] = _dot(_sigmoid(gl).astype(BF16), g2_ref[...]).astype(g_ref.dtype)

    ksum = jnp.zeros_like(k)
    for d, (kd_ref, ka_ref, lw_ref) in enumerate(((kd0_ref, ka0_ref, lw0_ref), (kd1_ref, ka1_ref, lw1_ref))):
        sel = (lane < HEAD_DIM) if d == 0 else (lane >= HEAD_DIM)
        wpre = _dot(jnp.where(sel, twl, 0.0).astype(BF16), w2_ref[...]) + w0_ref[d:d + 1, :]
        lw_ref[...] = (-math.exp(-0.5)) * _sigmoid(wpre)
        a = _sigmoid(_dot(jnp.where(sel, al, 0.0).astype(BF16), a2_ref[...]) + a0_ref[d:d + 1, :])
        kd = k * (1.0 + (a - 1.0) * ka_w_ref[...])
        kd_ref[...] = kd.astype(kd_ref.dtype)
        ka_ref[...] = (kk * a).astype(ka_ref.dtype)
        ksum = ksum + kd
    bonus = _seg_sum(r * ksum * rk_ref[...], e, et, signed=True)
    bv_ref[...] = (bonus * v).astype(bv_ref.dtype)


def _rwkv_prep(zr, mu, w0, w2, a0, a2, g2, kk_w, ka_w, rk, c, tm):
    b, s, cols = zr.shape
    nh = tm // HALO
    last = s // HALO - 1
    tok = lambda bi, i: (bi, i, 0)
    const = lambda bi, i: (0, 0)
    out = lambda dt: jax.ShapeDtypeStruct((b, s, c), dt)
    ospec = pl.BlockSpec((None, tm, c), tok)
    return pl.pallas_call(
        functools.partial(_rwkv_prep_kernel, c=c),
        grid=(b, s // tm),
        in_specs=[pl.BlockSpec((None, tm, cols), tok),
                  pl.BlockSpec((None, HALO, cols), lambda bi, i: (bi, jnp.maximum(i * nh - 1, 0), 0)),
                  pl.BlockSpec((None, HALO, cols), lambda bi, i: (bi, jnp.minimum((i + 1) * nh, last), 0)),
                  pl.BlockSpec((1, cols), const),
                  pl.BlockSpec((2, c), const),
                  pl.BlockSpec((LANES, c), const),
                  pl.BlockSpec((2, c), const),
                  pl.BlockSpec((LANES, c), const),
                  pl.BlockSpec((LANES, c), const),
                  pl.BlockSpec((1, c), const),
                  pl.BlockSpec((1, c), const),
                  pl.BlockSpec((1, c), const)],
        out_specs=[ospec] * 11,
        out_shape=[out(BF16)] * 7 + [out(F32)] * 2 + [out(BF16)] * 2,
        compiler_params=_params(("parallel", "parallel")),
        name="rwkv_prep",
    )(zr, zr, zr, mu, w0, w2, a0, a2, g2, kk_w, ka_w, rk)


def _wkv_scan_kernel(lwf_ref, rf_ref, vf_ref, kkf_ref, kaf_ref, kdf_ref,
                     lwb_ref, rb_ref, vb_ref, kkb_ref, kab_ref, kdb_ref,
                     yf_ref, yb_ref, st_ref):
    L = SCAN_CHUNK
    P2 = 2 * L
    ci = pl.program_id(1)

    @pl.when(ci == 0)
    def _():
        st_ref[...] = jnp.zeros(st_ref.shape, F32)

    row = lax.broadcasted_iota(jnp.int32, (P2, P2), 0)
    col = lax.broadcasted_iota(jnp.int32, (P2, P2), 1)
    same_head = (row // L) == (col // L)
    rt = row % L
    ct = col % L
    eye = row == col
    ti = lax.broadcasted_iota(jnp.int32, (L, L), 0)
    tj = lax.broadcasted_iota(jnp.int32, (L, L), 1)
    lane_a = lax.broadcasted_iota(jnp.int32, (L, LANES), 1) < HEAD_DIM

    def expand(x):
        return jnp.concatenate([jnp.where(lane_a, x, 0.0), jnp.where(lane_a, 0.0, x)], axis=0)

    units = []
    dirs = ((lwf_ref, rf_ref, vf_ref, kkf_ref, kaf_ref, kdf_ref, yf_ref),
            (lwb_ref, rb_ref, vb_ref, kkb_ref, kab_ref, kdb_ref, yb_ref))
    for d, (lw_ref, r_ref, v_ref, kk_ref, ka_ref, kd_ref, y_ref) in enumerate(dirs):
        if d == 0:
            m_strict = same_head & (rt > ct)
            m_incl = same_head & (rt >= ct)
            tri = jnp.where(tj <= ti, 1.0, 0.0).astype(BF16)
        else:
            m_strict = same_head & (rt < ct)
            m_incl = same_head & (rt <= ct)
            tri = jnp.where(tj >= ti, 1.0, 0.0).astype(BF16)
        m_level = []
        for lvl in range(L.bit_length() - 1):
            joined = same_head & ((rt >> (lvl + 1)) == (ct >> (lvl + 1)))
            r_bit = (rt >> lvl) & 1
            c_bit = (ct >> lvl) & 1
            later_row = (r_bit == 1) & (c_bit == 0) if d == 0 else (r_bit == 0) & (c_bit == 1)
            m_level.append(joined & later_row)
        lw = lw_ref[...]
        hi = lw.astype(BF16)
        r1 = lw - hi.astype(F32)
        mid = r1.astype(BF16)
        lo = (r1 - mid.astype(F32)).astype(BF16)
        cum = _dot(tri, hi) + _dot(tri, mid) + _dot(tri, lo)
        cum_last = cum[L - 1:L, :] if d == 0 else cum[0:1, :]
        ec = jnp.exp(cum)
        enc = jnp.exp(-cum)
        p_last = jnp.exp(cum_last)
        a_t = -kk_ref[...].astype(F32) * jnp.exp(cum - lw)
        r_t = r_ref[...].astype(F32) * ec
        b_t = ka_ref[...].astype(F32) * enc
        k_t = kd_ref[...].astype(F32) * enc
        b_h = b_t * p_last
        k_h = k_t * p_last
        vv = v_ref[...].astype(F32)

        for p in range(lw.shape[1] // LANES):
            sl = slice(p * LANES, (p + 1) * LANES)
            units.append(dict(
                d=d, p=p, sl=sl, y_ref=y_ref, m_strict=m_strict, m_incl=m_incl, m_level=m_level,
                ea=expand(a_t[:, sl]), er=expand(r_t[:, sl]), vexp=expand(vv[:, sl]),
                bk2=jnp.concatenate([b_t[:, sl]] * 2 + [k_t[:, sl]] * 2, axis=0).astype(BF16),
                bh=b_h[:, sl], kh=k_h[:, sl], p_last=p_last[:, sl]))

    for u in units:
        g = _dot_nt(jnp.concatenate([u["ea"], u["er"]], axis=0).astype(BF16), u["bk2"])
        u["xab"] = jnp.where(u["m_strict"], g[0:P2, 0:P2], 0.0)
        u["xak"] = jnp.where(u["m_strict"], g[0:P2, P2:2 * P2], 0.0).astype(BF16)
        u["xr"] = jnp.concatenate([jnp.where(u["m_incl"], g[P2:2 * P2, 0:P2], 0.0),
                                   jnp.where(u["m_incl"], g[P2:2 * P2, P2:2 * P2], 0.0)], axis=1).astype(BF16)
        u["tinv"] = jnp.where(eye, 1.0, jnp.where(u["m_level"][0], u["xab"], 0.0))
    for lvl in range(1, L.bit_length() - 1):
        for u in units:
            xoff = jnp.where(u["m_level"][lvl], u["xab"], 0.0).astype(BF16)
            u["tx"] = _dot(u["tinv"].astype(BF16), xoff).astype(BF16)
        for u in units:
            u["tinv"] = u["tinv"] + _dot(u["tx"], u["tinv"].astype(BF16))
    for u in units:
        u["xv"] = _dot(u["xak"], u["vexp"].astype(BF16))
    for u in units:
        u["wu0"] = _dot(u["tinv"].astype(BF16), jnp.concatenate([u["ea"], u["xv"]], axis=1).astype(BF16))
        u["bkt"] = jnp.concatenate([expand(u["bh"]).T, expand(u["kh"]).T], axis=1).astype(BF16)
        u["p_col"] = jnp.broadcast_to(u["p_last"], (P2, LANES)).T
    for u in units:
        u["st"] = st_ref[u["d"], u["p"]]
        u["ws"] = _dot(jnp.concatenate([u["wu0"][:, 0:LANES], u["er"]], axis=0).astype(BF16), u["st"].astype(BF16))
    for u in units:
        uu = u["ws"][0:P2] + u["wu0"][:, LANES:2 * LANES]
        u["uv"] = jnp.concatenate([uu, u["vexp"]], axis=0).astype(BF16)
    for u in units:
        yexp = u["ws"][P2:2 * P2] + _dot(u["xr"], u["uv"])
        u["y_ref"][:, u["sl"]] = (yexp[0:L] + yexp[L:P2]).astype(u["y_ref"].dtype)
    for u in units:
        st_ref[u["d"], u["p"]] = u["p_col"] * u["st"] + _dot(u["bkt"], u["uv"])


def _wkv_scan(lw0, lw1, r, v, kk, ka0, ka1, kd0, kd1):
    b, s, c = r.shape
    L = SCAN_CHUNK
    nc = s // L
    fwd = lambda bi, i: (bi, i, 0)
    bwd = lambda bi, i: (bi, nc - 1 - i, 0)
    fs = pl.BlockSpec((None, L, c), fwd)
    bs = pl.BlockSpec((None, L, c), bwd)
    return pl.pallas_call(
        _wkv_scan_kernel,
        grid=(b, nc),
        in_specs=[fs] * 6 + [bs] * 6,
        out_specs=[fs, bs],
        out_shape=[jax.ShapeDtypeStruct((b, s, c), BF16)] * 2,
        scratch_shapes=[pltpu.VMEM((2, c // LANES, LANES, LANES), F32)],
        compiler_params=_params(("parallel", "arbitrary")),
        name="wkv_scan",
    )(lw0, r, v, kk, ka0, kd0, lw1, r, v, kk, ka1, kd1)


def _post_kernel(x_ref, oa_ref, yf_ref, yb_ref, g_ref, bv_ref, zg_ref, lnw_ref, lnb_ref,
                 woa_ref, wor_ref, wout_ref, n2w_ref, wq_ref, x1_ref, xn_ref, qp_ref):
    y = yf_ref[...].astype(F32) + yb_ref[...].astype(F32)
    c = y.shape[1]
    e, et = _seg_mats(c, HEAD_DIM)
    mu = _seg_sum(y, e, et, signed=True) * (1.0 / HEAD_DIM)
    dlt = y - mu
    var = _seg_sum(dlt * dlt, e, et) * (1.0 / HEAD_DIM)
    yn = dlt * lax.rsqrt(var + RWKV_LN_EPS) * lnw_ref[...] + lnb_ref[...]
    out = ((yn + bv_ref[...].astype(F32)) * g_ref[...].astype(F32)).astype(BF16)
    h_rwkv = _dot(out, wor_ref[...])
    h_attn = _dot(oa_ref[...], woa_ref[...])
    zg = zg_ref[...].astype(F32)
    merged = _sigmoid(zg[:, 0:c]) * h_attn + _sigmoid(zg[:, c:2 * c]) * h_rwkv
    x1 = x_ref[...] + _dot(merged.astype(BF16), wout_ref[...])
    x1_ref[...] = x1
    ms = jnp.mean(x1 * x1, axis=-1, keepdims=True)
    xn = (x1 * lax.rsqrt(ms + NORM_EPS) * n2w_ref[...]).astype(BF16)
    xn_ref[...] = xn
    qp = _dot(xn, wq_ref[...]).astype(BF16)
    for j in range(qp_ref.shape[0]):
        qp_ref[j] = qp[:, j * LANES:(j + 1) * LANES]


def _post(x, oa, yf, yb, g, bv, zg, lnw, lnb, woa, wor, wout, n2w, wq, tm):
    t, d = x.shape
    nq = wq.shape[1] // LANES
    tok = lambda i: (i, 0)
    const = lambda i: (0, 0)
    tspec = lambda w: pl.BlockSpec((tm, w), tok)
    cspec = lambda a: pl.BlockSpec(a.shape, const)
    return pl.pallas_call(
        _post_kernel,
        grid=(t // tm,),
        in_specs=[tspec(d), tspec(d), tspec(d), tspec(d), tspec(d), tspec(d), tspec(2 * d),
                  cspec(lnw), cspec(lnb), cspec(woa), cspec(wor), cspec(wout), cspec(n2w), cspec(wq)],
        out_specs=[tspec(d), tspec(d), pl.BlockSpec((nq, tm, LANES), lambda i: (0, i, 0))],
        out_shape=[jax.ShapeDtypeStruct((t, d), F32), jax.ShapeDtypeStruct((t, d), BF16),
                   jax.ShapeDtypeStruct((nq, t, LANES), BF16)],
        compiler_params=_params(("parallel",)),
        name="post_merge",
    )(x, oa, yf, yb, g, bv, zg, lnw, lnb, woa, wor, wout, n2w, wq)


SUBLANES = 8


def _sort_pairs(n):
    def merge(lo, hi, r):
        step = r * 2
        if step < hi - lo:
            yield from merge(lo, hi, step)
            yield from merge(lo + r, hi, step)
            for i in range(lo + r, hi - r, step):
                yield (i, i + r)
        else:
            yield (lo, lo + r)

    def sort(lo, hi):
        if hi - lo >= 1:
            mid = lo + (hi - lo) // 2
            yield from sort(lo, mid)
            yield from sort(mid + 1, hi)
            yield from merge(lo, hi, 1)

    return list(sort(0, n - 1))


def _exchange(xs, i, j):
    a, b = xs[i], xs[j]
    if b is None:
        return
    if a is None:
        xs[i], xs[j] = b, None
        return
    xs[i], xs[j] = jnp.maximum(a, b), jnp.minimum(a, b)


def _top_sorted(blocks, k):
    xs = list(blocks)
    for i, j in _sort_pairs(k):
        _exchange(xs, i, j)
    shift = SUBLANES // 2
    while shift >= 1:
        ys = [None if x is None else pltpu.roll(x, shift, 0) for x in xs]
        zs = []
        for i in range(k):
            a, b = xs[i], ys[k - 1 - i]
            zs.append(b if a is None else (a if b is None else jnp.maximum(a, b)))
        step = k // 2
        while step >= 1:
            for i in range(k):
                if i & step == 0:
                    _exchange(zs, i, i + step)
            step //= 2
        xs = zs
        shift //= 2
    return [x[0:1, :] for x in xs]


def _row_blocks(x):
    return [x[r:r + SUBLANES, :] for r in range(0, x.shape[0], SUBLANES)]


def _peer_select(s0, s1, k):
    a = _top_sorted(_row_blocks(s0), k)
    b = _top_sorted(_row_blocks(s1), k)
    amat = jnp.concatenate(a, axis=0)
    bmat = jnp.concatenate(b, axis=0)
    h8 = k // 2
    cand = ([a[0] + bmat[0:h8], a[0] + bmat[h8:k], a[1] + bmat[0:h8], amat[h8:k] + b[0]]
            + [a[i] + bmat[0:h8] for i in range(2, h8)])
    best = _top_sorted(cand + [None] * (k - len(cand)), k)
    z = jnp.zeros_like(best[0])
    for val in best:
        z = z + jnp.exp(val - best[0])
    return amat, bmat, best[k - 1], z


def _peer_kernel(xn_ref, qp_ref, sk_ref, u_ref, vt_ref, x1_ref, o_ref,
                 cnt_ref, e0_ref, rank_ref, e1_ref, acc_ref):
    jb = pl.program_id(1)
    tt = xn_ref.shape[0]
    K = PEER_TOPK
    rows_per_blk = u_ref.shape[0] // PEER_NKEYS

    @pl.when(jb == 0)
    def _():
        acc_ref[...] = jnp.zeros(acc_ref.shape, F32)

        def head_body(h, carry):
            s0 = _dot_nt(sk_ref[2 * h], qp_ref[2 * h])
            s1 = _dot_nt(sk_ref[2 * h + 1], qp_ref[2 * h + 1])

            amat, bmat, thr, z = _peer_select(s0, s1, K)
            a0 = amat[0:1]
            top_cnt = jnp.zeros(amat.shape, F32)
            for kk in range(K):
                top_cnt = top_cnt + jnp.where(amat + bmat[kk:kk + 1] >= thr, 1.0, 0.0)
            cnt = jnp.zeros(s0.shape, F32)
            rank = jnp.full(s1.shape, float(PEER_NKEYS), F32)
            for kk in reversed(range(K)):
                cnt = jnp.where(s0 == amat[kk:kk + 1], top_cnt[kk:kk + 1], cnt)
                rank = jnp.where(s1 == bmat[kk:kk + 1], float(kk + 1), rank)
            cnt_ref[h] = cnt
            e0_ref[h] = jnp.exp(s0 - a0) / z
            rank_ref[h] = rank.astype(BF16)
            e1_ref[h] = jnp.exp(s1 - bmat[0:1]).astype(BF16)
            return carry

        lax.fori_loop(0, PEER_HEADS, head_body, 0)

    sub = PEER_SUB_ROWS * PEER_NKEYS
    xn = xn_ref[...]
    n_sub = u_ref.shape[0] // sub
    hpres = [None] * n_sub
    hpres[0] = _dot_nt(u_ref[0:sub, :], xn)
    zero = jnp.zeros((), BF16)
    hgs = []
    for c in range(n_sub):
        if c + 1 < n_sub:
            hpres[c + 1] = _dot_nt(u_ref[(c + 1) * sub:(c + 2) * sub, :], xn)
        g_rows = []
        for il in range(c * PEER_SUB_ROWS, (c + 1) * PEER_SUB_ROWS):
            i = jb * rows_per_blk + il
            acc = None
            for h in range(PEER_HEADS):
                crow = cnt_ref[h, pl.ds(i, 1), :].astype(BF16)
                e0row = e0_ref[h, pl.ds(i, 1), :].astype(BF16)
                contrib = jnp.where(rank_ref[h] <= crow, e0row * e1_ref[h], zero)
                acc = contrib if acc is None else acc + contrib
            g_rows.append(acc)
        gates = jnp.concatenate(g_rows, axis=0)
        hpre = hpres[c]
        gelu = 0.5 * hpre * (1.0 + lax.erf(hpre * (2.0 ** -0.5)))
        hgs.append(gelu.astype(BF16) * gates)
        if len(hgs) == PEER_OUT_GROUP:
            lo = (c + 1 - PEER_OUT_GROUP) * sub
            acc_ref[...] += _dot(vt_ref[:, lo:(c + 1) * sub], jnp.concatenate(hgs, axis=0))
            hgs = []

    @pl.when(jb == pl.num_programs(1) - 1)
    def _():
        o_ref[...] = x1_ref[...] + acc_ref[...].T


def _peer(xn, qp, sk, u, vt, x1, tt, eb):
    t, d = xn.shape
    n_exp = u.shape[0]
    nq = qp.shape[0]
    return pl.pallas_call(
        _peer_kernel,
        grid=(t // tt, n_exp // eb),
        in_specs=[pl.BlockSpec((tt, d), lambda i, j: (i, 0)),
                  pl.BlockSpec((nq, tt, LANES), lambda i, j: (0, i, 0)),
                  pl.BlockSpec(sk.shape, lambda i, j: (0, 0, 0)),
                  pl.BlockSpec((eb, d), lambda i, j: (j, 0)),
                  pl.BlockSpec((d, eb), lambda i, j: (0, j)),
                  pl.BlockSpec((tt, d), lambda i, j: (i, 0))],
        out_specs=pl.BlockSpec((tt, d), lambda i, j: (i, 0)),
        out_shape=jax.ShapeDtypeStruct((t, d), F32),
        scratch_shapes=[pltpu.VMEM((PEER_HEADS, PEER_NKEYS, tt), F32),
                        pltpu.VMEM((PEER_HEADS, PEER_NKEYS, tt), F32),
                        pltpu.VMEM((PEER_HEADS, PEER_NKEYS, tt), BF16),
                        pltpu.VMEM((PEER_HEADS, PEER_NKEYS, tt), BF16),
                        pltpu.VMEM((d, tt), F32)],
        compiler_params=_params(("parallel", "arbitrary")),
        name="peer",
    )(xn, qp, sk, u, vt, x1)


def _rope_tables(s):
    inv = 1.0 / (ROPE_THETA ** (jnp.arange(0, HEAD_DIM, 2, dtype=F32) / HEAD_DIM))
    ang = jnp.arange(s, dtype=F32)[:, None] * inv[None, :]
    cos, sin = jnp.cos(ang), jnp.sin(ang)
    reps = LANES // HEAD_DIM
    cos_t = jnp.tile(jnp.concatenate([cos, cos], axis=1), (1, reps))
    sin_t = jnp.tile(jnp.concatenate([-sin, sin], axis=1), (1, reps))
    return cos_t, sin_t


def _tile(n, pref):
    return pref if n % pref == 0 else n


def _tiling(t, s):
    return dict(
        proj=_tile(t, 512),
        inproj_attn=_tile(s, 512),
        attn_q=_tile(s, 4096),
        attn_k=_tile(s, 2048),
        rwkv_prep=_tile(s, 256),
        post=_tile(t, 256),
        peer_tokens=_tile(t, 512),
        peer_experts=2048,
    )


def _trunk(x, P):
    b, s, d = x.shape
    t = b * s
    c = d
    tiles = _tiling(t, s)
    xf = x.reshape(t, d)
    zg = _norm_matmul(xf, P["norm1_w"], P["w_g"], tiles["proj"])

    cos_t, sin_t = _rope_tables(s)
    q, kt, va = _inproj_attn(x, P["norm1_w"], P["w_a"], cos_t, sin_t, P["q_norm_w"], P["k_norm_w"],
                             HEAD_DIM ** -0.5 * math.log2(math.e), tiles["inproj_attn"])
    oa = _attention(P["lam_rows"], q, kt, va, P["subln_w"], tiles["attn_q"], tiles["attn_k"])

    zr = _norm_matmul(xf, P["norm1_w"], P["w_r"], tiles["proj"]).reshape(b, s, -1)
    (r, v, kk, kd0, kd1, ka0, ka1, lw0, lw1, g, bv) = _rwkv_prep(
        zr, P["shift_mu"], P["decay_w0"], P["decay_w2"], P["iclr_a0"], P["iclr_a2"],
        P["gate_g2"], P["k_k"], P["k_a"], P["r_k"], c, tiles["rwkv_prep"])
    yf, yb = _wkv_scan(lw0, lw1, r, v, kk, ka0, ka1, kd0, kd1)

    flat = lambda a: a.reshape(t, a.shape[-1])
    x1, xn, qp = _post(xf, flat(oa), flat(yf), flat(yb), flat(g), flat(bv), zg, P["lnx_w"], P["lnx_b"],
                       P["w_o_attn"], P["w_o_rwkv"], P["w_out"], P["norm2_w"], P["peer_wq"], tiles["post"])
    y = _peer(xn, qp, P["peer_subkeys"], P["peer_u"], P["peer_vt"], x1, tiles["peer_tokens"],
              tiles["peer_experts"])
    return y.reshape(b, s, d)


def kernel(x_prompt, x_sample, norm1_w, w_in, q_norm_w, k_norm_w, lambda_q1, lambda_k1, lambda_q2, lambda_k2, subln_w, w_o_attn, shift_mu, decay_w0, decay_w2, iclr_a0, iclr_a2, gate_g2, k_k, k_a, r_k, lnx_w, lnx_b, w_o_rwkv, w_out, norm2_w, peer_wq, peer_subkeys, peer_u, peer_v):
    d = x_prompt.shape[-1]
    c = d
    att_cols = 3 * d
    rwkv_cols = shift_mu.shape[-1]
    l = 0
    w = w_in[l].astype(BF16)
    nsub = d // HEAD_DIM
    P = {
        "norm1_w": norm1_w[l][None, :],
        "w_a": w[:, :att_cols],
        "w_r": w[:, att_cols:att_cols + rwkv_cols],
        "w_g": w[:, att_cols + rwkv_cols:],
        "q_norm_w": jnp.tile(q_norm_w[l], nsub)[None, :],
        "k_norm_w": jnp.tile(k_norm_w[l], nsub)[None, :],
        "lam_rows": jnp.stack([lambda_q1[l], lambda_k1[l], lambda_q2[l], lambda_k2[l]]),
        "subln_w": subln_w[l][None, :],
        "w_o_attn": w_o_attn[l].astype(BF16),
        "shift_mu": shift_mu[l][None, :],
        "decay_w0": decay_w0[l],
        "decay_w2": decay_w2[l].reshape(-1, c).astype(BF16),
        "iclr_a0": iclr_a0[l],
        "iclr_a2": iclr_a2[l].reshape(-1, c).astype(BF16),
        "gate_g2": gate_g2[l].astype(BF16),
        "k_k": k_k[l][None, :],
        "k_a": k_a[l][None, :],
        "r_k": r_k[l].reshape(1, c),
        "lnx_w": lnx_w[l][None, :],
        "lnx_b": lnx_b[l][None, :],
        "w_o_rwkv": w_o_rwkv[l].astype(BF16),
        "w_out": w_out[l].astype(BF16),
        "norm2_w": norm2_w[l][None, :],
        "peer_wq": peer_wq[l].astype(BF16),
        "peer_subkeys": peer_subkeys[l].reshape(-1, PEER_NKEYS, peer_subkeys.shape[-1]).astype(BF16),
        "peer_u": peer_u[l].astype(BF16),
        "peer_vt": peer_v[l].astype(BF16).T,
    }
    return (_trunk(x_prompt, P), _trunk(x_sample, P))
```

```python
import functools
import math

import jax
import jax.numpy as jnp
from jax import lax
from jax.experimental import pallas as pl
from jax.experimental.pallas import tpu as pltpu

F32 = jnp.float32
BF16 = jnp.bfloat16

LANES = 128
HEAD_DIM = 64
ATT_HEADS = 8
NORM_EPS = 1e-6
RWKV_LN_EPS = 64e-5
ROPE_THETA = 10000.0
LAMBDA_INIT = 0.8 - 0.6 * math.exp(-0.3 * 0)
SCAN_CHUNK = 64
PEER_NKEYS = 128
PEER_TOPK = 16
PEER_HEADS = 8
PEER_SUB_ROWS = 2
PEER_OUT_GROUP = 8
VMEM_LIMIT = 56 * 1024 * 1024


def _params(sem):
    return pltpu.CompilerParams(dimension_semantics=sem, vmem_limit_bytes=VMEM_LIMIT)


def _dot(a, b):
    return jnp.dot(a, b, preferred_element_type=F32)


def _dot_nt(a, b):
    return lax.dot_general(a, b, (((1,), (1,)), ((), ())), preferred_element_type=F32)


def _split_dot(x, m):
    hi = x.astype(BF16)
    lo = (x - hi.astype(F32)).astype(BF16)
    return _dot(hi, m) + _dot(lo, m)


def _seg_mats(width, seg):
    lane = lax.broadcasted_iota(jnp.int32, (width, LANES), 0)
    grp = lax.broadcasted_iota(jnp.int32, (width, LANES), 1)
    e = jnp.where(lane // seg == grp, 1.0, 0.0).astype(BF16)
    grp_t = lax.broadcasted_iota(jnp.int32, (LANES, width), 0)
    lane_t = lax.broadcasted_iota(jnp.int32, (LANES, width), 1)
    et = jnp.where(lane_t // seg == grp_t, 1.0, 0.0).astype(BF16)
    return e, et


def _seg_sum(x, e, et, signed=False):
    totals = _split_dot(x, e) if signed else _dot(x.astype(BF16), e)
    return _split_dot(totals, et)


def _sigmoid(x):
    return 1.0 / (1.0 + jnp.exp(-x))


def _norm_matmul_kernel(x_ref, nw_ref, w_ref, o_ref):
    x = x_ref[...]
    ms = jnp.mean(x * x, axis=-1, keepdims=True)
    h = (x * lax.rsqrt(ms + NORM_EPS) * nw_ref[...]).astype(BF16)
    o_ref[...] = _dot(h, w_ref[...]).astype(o_ref.dtype)


def _norm_matmul(x, nw, w, tm):
    t, d = x.shape
    n = w.shape[1]
    return pl.pallas_call(
        _norm_matmul_kernel,
        grid=(t // tm,),
        in_specs=[pl.BlockSpec((tm, d), lambda i: (i, 0)),
                  pl.BlockSpec((1, d), lambda i: (0, 0)),
                  pl.BlockSpec((d, n), lambda i: (0, 0))],
        out_specs=pl.BlockSpec((tm, n), lambda i: (i, 0)),
        out_shape=jax.ShapeDtypeStruct((t, n), BF16),
        compiler_params=_params(("parallel",)),
        name="norm_matmul",
    )(x, nw, w)


def _norm_rope(x, w, c, s, e, et, scale):
    width = x.shape[1]
    ms = _seg_sum(x * x, e, et) * (1.0 / HEAD_DIM)
    y = x * lax.rsqrt(ms + NORM_EPS) * w
    lane = lax.broadcasted_iota(jnp.int32, y.shape, 1)
    half = HEAD_DIM // 2
    partner = jnp.where(lane % HEAD_DIM < half, pltpu.roll(y, width - half, 1), pltpu.roll(y, half, 1))
    return (y * c + partner * s) * scale


def _inproj_attn_kernel(x_ref, nw_ref, w_ref, cos_ref, sin_ref, qw_ref, kw_ref, q_ref, kt_ref, v_ref, *, q_scale):
    x = x_ref[...]
    ms = jnp.mean(x * x, axis=-1, keepdims=True)
    h = (x * lax.rsqrt(ms + NORM_EPS) * nw_ref[...]).astype(BF16)
    z = _dot(h, w_ref[...])
    width = q_ref.shape[1]
    reps = width // LANES
    e, et = _seg_mats(width, HEAD_DIM)
    c = jnp.concatenate([cos_ref[...]] * reps, axis=1)
    s = jnp.concatenate([sin_ref[...]] * reps, axis=1)
    q_ref[...] = _norm_rope(z[:, 0:width], qw_ref[...], c, s, e, et, q_scale).astype(q_ref.dtype)
    k = _norm_rope(z[:, width:2 * width], kw_ref[...], c, s, e, et, 1.0)
    for j in range(reps):
        kt_ref[j * LANES:(j + 1) * LANES, :] = k[:, j * LANES:(j + 1) * LANES].T.astype(kt_ref.dtype)
    v_ref[...] = z[:, 2 * width:3 * width].astype(v_ref.dtype)


def _inproj_attn(x, nw, w_a, cos_t, sin_t, qw, kw, q_scale, tm):
    b, s, d = x.shape
    width = qw.shape[1]
    tok = lambda bi, i: (bi, i, 0)
    const = lambda bi, i: (0, 0)
    return pl.pallas_call(
        functools.partial(_inproj_attn_kernel, q_scale=q_scale),
        grid=(b, s // tm),
        in_specs=[pl.BlockSpec((None, tm, d), tok),
                  pl.BlockSpec((1, d), const),
                  pl.BlockSpec(w_a.shape, const),
                  pl.BlockSpec((tm, LANES), lambda bi, i: (i, 0)),
                  pl.BlockSpec((tm, LANES), lambda bi, i: (i, 0)),
                  pl.BlockSpec((1, width), const),
                  pl.BlockSpec((1, width), const)],
        out_specs=[pl.BlockSpec((None, tm, width), tok),
                   pl.BlockSpec((None, width, tm), lambda bi, i: (bi, 0, i)),
                   pl.BlockSpec((None, tm, width), tok)],
        out_shape=[jax.ShapeDtypeStruct((b, s, width), BF16),
                   jax.ShapeDtypeStruct((b, width, s), BF16),
                   jax.ShapeDtypeStruct((b, s, width), BF16)],
        compiler_params=_params(("parallel", "parallel")),
        name="inproj_attn",
    )(x, nw, w_a, cos_t, sin_t, qw, kw)


ATTN_ROW_CHUNK = 512


def _attn_kernel(lam_ref, q_ref, kt_ref, v_ref, sw_ref, o_ref, qs_ref, m_ref, acc_ref, s0_ref, *, tq, tk):
    q = q_ref[...]
    lane = lax.broadcasted_iota(jnp.int32, q.shape, 1)
    zero = jnp.zeros_like(q)
    qs_ref[0:tq, :] = jnp.where(lane < HEAD_DIM, q, zero)
    qs_ref[tq:2 * tq, :] = jnp.where(lane < HEAD_DIM, zero, q)
    m_ref[...] = jnp.full(m_ref.shape, -jnp.inf, F32)
    acc_ref[...] = jnp.zeros(acc_ref.shape, F32)

    rc = min(ATTN_ROW_CHUNK, 2 * tq)
    n_chunks = 2 * tq // rc
    n_tiles = kt_ref.shape[1] // tk
    s0_ref[...] = _dot(qs_ref[0:rc, :], kt_ref[:, 0:tk])

    def key_tile(j, carry):
        off = pl.multiple_of(j * tk, tk)
        v = v_ref[pl.ds(off, tk), :]
        v_ones = jnp.concatenate([v, jnp.ones_like(v)], axis=1)
        kt = kt_ref[:, pl.ds(off, tk)]
        scores = [None] * n_chunks
        scores[0] = s0_ref[...]
        for c in range(n_chunks):
            rows = slice(c * rc, (c + 1) * rc)
            if c + 1 < n_chunks:
                scores[c + 1] = _dot(qs_ref[(c + 1) * rc:(c + 2) * rc, :], kt)
            else:
                off_next = pl.multiple_of(jnp.minimum(j + 1, n_tiles - 1) * tk, tk)
                s0_ref[...] = _dot(qs_ref[0:rc, :], kt_ref[:, pl.ds(off_next, tk)])
            s = scores[c]
            m_prev = m_ref[rows, :]
            m_new = jnp.maximum(m_prev, jnp.max(s, axis=1, keepdims=True))
            alpha = jnp.exp2(m_prev - m_new)
            p = jnp.exp2((s - jnp.concatenate([m_new] * (tk // LANES), axis=1)).astype(BF16))
            acc_ref[rows, :] = (jnp.concatenate([alpha, alpha], axis=1) * acc_ref[rows, :]
                                + _dot(p, v_ones))
            m_ref[rows, :] = m_new
        return carry

    lax.fori_loop(0, n_tiles, key_tile, 0)

    lv = lam_ref[...]
    lam = (jnp.exp(jnp.sum(lv[0:1] * lv[1:2], axis=1, keepdims=True))
           - jnp.exp(jnp.sum(lv[2:3] * lv[3:4], axis=1, keepdims=True)) + LAMBDA_INIT)
    acc = acc_ref[...]
    o = (acc[0:tq, 0:LANES] / acc[0:tq, LANES:2 * LANES]
         - lam * (acc[tq:2 * tq, 0:LANES] / acc[tq:2 * tq, LANES:2 * LANES]))
    ms = jnp.mean(o * o, axis=-1, keepdims=True)
    o = o * lax.rsqrt(ms + NORM_EPS) * sw_ref[...] * (1.0 - LAMBDA_INIT)
    o_ref[...] = o.astype(o_ref.dtype)


def _attention(lam_rows, q, kt, v, subln_w, tq, tk):
    b, s, width = q.shape
    heads = width // LANES
    return pl.pallas_call(
        functools.partial(_attn_kernel, tq=tq, tk=tk),
        grid=(b, heads, s // tq),
        in_specs=[pl.BlockSpec((4, HEAD_DIM), lambda bi, h, qi: (0, 0)),
                  pl.BlockSpec((None, tq, LANES), lambda bi, h, qi: (bi, qi, h)),
                  pl.BlockSpec((None, LANES, s), lambda bi, h, qi: (bi, h, 0)),
                  pl.BlockSpec((None, s, LANES), lambda bi, h, qi: (bi, 0, h)),
                  pl.BlockSpec((1, LANES), lambda bi, h, qi: (0, 0))],
        out_specs=pl.BlockSpec((None, tq, LANES), lambda bi, h, qi: (bi, qi, h)),
        out_shape=jax.ShapeDtypeStruct((b, s, width), BF16),
        scratch_shapes=[pltpu.VMEM((2 * tq, LANES), BF16),
                        pltpu.VMEM((2 * tq, LANES), F32),
                        pltpu.VMEM((2 * tq, 2 * LANES), F32),
                        pltpu.VMEM((min(ATTN_ROW_CHUNK, 2 * tq), tk), F32)],
        compiler_params=_params(("parallel", "parallel", "arbitrary")),
        name="diff_attention",
    )(lam_rows, q, kt, v, subln_w)


HALO = 16


def _rwkv_prep_kernel(z_ref, zp_ref, zn_ref, mu_ref, w0_ref, w2_ref, a0_ref, a2_ref, g2_ref,
                      kk_w_ref, ka_w_ref, rk_ref,
                      r_ref, v_ref, kk_ref, kd0_ref, kd1_ref, ka0_ref, ka1_ref, lw0_ref, lw1_ref,
                      g_ref, bv_ref, *, c):
    i = pl.program_id(1)
    z = z_ref[...].astype(F32)
    tm = z.shape[0]
    row = lax.broadcasted_iota(jnp.int32, z.shape, 0)
    prev_row = jnp.where(i == 0, 0.0, zp_ref[HALO - 1:HALO, :].astype(F32))
    next_row = jnp.where(i == pl.num_programs(1) - 1, 0.0, zn_ref[0:1, :].astype(F32))
    z_up = jnp.where(row == 0, prev_row, pltpu.roll(z, 1, 0))
    z_dn = jnp.where(row == tm - 1, next_row, pltpu.roll(z, tm - 1, 0))
    zs = z + mu_ref[...] * (0.5 * (z_up + z_dn) - z)

    r = zs[:, 0:c]
    k = zs[:, c:2 * c]
    v = zs[:, 2 * c:3 * c]
    wl = zs[:, 3 * c:3 * c + LANES]
    al = zs[:, 3 * c + LANES:3 * c + 2 * LANES]
    gl = zs[:, 3 * c + 2 * LANES:3 * c + 3 * LANES]

    lane = lax.broadcasted_iota(jnp.int32, wl.shape, 1)
    twl = jnp.tanh(wl)
    e, et = _seg_mats(c, HEAD_DIM)

    kk = k * kk_w_ref[...]
    kk = kk * lax.rsqrt(_seg_sum(kk * kk, e, et) + 1e-12)
    kk_ref[...] = kk.astype(kk_ref.dtype)
    r_ref[...] = r.astype(r_ref.dtype)
    v_ref[...] = v.astype(v_ref.dtype)
    g_ref[...] = _dot(_sigmoid(gl).astype(BF16), g2_ref[...]).astype(g_ref.dtype)

    ksum = jnp.zeros_like(k)
    for d, (kd_ref, ka_ref, lw_ref) in enumerate(((kd0_ref, ka0_ref, lw0_ref), (kd1_ref, ka1_ref, lw1_ref))):
        sel = (lane < HEAD_DIM) if d == 0 else (lane >= HEAD_DIM)
        wpre = _dot(jnp.where(sel, twl, 0.0).astype(BF16), w2_ref[...]) + w0_ref[d:d + 1, :]
        lw_ref[...] = (-math.exp(-0.5)) * _sigmoid(wpre)
        a = _sigmoid(_dot(jnp.where(sel, al, 0.0).astype(BF16), a2_ref[...]) + a0_ref[d:d + 1, :])
        kd = k * (1.0 + (a - 1.0) * ka_w_ref[...])
        kd_ref[...] = kd.astype(kd_ref.dtype)
        ka_ref[...] = (kk * a).astype(ka_ref.dtype)
        ksum = ksum + kd
    bonus = _seg_sum(r * ksum * rk_ref[...], e, et, signed=True)
    bv_ref[...] = (bonus * v).astype(bv_ref.dtype)


def _rwkv_prep(zr, mu, w0, w2, a0, a2, g2, kk_w, ka_w, rk, c, tm):
    b, s, cols = zr.shape
    nh = tm // HALO
    last = s // HALO - 1
    tok = lambda bi, i: (bi, i, 0)
    const = lambda bi, i: (0, 0)
    out = lambda dt: jax.ShapeDtypeStruct((b, s, c), dt)
    ospec = pl.BlockSpec((None, tm, c), tok)
    return pl.pallas_call(
        functools.partial(_rwkv_prep_kernel, c=c),
        grid=(b, s // tm),
        in_specs=[pl.BlockSpec((None, tm, cols), tok),
                  pl.BlockSpec((None, HALO, cols), lambda bi, i: (bi, jnp.maximum(i * nh - 1, 0), 0)),
                  pl.BlockSpec((None, HALO, cols), lambda bi, i: (bi, jnp.minimum((i + 1) * nh, last), 0)),
                  pl.BlockSpec((1, cols), const),
                  pl.BlockSpec((2, c), const),
                  pl.BlockSpec((LANES, c), const),
                  pl.BlockSpec((2, c), const),
                  pl.BlockSpec((LANES, c), const),
                  pl.BlockSpec((LANES, c), const),
                  pl.BlockSpec((1, c), const),
                  pl.BlockSpec((1, c), const),
                  pl.BlockSpec((1, c), const)],
        out_specs=[ospec] * 11,
        out_shape=[out(BF16)] * 7 + [out(F32)] * 2 + [out(BF16)] * 2,
        compiler_params=_params(("parallel", "parallel")),
        name="rwkv_prep",
    )(zr, zr, zr, mu, w0, w2, a0, a2, g2, kk_w, ka_w, rk)


def _wkv_scan_kernel(lwf_ref, rf_ref, vf_ref, kkf_ref, kaf_ref, kdf_ref,
                     lwb_ref, rb_ref, vb_ref, kkb_ref, kab_ref, kdb_ref,
                     yf_ref, yb_ref, st_ref):
    L = SCAN_CHUNK
    P2 = 2 * L
    ci = pl.program_id(1)

    @pl.when(ci == 0)
    def _():
        st_ref[...] = jnp.zeros(st_ref.shape, F32)

    row = lax.broadcasted_iota(jnp.int32, (P2, P2), 0)
    col = lax.broadcasted_iota(jnp.int32, (P2, P2), 1)
    same_head = (row // L) == (col // L)
    rt = row % L
    ct = col % L
    eye = row == col
    ti = lax.broadcasted_iota(jnp.int32, (L, L), 0)
    tj = lax.broadcasted_iota(jnp.int32, (L, L), 1)
    lane_a = lax.broadcasted_iota(jnp.int32, (L, LANES), 1) < HEAD_DIM

    def expand(x):
        return jnp.concatenate([jnp.where(lane_a, x, 0.0), jnp.where(lane_a, 0.0, x)], axis=0)

    units = []
    dirs = ((lwf_ref, rf_ref, vf_ref, kkf_ref, kaf_ref, kdf_ref, yf_ref),
            (lwb_ref, rb_ref, vb_ref, kkb_ref, kab_ref, kdb_ref, yb_ref))
    for d, (lw_ref, r_ref, v_ref, kk_ref, ka_ref, kd_ref, y_ref) in enumerate(dirs):
        if d == 0:
            m_strict = same_head & (rt > ct)
            m_incl = same_head & (rt >= ct)
            tri = jnp.where(tj <= ti, 1.0, 0.0).astype(BF16)
        else:
            m_strict = same_head & (rt < ct)
            m_incl = same_head & (rt <= ct)
            tri = jnp.where(tj >= ti, 1.0, 0.0).astype(BF16)
        m_level = []
        for lvl in range(L.bit_length() - 1):
            joined = same_head & ((rt >> (lvl + 1)) == (ct >> (lvl + 1)))
            r_bit = (rt >> lvl) & 1
            c_bit = (ct >> lvl) & 1
            later_row = (r_bit == 1) & (c_bit == 0) if d == 0 else (r_bit == 0) & (c_bit == 1)
            m_level.append(joined & later_row)
        lw = lw_ref[...]
        hi = lw.astype(BF16)
        r1 = lw - hi.astype(F32)
        mid = r1.astype(BF16)
        lo = (r1 - mid.astype(F32)).astype(BF16)
        cum = _dot(tri, hi) + _dot(tri, mid) + _dot(tri, lo)
        cum_last = cum[L - 1:L, :] if d == 0 else cum[0:1, :]
        ec = jnp.exp(cum)
        enc = jnp.exp(-cum)
        p_last = jnp.exp(cum_last)
        a_t = -kk_ref[...].astype(F32) * jnp.exp(cum - lw)
        r_t = r_ref[...].astype(F32) * ec
        b_t = ka_ref[...].astype(F32) * enc
        k_t = kd_ref[...].astype(F32) * enc
        b_h = b_t * p_last
        k_h = k_t * p_last
        vv = v_ref[...].astype(F32)

        for p in range(lw.shape[1] // LANES):
            sl = slice(p * LANES, (p + 1) * LANES)
            units.append(dict(
                d=d, p=p, sl=sl, y_ref=y_ref, m_strict=m_strict, m_incl=m_incl, m_level=m_level,
                ea=expand(a_t[:, sl]), er=expand(r_t[:, sl]), vexp=expand(vv[:, sl]),
                bk2=jnp.concatenate([b_t[:, sl]] * 2 + [k_t[:, sl]] * 2, axis=0).astype(BF16),
                bh=b_h[:, sl], kh=k_h[:, sl], p_last=p_last[:, sl]))

    for u in units:
        g = _dot_nt(jnp.concatenate([u["ea"], u["er"]], axis=0).astype(BF16), u["bk2"])
        u["xab"] = jnp.where(u["m_strict"], g[0:P2, 0:P2], 0.0)
        u["xak"] = jnp.where(u["m_strict"], g[0:P2, P2:2 * P2], 0.0).astype(BF16)
        u["xr"] = jnp.concatenate([jnp.where(u["m_incl"], g[P2:2 * P2, 0:P2], 0.0),
                                   jnp.where(u["m_incl"], g[P2:2 * P2, P2:2 * P2], 0.0)], axis=1).astype(BF16)
        u["tinv"] = jnp.where(eye, 1.0, jnp.where(u["m_level"][0], u["xab"], 0.0))
    for lvl in range(1, L.bit_length() - 1):
        for u in units:
            xoff = jnp.where(u["m_level"][lvl], u["xab"], 0.0).astype(BF16)
            u["tx"] = _dot(u["tinv"].astype(BF16), xoff).astype(BF16)
        for u in units:
            u["tinv"] = u["tinv"] + _dot(u["tx"], u["tinv"].astype(BF16))
    for u in units:
        u["xv"] = _dot(u["xak"], u["vexp"].astype(BF16))
    for u in units:
        u["wu0"] = _dot(u["tinv"].astype(BF16), jnp.concatenate([u["ea"], u["xv"]], axis=1).astype(BF16))
        u["bkt"] = jnp.concatenate([expand(u["bh"]).T, expand(u["kh"]).T], axis=1).astype(BF16)
        u["p_col"] = jnp.broadcast_to(u["p_last"], (P2, LANES)).T
    for u in units:
        u["st"] = st_ref[u["d"], u["p"]]
        u["ws"] = _dot(jnp.concatenate([u["wu0"][:, 0:LANES], u["er"]], axis=0).astype(BF16), u["st"].astype(BF16))
    for u in units:
        uu = u["ws"][0:P2] + u["wu0"][:, LANES:2 * LANES]
        u["uv"] = jnp.concatenate([uu, u["vexp"]], axis=0).astype(BF16)
    for u in units:
        yexp = u["ws"][P2:2 * P2] + _dot(u["xr"], u["uv"])
        u["y_ref"][:, u["sl"]] = (yexp[0:L] + yexp[L:P2]).astype(u["y_ref"].dtype)
    for u in units:
        st_ref[u["d"], u["p"]] = u["p_col"] * u["st"] + _dot(u["bkt"], u["uv"])


def _wkv_scan(lw0, lw1, r, v, kk, ka0, ka1, kd0, kd1):
    b, s, c = r.shape
    L = SCAN_CHUNK
    nc = s // L
    fwd = lambda bi, i: (bi, i, 0)
    bwd = lambda bi, i: (bi, nc - 1 - i, 0)
    fs = pl.BlockSpec((None, L, c), fwd)
    bs = pl.BlockSpec((None, L, c), bwd)
    return pl.pallas_call(
        _wkv_scan_kernel,
        grid=(b, nc),
        in_specs=[fs] * 6 + [bs] * 6,
        out_specs=[fs, bs],
        out_shape=[jax.ShapeDtypeStruct((b, s, c), F32)] * 2,
        scratch_shapes=[pltpu.VMEM((2, c // LANES, LANES, LANES), F32)],
        compiler_params=_params(("parallel", "arbitrary")),
        name="wkv_scan",
    )(lw0, r, v, kk, ka0, kd0, lw1, r, v, kk, ka1, kd1)


def _post_kernel(x_ref, oa_ref, yf_ref, yb_ref, g_ref, bv_ref, zg_ref, lnw_ref, lnb_ref,
                 woa_ref, wor_ref, wout_ref, n2w_ref, wq_ref, x1_ref, xn_ref, qp_ref):
    y = yf_ref[...] + yb_ref[...]
    c = y.shape[1]
    e, et = _seg_mats(c, HEAD_DIM)
    mu = _seg_sum(y, e, et, signed=True) * (1.0 / HEAD_DIM)
    dlt = y - mu
    var = _seg_sum(dlt * dlt, e, et) * (1.0 / HEAD_DIM)
    yn = dlt * lax.rsqrt(var + RWKV_LN_EPS) * lnw_ref[...] + lnb_ref[...]
    out = ((yn + bv_ref[...].astype(F32)) * g_ref[...].astype(F32)).astype(BF16)
    h_rwkv = _dot(out, wor_ref[...])
    h_attn = _dot(oa_ref[...], woa_ref[...])
    zg = zg_ref[...].astype(F32)
    merged = _sigmoid(zg[:, 0:c]) * h_attn + _sigmoid(zg[:, c:2 * c]) * h_rwkv
    x1 = x_ref[...] + _dot(merged.astype(BF16), wout_ref[...])
    x1_ref[...] = x1
    ms = jnp.mean(x1 * x1, axis=-1, keepdims=True)
    xn = (x1 * lax.rsqrt(ms + NORM_EPS) * n2w_ref[...]).astype(BF16)
    xn_ref[...] = xn
    qp = _dot(xn, wq_ref[...]).astype(BF16)
    for j in range(qp_ref.shape[0]):
        qp_ref[j] = qp[:, j * LANES:(j + 1) * LANES]


def _post(x, oa, yf, yb, g, bv, zg, lnw, lnb, woa, wor, wout, n2w, wq, tm):
    t, d = x.shape
    nq = wq.shape[1] // LANES
    tok = lambda i: (i, 0)
    const = lambda i: (0, 0)
    tspec = lambda w: pl.BlockSpec((tm, w), tok)
    cspec = lambda a: pl.BlockSpec(a.shape, const)
    return pl.pallas_call(
        _post_kernel,
        grid=(t // tm,),
        in_specs=[tspec(d), tspec(d), tspec(d), tspec(d), tspec(d), tspec(d), tspec(2 * d),
                  cspec(lnw), cspec(lnb), cspec(woa), cspec(wor), cspec(wout), cspec(n2w), cspec(wq)],
        out_specs=[tspec(d), tspec(d), pl.BlockSpec((nq, tm, LANES), lambda i: (0, i, 0))],
        out_shape=[jax.ShapeDtypeStruct((t, d), F32), jax.ShapeDtypeStruct((t, d), BF16),
                   jax.ShapeDtypeStruct((nq, t, LANES), BF16)],
        compiler_params=_params(("parallel",)),
        name="post_merge",
    )(x, oa, yf, yb, g, bv, zg, lnw, lnb, woa, wor, wout, n2w, wq)


SUBLANES = 8


def _sort_pairs(n):
    def merge(lo, hi, r):
        step = r * 2
        if step < hi - lo:
            yield from merge(lo, hi, step)
            yield from merge(lo + r, hi, step)
            for i in range(lo + r, hi - r, step):
                yield (i, i + r)
        else:
            yield (lo, lo + r)

    def sort(lo, hi):
        if hi - lo >= 1:
            mid = lo + (hi - lo) // 2
            yield from sort(lo, mid)
            yield from sort(mid + 1, hi)
            yield from merge(lo, hi, 1)

    return list(sort(0, n - 1))


def _exchange(xs, i, j):
    a, b = xs[i], xs[j]
    if b is None:
        return
    if a is None:
        xs[i], xs[j] = b, None
        return
    xs[i], xs[j] = jnp.maximum(a, b), jnp.minimum(a, b)


def _top_sorted(blocks, k):
    xs = list(blocks)
    for i, j in _sort_pairs(k):
        _exchange(xs, i, j)
    shift = SUBLANES // 2
    while shift >= 1:
        ys = [None if x is None else pltpu.roll(x, shift, 0) for x in xs]
        zs = []
        for i in range(k):
            a, b = xs[i], ys[k - 1 - i]
            zs.append(b if a is None else (a if b is None else jnp.maximum(a, b)))
        step = k // 2
        while step >= 1:
            for i in range(k):
                if i & step == 0:
                    _exchange(zs, i, i + step)
            step //= 2
        xs = zs
        shift //= 2
    return [x[0:1, :] for x in xs]


def _row_blocks(x):
    return [x[r:r + SUBLANES, :] for r in range(0, x.shape[0], SUBLANES)]


def _peer_select(s0, s1, k):
    a = _top_sorted(_row_blocks(s0), k)
    b = _top_sorted(_row_blocks(s1), k)
    amat = jnp.concatenate(a, axis=0)
    bmat = jnp.concatenate(b, axis=0)
    h8 = k // 2
    cand = ([a[0] + bmat[0:h8], a[0] + bmat[h8:k], a[1] + bmat[0:h8], amat[h8:k] + b[0]]
            + [a[i] + bmat[0:h8] for i in range(2, h8)])
    best = _top_sorted(cand + [None] * (k - len(cand)), k)
    z = jnp.zeros_like(best[0])
    for val in best:
        z = z + jnp.exp(val - best[0])
    return amat, bmat, best[k - 1], z


def _peer_kernel(xn_ref, qp_ref, sk_ref, u_ref, vt_ref, x1_ref, o_ref,
                 cnt_ref, e0_ref, rank_ref, e1_ref, acc_ref):
    jb = pl.program_id(1)
    tt = xn_ref.shape[0]
    K = PEER_TOPK
    rows_per_blk = u_ref.shape[0] // PEER_NKEYS

    @pl.when(jb == 0)
    def _():
        acc_ref[...] = jnp.zeros(acc_ref.shape, F32)

        def head_body(h, carry):
            s0 = _dot_nt(sk_ref[2 * h], qp_ref[2 * h])
            s1 = _dot_nt(sk_ref[2 * h + 1], qp_ref[2 * h + 1])

            amat, bmat, thr, z = _peer_select(s0, s1, K)
            a0 = amat[0:1]
            top_cnt = jnp.zeros(amat.shape, F32)
            for kk in range(K):
                top_cnt = top_cnt + jnp.where(amat + bmat[kk:kk + 1] >= thr, 1.0, 0.0)
            cnt = jnp.zeros(s0.shape, F32)
            rank = jnp.full(s1.shape, float(PEER_NKEYS), F32)
            for kk in reversed(range(K)):
                cnt = jnp.where(s0 == amat[kk:kk + 1], top_cnt[kk:kk + 1], cnt)
                rank = jnp.where(s1 == bmat[kk:kk + 1], float(kk + 1), rank)
            cnt_ref[h] = cnt
            e0_ref[h] = jnp.exp(s0 - a0) / z
            rank_ref[h] = rank.astype(BF16)
            e1_ref[h] = jnp.exp(s1 - bmat[0:1]).astype(BF16)
            return carry

        lax.fori_loop(0, PEER_HEADS, head_body, 0)

    sub = PEER_SUB_ROWS * PEER_NKEYS
    xn = xn_ref[...]
    n_sub = u_ref.shape[0] // sub
    hpres = [None] * n_sub
    hpres[0] = _dot_nt(u_ref[0:sub, :], xn)
    zero = jnp.zeros((), BF16)
    hgs = []
    for c in range(n_sub):
        if c + 1 < n_sub:
            hpres[c + 1] = _dot_nt(u_ref[(c + 1) * sub:(c + 2) * sub, :], xn)
        g_rows = []
        for il in range(c * PEER_SUB_ROWS, (c + 1) * PEER_SUB_ROWS):
            i = jb * rows_per_blk + il
            acc = None
            for h in range(PEER_HEADS):
                crow = cnt_ref[h, pl.ds(i, 1), :].astype(BF16)
                e0row = e0_ref[h, pl.ds(i, 1), :].astype(BF16)
                contrib = jnp.where(rank_ref[h] <= crow, e0row * e1_ref[h], zero)
                acc = contrib if acc is None else acc + contrib
            g_rows.append(acc)
        gates = jnp.concatenate(g_rows, axis=0)
        hpre = hpres[c]
        gelu = 0.5 * hpre * (1.0 + lax.erf(hpre * (2.0 ** -0.5)))
        hgs.append(gelu.astype(BF16) * gates)
        if len(hgs) == PEER_OUT_GROUP:
            lo = (c + 1 - PEER_OUT_GROUP) * sub
            acc_ref[...] += _dot(vt_ref[:, lo:(c + 1) * sub], jnp.concatenate(hgs, axis=0))
            hgs = []

    @pl.when(jb == pl.num_programs(1) - 1)
    def _():
        o_ref[...] = x1_ref[...] + acc_ref[...].T


def _peer(xn, qp, sk, u, vt, x1, tt, eb):
    t, d = xn.shape
    n_exp = u.shape[0]
    nq = qp.shape[0]
    return pl.pallas_call(
        _peer_kernel,
        grid=(t // tt, n_exp // eb),
        in_specs=[pl.BlockSpec((tt, d), lambda i, j: (i, 0)),
                  pl.BlockSpec((nq, tt, LANES), lambda i, j: (0, i, 0)),
                  pl.BlockSpec(sk.shape, lambda i, j: (0, 0, 0)),
                  pl.BlockSpec((eb, d), lambda i, j: (j, 0)),
                  pl.BlockSpec((d, eb), lambda i, j: (0, j)),
                  pl.BlockSpec((tt, d), lambda i, j: (i, 0))],
        out_specs=pl.BlockSpec((tt, d), lambda i, j: (i, 0)),
        out_shape=jax.ShapeDtypeStruct((t, d), F32),
        scratch_shapes=[pltpu.VMEM((PEER_HEADS, PEER_NKEYS, tt), F32),
                        pltpu.VMEM((PEER_HEADS, PEER_NKEYS, tt), F32),
                        pltpu.VMEM((PEER_HEADS, PEER_NKEYS, tt), BF16),
                        pltpu.VMEM((PEER_HEADS, PEER_NKEYS, tt), BF16),
                        pltpu.VMEM((d, tt), F32)],
        compiler_params=_params(("parallel", "arbitrary")),
        name="peer",
    )(xn, qp, sk, u, vt, x1)


def _rope_tables(s):
    inv = 1.0 / (ROPE_THETA ** (jnp.arange(0, HEAD_DIM, 2, dtype=F32) / HEAD_DIM))
    ang = jnp.arange(s, dtype=F32)[:, None] * inv[None, :]
    cos, sin = jnp.cos(ang), jnp.sin(ang)
    reps = LANES // HEAD_DIM
    cos_t = jnp.tile(jnp.concatenate([cos, cos], axis=1), (1, reps))
    sin_t = jnp.tile(jnp.concatenate([-sin, sin], axis=1), (1, reps))
    return cos_t, sin_t


def _tile(n, pref):
    return pref if n % pref == 0 else n


def _tiling(t, s):
    return dict(
        proj=_tile(t, 512),
        inproj_attn=_tile(s, 512),
        attn_q=_tile(s, 4096),
        attn_k=_tile(s, 2048),
        rwkv_prep=_tile(s, 256),
        post=_tile(t, 256),
        peer_tokens=_tile(t, 512),
        peer_experts=2048,
    )


def _trunk(x, P):
    b, s, d = x.shape
    t = b * s
    c = d
    tiles = _tiling(t, s)
    xf = x.reshape(t, d)
    zg = _norm_matmul(xf, P["norm1_w"], P["w_g"], tiles["proj"])

    cos_t, sin_t = _rope_tables(s)
    q, kt, va = _inproj_attn(x, P["norm1_w"], P["w_a"], cos_t, sin_t, P["q_norm_w"], P["k_norm_w"],
                             HEAD_DIM ** -0.5 * math.log2(math.e), tiles["inproj_attn"])
    oa = _attention(P["lam_rows"], q, kt, va, P["subln_w"], tiles["attn_q"], tiles["attn_k"])

    zr = _norm_matmul(xf, P["norm1_w"], P["w_r"], tiles["proj"]).reshape(b, s, -1)
    (r, v, kk, kd0, kd1, ka0, ka1, lw0, lw1, g, bv) = _rwkv_prep(
        zr, P["shift_mu"], P["decay_w0"], P["decay_w2"], P["iclr_a0"], P["iclr_a2"],
        P["gate_g2"], P["k_k"], P["k_a"], P["r_k"], c, tiles["rwkv_prep"])
    yf, yb = _wkv_scan(lw0, lw1, r, v, kk, ka0, ka1, kd0, kd1)

    flat = lambda a: a.reshape(t, a.shape[-1])
    x1, xn, qp = _post(xf, flat(oa), flat(yf), flat(yb), flat(g), flat(bv), zg, P["lnx_w"], P["lnx_b"],
                       P["w_o_attn"], P["w_o_rwkv"], P["w_out"], P["norm2_w"], P["peer_wq"], tiles["post"])
    y = _peer(xn, qp, P["peer_subkeys"], P["peer_u"], P["peer_vt"], x1, tiles["peer_tokens"],
              tiles["peer_experts"])
    return y.reshape(b, s, d)


def kernel(x_prompt, x_sample, norm1_w, w_in, q_norm_w, k_norm_w, lambda_q1, lambda_k1, lambda_q2, lambda_k2, subln_w, w_o_attn, shift_mu, decay_w0, decay_w2, iclr_a0, iclr_a2, gate_g2, k_k, k_a, r_k, lnx_w, lnx_b, w_o_rwkv, w_out, norm2_w, peer_wq, peer_subkeys, peer_u, peer_v):
    d = x_prompt.shape[-1]
    c = d
    att_cols = 3 * d
    rwkv_cols = shift_mu.shape[-1]
    l = 0
    w = w_in[l].astype(BF16)
    nsub = d // HEAD_DIM
    P = {
        "norm1_w": norm1_w[l][None, :],
        "w_a": w[:, :att_cols],
        "w_r": w[:, att_cols:att_cols + rwkv_cols],
        "w_g": w[:, att_cols + rwkv_cols:],
        "q_norm_w": jnp.tile(q_norm_w[l], nsub)[None, :],
        "k_norm_w": jnp.tile(k_norm_w[l], nsub)[None, :],
        "lam_rows": jnp.stack([lambda_q1[l], lambda_k1[l], lambda_q2[l], lambda_k2[l]]),
        "subln_w": subln_w[l][None, :],
        "w_o_attn": w_o_attn[l].astype(BF16),
        "shift_mu": shift_mu[l][None, :],
        "decay_w0": decay_w0[l],
        "decay_w2": decay_w2[l].reshape(-1, c).astype(BF16),
        "iclr_a0": iclr_a0[l],
        "iclr_a2": iclr_a2[l].reshape(-1, c).astype(BF16),
        "gate_g2": gate_g2[l].astype(BF16),
        "k_k": k_k[l][None, :],
        "k_a": k_a[l][None, :],
        "r_k": r_k[l].reshape(1, c),
        "lnx_w": lnx_w[l][None, :],
        "lnx_b": lnx_b[l][None, :],
        "w_o_rwkv": w_o_rwkv[l].astype(BF16),
        "w_out": w_out[l].astype(BF16),
        "norm2_w": norm2_w[l][None, :],
        "peer_wq": peer_wq[l].astype(BF16),
        "peer_subkeys": peer_subkeys[l].reshape(-1, PEER_NKEYS, peer_subkeys.shape[-1]).astype(BF16),
        "peer_u": peer_u[l].astype(BF16),
        "peer_vt": peer_v[l].astype(BF16).T,
    }
    if x_prompt.shape[1:] == x_sample.shape[1:]:
        nb = x_prompt.shape[0]
        y = _trunk(jnp.concatenate([x_prompt, x_sample], axis=0), P)
        return (y[:nb], y[nb:])
    return (_trunk(x_prompt, P), _trunk(x_sample, P))
```
